```python
import math
import jax, jax.numpy as jnp
from jax import lax
import numpy as np

D_MODEL = 2048
BATCH = 8
SEQ = 2048
DEPTH = 4

MIX_DIM = D_MODEL
RWKV_HEAD = 64
RWKV_DIM = MIX_DIM // 2
RWKV_HEADS = RWKV_DIM // RWKV_HEAD
RWKV_DECAY_LORA = max(32, int(round(1.8 * RWKV_DIM ** 0.5 / 32)) * 32)
RWKV_AAA_LORA = max(32, int(round(1.8 * RWKV_DIM ** 0.5 / 32)) * 32)
RWKV_MV_LORA = max(32, int(round(0.6 * RWKV_DIM ** 0.5 / 32)) * 32)
RWKV_GATE_LORA = max(32, int(round(0.6 * RWKV_DIM ** 0.8 / 32)) * 32)
RWKV_GN_EPS = RWKV_HEAD * 1e-5
RWKV_L2_EPS = 1e-12

GDN_HEAD = 128
GDN_V_DIM = MIX_DIM - RWKV_DIM
GDN_V_HEADS = GDN_V_DIM // GDN_HEAD
GDN_QK_HEADS = GDN_V_HEADS // 2
GDN_QK_DIM = GDN_QK_HEADS * GDN_HEAD
GDN_CONV = 4
GDN_CHUNK = 64
GDN_L2_EPS = 1e-6
GDN_NORM_EPS = 1e-6

D_FF = -(-8 * D_MODEL // (3 * 256)) * 256
NORM_EPS = 1e-5

N_RWKV_IN = 3 * RWKV_DIM + RWKV_DECAY_LORA + RWKV_AAA_LORA + RWKV_GATE_LORA
N_GDN_IN = 2 * GDN_QK_DIM + 2 * GDN_V_DIM + 2 * GDN_V_HEADS
N_IN = N_RWKV_IN + N_GDN_IN

kernel_name = 'hybrid_rwkv7_gdn_trunk'


def split_cols(p, sizes):
    offsets = np.cumsum(np.array(sizes))[:-1].tolist()
    return jnp.split(p, offsets, axis=-1)


def rms_norm(x, w, eps=NORM_EPS):
    x32 = x.astype(jnp.float32)
    y = x32 * lax.rsqrt(jnp.mean(x32 * x32, axis=-1, keepdims=True) + eps)
    return (y * w.astype(jnp.float32)).astype(x.dtype)


def l2_normalize(x, eps):
    x32 = x.astype(jnp.float32)
    return x32 * lax.rsqrt(jnp.sum(x32 * x32, axis=-1, keepdims=True) + eps)


def token_shift_lerp(p, mu):
    prev = jnp.pad(p, ((0, 0), (1, 0), (0, 0)))[:, :-1]
    return p + (prev - p) * mu


def causal_depthwise_conv(x, w):
    K, C = w.shape
    return lax.conv_general_dilated(
        x, w[:, None, :].astype(x.dtype), window_strides=(1,), padding=((K - 1, 0),),
        dimension_numbers=('NWC', 'WIO', 'NWC'), feature_group_count=C)


def rwkv7_recurrence(r, w, k, v, a, b):
    B, T, H, N = r.shape

    def step(S, inp):
        r_t, w_t, k_t, v_t, a_t, b_t = inp
        Sa = jnp.einsum('bhvk,bhk->bhv', S, a_t)
        S = S * w_t[:, :, None, :] + Sa[..., None] * b_t[:, :, None, :] + v_t[..., None] * k_t[:, :, None, :]
        return S, jnp.einsum('bhvk,bhk->bhv', S, r_t)

    xs = tuple(jnp.moveaxis(t, 1, 0) for t in (r, w, k, v, a, b))
    S0 = jnp.zeros((B, H, N, N), jnp.float32)
    _, y = lax.scan(step, S0, xs)
    return jnp.moveaxis(y, 0, 1)


def rwkv7_mixer(p, vres, v_first, mu, w0, w_up, a0, a_up, g_up, k_k, k_a, r_k, ln_w, ln_b):
    B, T, _ = p.shape
    H, N = RWKV_HEADS, RWKV_HEAD
    p = token_shift_lerp(p.astype(jnp.float32), mu)
    r, k, v, wd, ad, gd = split_cols(
        p, (RWKV_DIM, RWKV_DIM, RWKV_DIM, RWKV_DECAY_LORA, RWKV_AAA_LORA, RWKV_GATE_LORA))
    w = -jax.nn.softplus(-(w0 + jnp.tanh(wd) @ w_up)) - 0.5
    decay = jnp.exp(-jnp.exp(w))
    if vres is None:
        v_first = v
    else:
        vd, v_up, v0 = vres
        v = v + (v_first - v) * jax.nn.sigmoid(v0 + vd @ v_up)
    a = jax.nn.sigmoid(a0 + ad @ a_up)
    g = jax.nn.sigmoid(gd) @ g_up
    heads = lambda t: t.reshape(B, T, H, N)
    kk = l2_normalize(heads(k * k_k), RWKV_L2_EPS)
    k = k * (1.0 + (a - 1.0) * k_a)
    r, k, v, a, decay = map(heads, (r, k, v, a, decay))
    y = rwkv7_recurrence(r, decay, k, v, -kk, kk * a)
    mean = jnp.mean(y, axis=-1, keepdims=True)
    var = jnp.mean(jnp.square(y - mean), axis=-1, keepdims=True)
    y = ((y - mean) * lax.rsqrt(var + RWKV_GN_EPS)).reshape(B, T, RWKV_DIM) * ln_w + ln_b
    bonus = (jnp.sum(r * k * r_k, axis=-1, keepdims=True) * v).reshape(B, T, RWKV_DIM)
    return (y + bonus) * g, v_first


def gated_delta_rule_chunked(q, k, v, g, beta):
    B, T, H, dk = q.shape
    dv = v.shape[-1]
    C = GDN_CHUNK
    NC = T // C
    q = l2_normalize(q, GDN_L2_EPS) * (dk ** -0.5)
    k = l2_normalize(k, GDN_L2_EPS)

    def to_chunks(t):
        t = t.reshape((B, NC, C, H) + t.shape[3:])
        return jnp.moveaxis(t, 3, 1)

    q, k, v, g, beta = map(to_chunks, (q, k, v, g, beta))
    g = jnp.cumsum(g, axis=-1)
    idx = jnp.arange(C)
    causal = idx[:, None] >= idx[None, :]
    strict = idx[:, None] > idx[None, :]
    gamma = jnp.exp(jnp.where(causal, g[..., :, None] - g[..., None, :], -jnp.inf))
    k_beta = k * beta[..., None]
    v_beta = v * beta[..., None]
    L = jnp.where(strict, jnp.einsum('bhnid,bhnjd->bhnij', k_beta, k) * gamma, 0.0)
    eye = jnp.eye(C, dtype=L.dtype)
    rhs = jnp.concatenate([v_beta, k_beta * jnp.exp(g)[..., None]], axis=-1)
    sol = lax.linalg.triangular_solve(L + eye, rhs, left_side=True, lower=True, unit_diagonal=True)
    u, w = sol[..., :dv], sol[..., dv:]
    A_qk = jnp.einsum('bhnid,bhnjd->bhnij', q, k) * gamma
    q_g = q * jnp.exp(g)[..., None]
    g_last = g[..., -1]
    k_g = k * jnp.exp(g_last[..., None] - g)[..., None]

    def step(S, inp):
        u_i, w_i, qg_i, kg_i, A_i, gl_i = inp
        v_new = u_i - jnp.einsum('bhck,bhkv->bhcv', w_i, S)
        o = jnp.einsum('bhck,bhkv->bhcv', qg_i, S) + jnp.einsum('bhij,bhjv->bhiv', A_i, v_new)
        S = S * jnp.exp(gl_i)[..., None, None] + jnp.einsum('bhck,bhcv->bhkv', kg_i, v_new)
        return S, o

    xs = tuple(jnp.moveaxis(t, 2, 0) for t in (u, w, q_g, k_g, A_qk, g_last))
    S0 = jnp.zeros((B, H, dk, dv), jnp.float32)
    _, o = lax.scan(step, S0, xs)
    return o.transpose(1, 0, 3, 2, 4).reshape(B, T, H, dv)


def gdn_mixer(p, conv_w, A_log, dt_bias, norm_w):
    B, T, _ = p.shape
    p = p.astype(jnp.float32)
    qkv, z, a, b = split_cols(p, (2 * GDN_QK_DIM + GDN_V_DIM, GDN_V_DIM, GDN_V_HEADS, GDN_V_HEADS))
    qkv = jax.nn.silu(causal_depthwise_conv(qkv, conv_w.astype(jnp.float32)))
    q, k, v = split_cols(qkv, (GDN_QK_DIM, GDN_QK_DIM, GDN_V_DIM))
    rep = GDN_V_HEADS // GDN_QK_HEADS
    q = jnp.repeat(q.reshape(B, T, GDN_QK_HEADS, GDN_HEAD), rep, axis=2)
    k = jnp.repeat(k.reshape(B, T, GDN_QK_HEADS, GDN_HEAD), rep, axis=2)
    v = v.reshape(B, T, GDN_V_HEADS, GDN_HEAD)
    g = -jnp.exp(A_log) * jax.nn.softplus(a + dt_bias)
    beta = jax.nn.sigmoid(b)
    o = gated_delta_rule_chunked(q, k, v, g, beta)
    o = o * lax.rsqrt(jnp.mean(o * o, axis=-1, keepdims=True) + GDN_NORM_EPS) * norm_w
    o = o * jax.nn.silu(z.reshape(B, T, GDN_V_HEADS, GDN_HEAD))
    return o.reshape(B, T, GDN_V_DIM)


def setup_inputs(seed: int = 0) -> dict:
    key = jax.random.key(seed)
    ks = iter(jax.random.split(key, 40))
    L, D, Lv = DEPTH, D_MODEL, DEPTH - 1

    def nrm(shape, scale):
        return jax.random.normal(next(ks), shape, jnp.float32) * scale

    def unif(shape, lo, hi):
        return jax.random.uniform(next(ks), shape, jnp.float32, lo, hi)

    dt = jnp.exp(unif((L, GDN_V_HEADS), math.log(1e-3), math.log(1e-1)))
    return {
        'x': nrm((BATCH, SEQ, D), 1.0),
        'attn_norm_w': 1.0 + nrm((L, D), 0.05),
        'w_in': nrm((L, D, N_IN), D ** -0.5),
        'rwkv_mu': unif((L, N_RWKV_IN), 0.0, 1.0),
        'rwkv_w0': unif((L, RWKV_DIM), -6.0, -0.5),
        'rwkv_w_up': nrm((L, RWKV_DECAY_LORA, RWKV_DIM), 0.1 * RWKV_DECAY_LORA ** -0.5),
        'rwkv_a0': nrm((L, RWKV_DIM), 0.1),
        'rwkv_a_up': nrm((L, RWKV_AAA_LORA, RWKV_DIM), 0.5 * RWKV_AAA_LORA ** -0.5),
        'rwkv_g_up': nrm((L, RWKV_GATE_LORA, RWKV_DIM), RWKV_GATE_LORA ** -0.5),
        'rwkv_k_k': 0.85 + nrm((L, RWKV_DIM), 0.05),
        'rwkv_k_a': 1.0 + nrm((L, RWKV_DIM), 0.05),
        'rwkv_r_k': -0.04 + nrm((L, RWKV_HEADS, RWKV_HEAD), 0.02),
        'rwkv_ln_w': 1.0 + nrm((L, RWKV_DIM), 0.05),
        'rwkv_ln_b': nrm((L, RWKV_DIM), 0.02),
        'vres_down': nrm((Lv, D, RWKV_MV_LORA), D ** -0.5),
        'vres_mu': unif((Lv, RWKV_MV_LORA), 0.0, 1.0),
        'vres_up': nrm((Lv, RWKV_MV_LORA, RWKV_DIM), 0.5 * RWKV_MV_LORA ** -0.5),
        'vres_v0': 1.0 + nrm((Lv, RWKV_DIM), 0.1),
        'gdn_conv_w': nrm((L, GDN_CONV, 2 * GDN_QK_DIM + GDN_V_DIM), 0.5),
        'gdn_A_log': jnp.log(unif((L, GDN_V_HEADS), 1.0, 16.0)),
        'gdn_dt_bias': jnp.log(jnp.expm1(dt)),
        'gdn_norm_w': 1.0 + nrm((L, GDN_HEAD), 0.05),
        'w_out': nrm((L, MIX_DIM, D), MIX_DIM ** -0.5),
        'ffn_norm_w': 1.0 + nrm((L, D), 0.05),
        'ffn_w_gate': nrm((L, D, D_FF), D ** -0.5),
        'ffn_w_up': nrm((L, D, D_FF), D ** -0.5),
        'ffn_w_down': nrm((L, D_FF, D), D_FF ** -0.5),
        'final_norm_w': 1.0 + nrm((D,), 0.05),
    }


def reference(x, attn_norm_w, w_in, rwkv_mu, rwkv_w0, rwkv_w_up, rwkv_a0, rwkv_a_up, rwkv_g_up,
              rwkv_k_k, rwkv_k_a, rwkv_r_k, rwkv_ln_w, rwkv_ln_b, vres_down, vres_mu, vres_up, vres_v0,
              gdn_conv_w, gdn_A_log, gdn_dt_bias, gdn_norm_w, w_out, ffn_norm_w, ffn_w_gate, ffn_w_up,
              ffn_w_down, final_norm_w):
    v_first = None
    for l in range(DEPTH):
        h = rms_norm(x, attn_norm_w[l])
        if l == 0:
            w_cat = w_in[0]
        else:
            w_cat = jnp.concatenate([w_in[l], vres_down[l - 1]], axis=1)
        proj = h @ w_cat
        p_rwkv = proj[..., :N_RWKV_IN]
        p_gdn = proj[..., N_RWKV_IN:N_IN]
        if l == 0:
            vres = None
        else:
            vd = token_shift_lerp(proj[..., N_IN:].astype(jnp.float32), vres_mu[l - 1])
            vres = (vd, vres_up[l - 1], vres_v0[l - 1])
        y_a, v_first = rwkv7_mixer(p_rwkv, vres, v_first, rwkv_mu[l], rwkv_w0[l], rwkv_w_up[l],
                                   rwkv_a0[l], rwkv_a_up[l], rwkv_g_up[l], rwkv_k_k[l], rwkv_k_a[l],
                                   rwkv_r_k[l], rwkv_ln_w[l], rwkv_ln_b[l])
        y_b = gdn_mixer(p_gdn, gdn_conv_w[l], gdn_A_log[l], gdn_dt_bias[l], gdn_norm_w[l])
        y = jnp.concatenate([y_a, y_b], axis=-1).astype(x.dtype)
        x = x + y @ w_out[l]
        h = rms_norm(x, ffn_norm_w[l])
        x = x + (jax.nn.silu(h @ ffn_w_gate[l]) * (h @ ffn_w_up[l])) @ ffn_w_down[l]
    return rms_norm(x, final_norm_w)
```

```python
import functools

import jax
import jax.numpy as jnp
from jax import lax
from jax.experimental import pallas as pl
from jax.experimental.pallas import tpu as pltpu

F32 = jnp.float32
BF16 = jnp.bfloat16

D_MODEL = 2048
RWKV_HEAD = 64
RWKV_DIM = 1024
RWKV_DECAY_LORA = 64
RWKV_AAA_LORA = 64
RWKV_MV_LORA = 32
RWKV_GATE_LORA = 160
RWKV_GN_EPS = RWKV_HEAD * 1e-5
RWKV_L2_EPS = 1e-12
GDN_HEAD = 128
GDN_V_DIM = 1024
GDN_V_HEADS = 8
GDN_QK_HEADS = 4
GDN_QK_DIM = 512
GDN_CONV = 4
GDN_L2_EPS = 1e-6
GDN_NORM_EPS = 1e-6
D_FF = 5632
NORM_EPS = 1e-5

LANES = 128
SUBLANES = 8
CHUNK = 64
PAIR = 2 * CHUNK
VMEM_LIMIT = 56 * 1024 * 1024

COL_RKV = 0
COL_GQK = 3 * RWKV_DIM
COL_GV = COL_GQK + 2 * GDN_QK_DIM
COL_GZ = COL_GV + GDN_V_DIM
COL_SM = COL_GZ + GDN_V_DIM
SM_WD, SM_AD, SM_GD, SM_VD, SM_AB = 0, 128, 256, 512, 640
SM_W = 768
N_PROJ = COL_SM + SM_W


def _mm(a, b):
    return jnp.dot(a.astype(BF16), b.astype(BF16), preferred_element_type=F32)


def _mm_nt(a, b):
    return lax.dot_general(a.astype(BF16), b.astype(BF16), (((1,), (1,)), ((), ())),
                           preferred_element_type=F32)


def _mm_tn(a, b):
    return lax.dot_general(a.astype(BF16), b.astype(BF16), (((0,), (0,)), ((), ())),
                           preferred_element_type=F32)


def _split3(x):
    hi = x.astype(BF16)
    r1 = x - hi.astype(F32)
    mid = r1.astype(BF16)
    lo = (r1 - mid.astype(F32)).astype(BF16)
    return hi, mid, lo


def _mm_exact_lhs(a01, x):
    n = x.shape[1]
    hi, mid, lo = _split3(x)
    y = jnp.dot(a01, jnp.concatenate([hi, mid, lo], axis=1), preferred_element_type=F32)
    return y[:, :n] + y[:, n:2 * n] + y[:, 2 * n:]


def _mm_exact_rhs(x, b01):
    m = x.shape[0]
    hi, mid, lo = _split3(x)
    y = jnp.dot(jnp.concatenate([hi, mid, lo], axis=0), b01, preferred_element_type=F32)
    return y[:m] + y[m:2 * m] + y[2 * m:]


def _sigmoid(x):
    return 1.0 / (1.0 + jnp.exp(-x))


def _softplus(x):
    return jnp.maximum(x, 0.0) + jnp.log(1.0 + jnp.exp(-jnp.abs(x)))


def _silu(x):
    return x * _sigmoid(x)


def _iota2(shape):
    return (lax.broadcasted_iota(jnp.int32, shape, 0), lax.broadcasted_iota(jnp.int32, shape, 1))


def _unit_lower_inverse(n_mat):
    sz = n_mat.shape[0]
    i, j = _iota2((sz, sz))
    eye = (i == j).astype(F32)
    d1 = jnp.where((i // 8) == (j // 8), n_mat, 0.0)
    d2 = _mm(d1, d1)
    d4 = _mm(d2, d2)
    p = eye + d1
    p = p + _mm(p, d2)
    p = p + _mm(p, d4)
    for s in (8, 16, 32):
        off = ((i // (2 * s)) == (j // (2 * s))) & ((i // s) == (j // s) + 1)
        m_off = jnp.where(off, n_mat, 0.0)
        p = p + _mm(p, _mm(m_off, p))
    return p


def _norm_matmul_kernel(x_ref, nw_ref, w_ref, o_ref, h_ref):
    @pl.when(pl.program_id(1) == 0)
    def _():
        x = x_ref[...]
        ms = jnp.mean(x * x, axis=-1, keepdims=True)
        h_ref[...] = (x * lax.rsqrt(ms + NORM_EPS) * nw_ref[...]).astype(BF16)

    o_ref[...] = jnp.dot(h_ref[...], w_ref[...], preferred_element_type=F32)


def _norm_matmul(x, nw, w, tm, tn):
    m, d = x.shape
    n = w.shape[1]
    return pl.pallas_call(
        _norm_matmul_kernel,
        grid=(m // tm, n // tn),
        in_specs=[pl.BlockSpec((tm, d), lambda i, j: (i, 0)),
                  pl.BlockSpec((1, d), lambda i, j: (0, 0)),
                  pl.BlockSpec((d, tn), lambda i, j: (0, j))],
        out_specs=pl.BlockSpec((tm, tn), lambda i, j: (i, j)),
        out_shape=jax.ShapeDtypeStruct((m, n), F32),
        scratch_shapes=[pltpu.VMEM((tm, d), BF16)],
        compiler_params=pltpu.CompilerParams(
            dimension_semantics=("parallel", "arbitrary"), vmem_limit_bytes=VMEM_LIMIT),
        name="norm_in_proj",
    )(x, nw, w)


def _norm_swiglu_kernel(x_ref, nw_ref, wg_ref, wu_ref, o_ref, h_ref):
    @pl.when(pl.program_id(1) == 0)
    def _():
        x = x_ref[...]
        ms = jnp.mean(x * x, axis=-1, keepdims=True)
        h_ref[...] = (x * lax.rsqrt(ms + NORM_EPS) * nw_ref[...]).astype(BF16)

    h = h_ref[...]
    g = jnp.dot(h, wg_ref[...], preferred_element_type=F32)
    u = jnp.dot(h, wu_ref[...], preferred_element_type=F32)
    o_ref[...] = (_silu(g) * u).astype(BF16)


def _norm_swiglu(x, nw, wg, wu, tm, tn):
    m, d = x.shape
    n = wg.shape[1]
    return pl.pallas_call(
        _norm_swiglu_kernel,
        grid=(m // tm, n // tn),
        in_specs=[pl.BlockSpec((tm, d), lambda i, j: (i, 0)),
                  pl.BlockSpec((1, d), lambda i, j: (0, 0)),
                  pl.BlockSpec((d, tn), lambda i, j: (0, j)),
                  pl.BlockSpec((d, tn), lambda i, j: (0, j))],
        out_specs=pl.BlockSpec((tm, tn), lambda i, j: (i, j)),
        out_shape=jax.ShapeDtypeStruct((m, n), BF16),
        scratch_shapes=[pltpu.VMEM((tm, d), BF16)],
        compiler_params=pltpu.CompilerParams(
            dimension_semantics=("parallel", "arbitrary"), vmem_limit_bytes=VMEM_LIMIT),
        name="norm_swiglu",
    )(x, nw, wg, wu)


def _matmul_res_kernel(a_ref, w_ref, r_ref, o_ref):
    o_ref[...] = r_ref[...] + jnp.dot(a_ref[...], w_ref[...], preferred_element_type=F32)


def _matmul_res(a, w, res, tm, tn):
    m, k = a.shape
    n = w.shape[1]
    return pl.pallas_call(
        _matmul_res_kernel,
        grid=(m // tm, n // tn),
        in_specs=[pl.BlockSpec((tm, k), lambda i, j: (i, 0)),
                  pl.BlockSpec((k, tn), lambda i, j: (0, j)),
                  pl.BlockSpec((tm, tn), lambda i, j: (i, j))],
        out_specs=pl.BlockSpec((tm, tn), lambda i, j: (i, j)),
        out_shape=jax.ShapeDtypeStruct((m, n), F32),
        compiler_params=pltpu.CompilerParams(
            dimension_semantics=("parallel", "arbitrary"), vmem_limit_bytes=VMEM_LIMIT),
        name="matmul_residual",
    )(a, w, res)


def _matmul2_res_kernel(a_ref, b_ref, wa_ref, wb_ref, r_ref, o_ref):
    acc = jnp.dot(a_ref[...], wa_ref[...], preferred_element_type=F32)
    acc = acc + jnp.dot(b_ref[...], wb_ref[...], preferred_element_type=F32)
    o_ref[...] = r_ref[...] + acc


def _matmul2_res(a, b, w, res, tm, tn):
    m, k = a.shape
    n = w.shape[1]
    return pl.pallas_call(
        _matmul2_res_kernel,
        grid=(m // tm, n // tn),
        in_specs=[pl.BlockSpec((tm, k), lambda i, j: (i, 0)),
                  pl.BlockSpec((tm, k), lambda i, j: (i, 0)),
                  pl.BlockSpec((k, tn), lambda i, j: (0, j)),
                  pl.BlockSpec((k, tn), lambda i, j: (1, j)),
                  pl.BlockSpec((tm, tn), lambda i, j: (i, j))],
        out_specs=pl.BlockSpec((tm, tn), lambda i, j: (i, j)),
        out_shape=jax.ShapeDtypeStruct((m, n), F32),
        compiler_params=pltpu.CompilerParams(
            dimension_semantics=("parallel", "arbitrary"), vmem_limit_bytes=VMEM_LIMIT),
        name="out_proj_residual",
    )(a, b, w, w, res)


def _rmsnorm_kernel(x_ref, nw_ref, o_ref):
    x = x_ref[...]
    ms = jnp.mean(x * x, axis=-1, keepdims=True)
    o_ref[...] = x * lax.rsqrt(ms + NORM_EPS) * nw_ref[...]


def _rmsnorm(x, nw, tm):
    m, d = x.shape
    return pl.pallas_call(
        _rmsnorm_kernel,
        grid=(m // tm,),
        in_specs=[pl.BlockSpec((tm, d), lambda i: (i, 0)),
                  pl.BlockSpec((1, d), lambda i: (0, 0))],
        out_specs=pl.BlockSpec((tm, d), lambda i: (i, 0)),
        out_shape=jax.ShapeDtypeStruct((m, d), F32),
        compiler_params=pltpu.CompilerParams(dimension_semantics=("parallel",)),
        name="final_rmsnorm",
    )(x, nw)


def _shifted(ext_ref, p, n_back, first_block):
    tb = p.shape[0]

    @pl.when(first_block)
    def _():
        ext_ref[0:SUBLANES, :] = jnp.zeros((SUBLANES, p.shape[1]), F32)

    ext_ref[SUBLANES:SUBLANES + tb, :] = p
    views = [ext_ref[SUBLANES - s:SUBLANES - s + tb, :] for s in range(1, n_back + 1)]
    tail = ext_ref[tb:tb + SUBLANES, :]
    return views, tail


def _rwkv_prep_kernel(has_vres, *refs):
    if has_vres:
        (rkv_ref, sm_ref, vf_ref, mu_rkv_ref, mu_sm_ref, w0_ref, wup_ref, a0_ref, aup_ref, gup_ref,
         kk_ref, ka_ref, vup_ref, v0_ref,
         r_o, k_o, v_o, lw_o, kr_o, eta_o, g_o, ext_rkv, ext_sm) = refs
    else:
        (rkv_ref, sm_ref, mu_rkv_ref, mu_sm_ref, w0_ref, wup_ref, a0_ref, aup_ref, gup_ref,
         kk_ref, ka_ref,
         r_o, k_o, v_o, lw_o, kr_o, eta_o, g_o, ext_rkv, ext_sm) = refs
    first = pl.program_id(1) == 0

    p = rkv_ref[...]
    (prev,), tail = _shifted(ext_rkv, p, 1, first)
    sh = p + (prev - p) * mu_rkv_ref[...]
    ext_rkv[0:SUBLANES, :] = tail

    ps = sm_ref[...]
    (prev_s,), tail_s = _shifted(ext_sm, ps, 1, first)
    shs = ps + (prev_s - ps) * mu_sm_ref[...]
    ext_sm[0:SUBLANES, :] = tail_s

    r = sh[:, 0:RWKV_DIM]
    k = sh[:, RWKV_DIM:2 * RWKV_DIM]
    v = sh[:, 2 * RWKV_DIM:3 * RWKV_DIM]
    wd = shs[:, SM_WD:SM_WD + 128]
    ad = shs[:, SM_AD:SM_AD + 128]
    gd = shs[:, SM_GD:SM_GD + 256]

    w = -_softplus(-(w0_ref[...] + _mm(jnp.tanh(wd), wup_ref[...]))) - 0.5
    lw_o[...] = -jnp.exp(w)
    eta = _sigmoid(a0_ref[...] + _mm(ad, aup_ref[...]))
    g_o[...] = _mm(_sigmoid(gd), gup_ref[...])
    if has_vres:
        vd = shs[:, SM_VD:SM_VD + 128]
        v = v + (vf_ref[...] - v) * _sigmoid(v0_ref[...] + _mm(vd, vup_ref[...]))
    r_o[...] = r
    v_o[...] = v
    kr_o[...] = k * kk_ref[...]
    k_o[...] = k * (1.0 + (eta - 1.0) * ka_ref[...])
    eta_o[...] = eta


def _rwkv_prep(proj, v_first, lp, batch, seq, tb):
    m = proj.shape[0]
    nt = seq // tb
    has_vres = v_first is not None
    row = lambda b, t: (b * nt + t, 0)
    full = lambda b, t: (0, 0)
    wide = pl.BlockSpec((tb, RWKV_DIM), row)
    vec = pl.BlockSpec((1, RWKV_DIM), full)
    in_specs = [pl.BlockSpec((tb, 3 * RWKV_DIM), lambda b, t: (b * nt + t, COL_RKV // (3 * RWKV_DIM))),
                pl.BlockSpec((tb, SM_W), lambda b, t: (b * nt + t, COL_SM // SM_W))]
    args = [proj, proj]
    if has_vres:
        in_specs.append(wide)
        args.append(v_first)
    in_specs += [pl.BlockSpec((1, 3 * RWKV_DIM), full), pl.BlockSpec((1, SM_W), full),
                 vec, pl.BlockSpec((128, RWKV_DIM), full), vec, pl.BlockSpec((128, RWKV_DIM), full),
                 pl.BlockSpec((256, RWKV_DIM), full), vec, vec]
    args += [lp["mu_rkv"], lp["mu_sm"], lp["w0"], lp["w_up"], lp["a0"], lp["a_up"], lp["g_up"],
             lp["k_k"], lp["k_a"]]
    if has_vres:
        in_specs += [pl.BlockSpec((128, RWKV_DIM), full), vec]
        args += [lp["v_up"], lp["v0"]]
    out = jax.ShapeDtypeStruct((m, RWKV_DIM), F32)
    return pl.pallas_call(
        functools.partial(_rwkv_prep_kernel, has_vres),
        grid=(batch, nt),
        in_specs=in_specs,
        out_specs=[wide] * 7,
        out_shape=[out] * 7,
        scratch_shapes=[pltpu.VMEM((tb + SUBLANES, 3 * RWKV_DIM), F32),
                        pltpu.VMEM((tb + SUBLANES, SM_W), F32)],
        compiler_params=pltpu.CompilerParams(
            dimension_semantics=("parallel", "arbitrary"), vmem_limit_bytes=VMEM_LIMIT),
        name="rwkv_prep",
    )(*args)


def _rwkv_pair_chunk(r, k, v, lw, kraw, eta, s_mat, ones_bd, tri, m0):
    c = r.shape[0]
    ss = _mm_exact_rhs(kraw * kraw, ones_bd)
    kk = kraw * lax.rsqrt(ss + RWKV_L2_EPS)
    beta = kk * eta
    log_w = _mm_exact_lhs(tri, lw)
    lw_last = log_w[c - 1:c, :]
    w_in = jnp.exp(log_w)
    w_prev = jnp.exp(log_w - lw)
    w_inv = jnp.exp(-log_w)
    w_end = jnp.exp(lw_last - log_w)

    def bd(x):
        return jnp.concatenate([jnp.where(m0, x, 0.0), jnp.where(m0, 0.0, x)], axis=0).astype(BF16)

    lhs = jnp.concatenate([bd(-kk * w_prev), bd(r * w_in)], axis=0)
    rhs = jnp.concatenate([bd(beta * w_inv), bd(k * w_inv)], axis=0)
    sc = _mm_nt(lhs, rhs)
    n2 = 2 * c
    i, j = _iota2((n2, n2))
    strict = i > j
    incl = i >= j
    a_ab = jnp.where(strict, sc[:n2, :n2], 0.0)
    a_ak = jnp.where(strict, sc[:n2, n2:], 0.0)
    a_rb = jnp.where(incl, sc[n2:, :n2], 0.0)
    a_rk = jnp.where(incl, sc[n2:, n2:], 0.0)
    t_inv = _unit_lower_inverse(a_ab)

    xr = _mm_nt(lhs, s_mat)
    v_bd = bd(v)
    u = _mm(t_inv, xr[:n2] + _mm(a_ak, v_bd))
    uv = jnp.concatenate([u.astype(BF16), v_bd], axis=0)
    y_bd = xr[n2:] + _mm(jnp.concatenate([a_rb, a_rk], axis=1), uv)
    y = y_bd[:c] + y_bd[c:]
    bk = jnp.concatenate([bd(beta * w_end), bd(k * w_end)], axis=0)
    s_new = s_mat * jnp.exp(lw_last) + _mm_tn(uv, bk)
    return y, s_new


def _rwkv_rec_kernel(n_pairs, r_ref, k_ref, v_ref, lw_ref, kr_ref, eta_ref, g_ref,
                     rk_ref, lnw_ref, lnb_ref, o_ref, s_ref):
    tt = r_ref.shape[0]

    @pl.when(pl.program_id(2) == 0)
    def _():
        s_ref[...] = jnp.zeros(s_ref.shape, F32)

    li, lj = _iota2((LANES, LANES))
    ones_bd = ((li // RWKV_HEAD) == (lj // RWKV_HEAD)).astype(BF16)
    ti, tj = _iota2((CHUNK, CHUNK))
    tri = (ti >= tj).astype(BF16)
    m0 = lax.broadcasted_iota(jnp.int32, (CHUNK, LANES), 1) < RWKV_HEAD
    inv_n = 1.0 / RWKV_HEAD

    def body(ci, carry):
        rows = pl.ds(pl.multiple_of(ci * CHUNK, CHUNK), CHUNK)
        for p in range(n_pairs):
            cols = slice(p * LANES, (p + 1) * LANES)
            r = r_ref[rows, cols]
            k = k_ref[rows, cols]
            v = v_ref[rows, cols]
            y, s_new = _rwkv_pair_chunk(r, k, v, lw_ref[rows, cols], kr_ref[rows, cols],
                                        eta_ref[rows, cols], s_ref[p], ones_bd, tri, m0)
            s_ref[p] = s_new
            mean = _mm_exact_rhs(y, ones_bd) * inv_n
            d = y - mean
            var = _mm_exact_rhs(d * d, ones_bd) * inv_n
            yn = d * lax.rsqrt(var + RWKV_GN_EPS) * lnw_ref[:, cols] + lnb_ref[:, cols]
            bonus = _mm_exact_rhs(r * k * rk_ref[:, cols], ones_bd) * v
            o_ref[rows, cols] = ((yn + bonus) * g_ref[rows, cols]).astype(BF16)
        return carry

    lax.fori_loop(0, tt // CHUNK, body, 0)


def _rwkv_rec(r, k, v, lw, kr, eta, g, lp, batch, seq, tt, n_pairs):
    m = r.shape[0]
    nt = seq // tt
    gw = n_pairs * LANES
    blk = pl.BlockSpec((tt, gw), lambda b, h, t: (b * nt + t, h))
    vec = pl.BlockSpec((1, gw), lambda b, h, t: (0, h))
    return pl.pallas_call(
        functools.partial(_rwkv_rec_kernel, n_pairs),
        grid=(batch, RWKV_DIM // gw, nt),
        in_specs=[blk] * 7 + [vec] * 3,
        out_specs=blk,
        out_shape=jax.ShapeDtypeStruct((m, RWKV_DIM), BF16),
        scratch_shapes=[pltpu.VMEM((n_pairs, LANES, LANES), F32)],
        compiler_params=pltpu.CompilerParams(
            dimension_semantics=("parallel", "parallel", "arbitrary"), vmem_limit_bytes=VMEM_LIMIT),
        name="rwkv_recurrence",
    )(r, k, v, lw, kr, eta, g, lp["r_k"], lp["ln_w"], lp["ln_b"])


def _gdn_prep_kernel(qk_ref, v_ref, sm_ref, cw_ref, alog_ref, dtb_ref,
                     q_o, k_o, v_o, g_o, b_o, ext_ref):
    tb = qk_ref.shape[0]
    first = pl.program_id(1) == 0
    x = jnp.concatenate([qk_ref[...], v_ref[...]], axis=1)
    (x1, x2, x3), tail = _shifted(ext_ref, x, GDN_CONV - 1, first)
    cw = cw_ref[...]
    y = x * cw[3:4, :] + x1 * cw[2:3, :] + x2 * cw[1:2, :] + x3 * cw[0:1, :]
    ext_ref[0:SUBLANES, :] = tail
    y = _silu(y)
    for h in range(GDN_QK_HEADS):
        cq = slice(h * GDN_HEAD, (h + 1) * GDN_HEAD)
        ck = slice(GDN_QK_DIM + h * GDN_HEAD, GDN_QK_DIM + (h + 1) * GDN_HEAD)
        qh = y[:, cq]
        kh = y[:, ck]
        q_o[:, cq] = qh * (lax.rsqrt(jnp.sum(qh * qh, axis=-1, keepdims=True) + GDN_L2_EPS)
                           * (GDN_HEAD ** -0.5))
        k_o[:, cq] = kh * lax.rsqrt(jnp.sum(kh * kh, axis=-1, keepdims=True) + GDN_L2_EPS)
    v_o[...] = y[:, 2 * GDN_QK_DIM:]

    ab = sm_ref[:, SM_AB:SM_AB + LANES]
    g = -jnp.exp(alog_ref[...]) * _softplus(ab + dtb_ref[...])
    beta = _sigmoid(ab)
    ti, tj = _iota2((tb, tb))
    tri = ((ti >= tj) & ((ti // CHUNK) == (tj // CHUNK))).astype(BF16)
    g_cum = _mm_exact_lhs(tri, g)
    ei, ej = _iota2((LANES, GDN_V_DIM))
    e_g = (ei == ej // GDN_HEAD).astype(BF16)
    e_b = (ei == ej // GDN_HEAD + GDN_V_HEADS).astype(BF16)
    g_o[...] = _mm_exact_rhs(g_cum, e_g)
    b_o[...] = _mm_exact_rhs(beta, e_b)


def _gdn_prep(proj, lp, batch, seq, tb):
    m = proj.shape[0]
    nt = seq // tb
    full = lambda b, t: (0, 0)
    qk_w = 2 * GDN_QK_DIM
    return pl.pallas_call(
        _gdn_prep_kernel,
        grid=(batch, nt),
        in_specs=[pl.BlockSpec((tb, qk_w), lambda b, t: (b * nt + t, COL_GQK // qk_w)),
                  pl.BlockSpec((tb, GDN_V_DIM), lambda b, t: (b * nt + t, COL_GV // GDN_V_DIM)),
                  pl.BlockSpec((tb, SM_W), lambda b, t: (b * nt + t, COL_SM // SM_W)),
                  pl.BlockSpec((GDN_CONV, qk_w + GDN_V_DIM), full),
                  pl.BlockSpec((1, LANES), full),
                  pl.BlockSpec((1, LANES), full)],
        out_specs=[pl.BlockSpec((tb, GDN_QK_DIM), lambda b, t: (b * nt + t, 0)),
                   pl.BlockSpec((tb, GDN_QK_DIM), lambda b, t: (b * nt + t, 0)),
                   pl.BlockSpec((tb, GDN_V_DIM), lambda b, t: (b * nt + t, 0)),
                   pl.BlockSpec((tb, GDN_V_DIM), lambda b, t: (b * nt + t, 0)),
                   pl.BlockSpec((tb, GDN_V_DIM), lambda b, t: (b * nt + t, 0))],
        out_shape=[jax.ShapeDtypeStruct((m, GDN_QK_DIM), F32),
                   jax.ShapeDtypeStruct((m, GDN_QK_DIM), F32),
                   jax.ShapeDtypeStruct((m, GDN_V_DIM), F32),
                   jax.ShapeDtypeStruct((m, GDN_V_DIM), F32),
                   jax.ShapeDtypeStruct((m, GDN_V_DIM), F32)],
        scratch_shapes=[pltpu.VMEM((tb + SUBLANES, qk_w + GDN_V_DIM), F32)],
        compiler_params=pltpu.CompilerParams(
            dimension_semantics=("parallel", "arbitrary"), vmem_limit_bytes=VMEM_LIMIT),
        name="gdn_prep",
    )(proj, proj, proj, lp["conv_w"], lp["a_log"], lp["dt_bias"])


def _gdn_pair_chunk(q, k, v, g_b, beta_b, s_cat):
    c = q.shape[0]
    n2 = 2 * c
    st = lambda x: jnp.concatenate([x[:, :GDN_HEAD], x[:, GDN_HEAD:]], axis=0)
    g_c = st(g_b)
    b_c = st(beta_b)
    v_s = st(v)
    k2 = jnp.concatenate([k, k], axis=0)
    q2 = jnp.concatenate([q, q], axis=0)
    i, j = _iota2((n2, n2))
    same = (i // c) == (j // c)
    incl = same & (i >= j)
    gamma = jnp.exp(jnp.where(incl, g_c - g_c.T, -jnp.inf))
    kb = k2 * b_c
    kq = _mm_nt(jnp.concatenate([kb, q2], axis=0), k2)
    l_mat = jnp.where(i > j, kq[:n2] * gamma, 0.0)
    a_qk = kq[n2:] * gamma
    t_inv = _unit_lower_inverse(-l_mat)
    e_g = jnp.exp(g_c)
    uw = _mm(t_inv, jnp.concatenate([v_s * b_c, kb * e_g], axis=1))
    top = lax.broadcasted_iota(jnp.int32, (n2, GDN_HEAD), 0) < c
    gl0 = g_c[c - 1:c, :]
    gl1 = g_c[n2 - 1:n2, :]
    g_last = jnp.where(top, gl0, gl1)
    k_g = k2 * jnp.exp(g_last - g_c)
    wq = _mm(jnp.concatenate([uw[:, GDN_HEAD:], q2 * e_g], axis=0), s_cat)
    pick = lambda x: jnp.where(top, x[:, :GDN_HEAD], x[:, GDN_HEAD:])
    v_new = uw[:, :GDN_HEAD] - pick(wq[:n2])
    o_s = pick(wq[n2:]) + _mm(a_qk, v_new)
    v_bd = jnp.concatenate([jnp.where(top, v_new, 0.0), jnp.where(top, 0.0, v_new)], axis=1)
    decay = jnp.concatenate([jnp.exp(gl0), jnp.exp(gl1)], axis=1)
    s_new = s_cat * decay + _mm_tn(k_g, v_bd)
    o = jnp.concatenate([o_s[:c], o_s[c:]], axis=1)
    return o, s_new


def _gdn_rec_kernel(q_ref, k_ref, v_ref, z_ref, g_ref, b_ref, nw_ref, o_ref, s_ref):
    tt = q_ref.shape[0]

    @pl.when(pl.program_id(2) == 0)
    def _():
        s_ref[...] = jnp.zeros(s_ref.shape, F32)

    def body(ci, carry):
        rows = pl.ds(pl.multiple_of(ci * CHUNK, CHUNK), CHUNK)
        o, s_new = _gdn_pair_chunk(q_ref[rows, :], k_ref[rows, :], v_ref[rows, :],
                                   g_ref[rows, :], b_ref[rows, :], s_ref[...])
        s_ref[...] = s_new
        z = z_ref[rows, :]
        for h in range(2):
            cols = slice(h * GDN_HEAD, (h + 1) * GDN_HEAD)
            oh = o[:, cols]
            oh = oh * lax.rsqrt(jnp.mean(oh * oh, axis=-1, keepdims=True) + GDN_NORM_EPS) * nw_ref[...]
            o_ref[rows, cols] = (oh * _silu(z[:, cols])).astype(BF16)
        return carry

    lax.fori_loop(0, tt // CHUNK, body, 0)


def _gdn_rec(q, k, v, proj, g_b, beta_b, lp, batch, seq, tt):
    m = q.shape[0]
    nt = seq // tt
    pw = 2 * GDN_HEAD
    qk_blk = pl.BlockSpec((tt, GDN_HEAD), lambda b, h, t: (b * nt + t, h))
    v_blk = pl.BlockSpec((tt, pw), lambda b, h, t: (b * nt + t, h))
    z_blk = pl.BlockSpec((tt, pw), lambda b, h, t: (b * nt + t, COL_GZ // pw + h))
    return pl.pallas_call(
        _gdn_rec_kernel,
        grid=(batch, GDN_QK_HEADS, nt),
        in_specs=[qk_blk, qk_blk, v_blk, z_blk, v_blk, v_blk,
                  pl.BlockSpec((1, GDN_HEAD), lambda b, h, t: (0, 0))],
        out_specs=v_blk,
        out_shape=jax.ShapeDtypeStruct((m, GDN_V_DIM), BF16),
        scratch_shapes=[pltpu.VMEM((GDN_HEAD, pw), F32)],
        compiler_params=pltpu.CompilerParams(
            dimension_semantics=("parallel", "parallel", "arbitrary"), vmem_limit_bytes=VMEM_LIMIT),
        name="gdn_recurrence",
    )(q, k, v, proj, g_b, beta_b, lp["norm_w"])


def _pad_cols(a, n):
    return jnp.pad(a, [(0, 0)] * (a.ndim - 1) + [(0, n - a.shape[-1])])


def _pad_rows(a, n):
    return jnp.pad(a, [(0, 0)] * (a.ndim - 2) + [(0, n - a.shape[-2]), (0, 0)])


def _proj_layout(main, vd):
    o = 0
    parts = {}
    for name, n in (("r", RWKV_DIM), ("k", RWKV_DIM), ("v", RWKV_DIM), ("wd", RWKV_DECAY_LORA),
                    ("ad", RWKV_AAA_LORA), ("gd", RWKV_GATE_LORA), ("gq", GDN_QK_DIM), ("gk", GDN_QK_DIM),
                    ("gv", GDN_V_DIM), ("gz", GDN_V_DIM), ("ga", GDN_V_HEADS), ("gb", GDN_V_HEADS)):
        parts[name] = main[..., o:o + n]
        o += n
    return jnp.concatenate(
        [parts["r"], parts["k"], parts["v"], parts["gq"], parts["gk"], parts["gv"], parts["gz"],
         _pad_cols(parts["wd"], 128), _pad_cols(parts["ad"], 128), _pad_cols(parts["gd"], 256),
         _pad_cols(vd, 128), _pad_cols(jnp.concatenate([parts["ga"], parts["gb"]], axis=-1), 128)],
        axis=-1)


def kernel(x, attn_norm_w, w_in, rwkv_mu, rwkv_w0, rwkv_w_up, rwkv_a0, rwkv_a_up, rwkv_g_up,
           rwkv_k_k, rwkv_k_a, rwkv_r_k, rwkv_ln_w, rwkv_ln_b, vres_down, vres_mu, vres_up, vres_v0,
           gdn_conv_w, gdn_A_log, gdn_dt_bias, gdn_norm_w, w_out, ffn_norm_w, ffn_w_gate, ffn_w_up,
           ffn_w_down, final_norm_w):
    batch, seq, d = x.shape
    depth = w_in.shape[0]
    m = batch * seq

    vd_w = jnp.concatenate([jnp.zeros((1, d, RWKV_MV_LORA), F32), vres_down], axis=0)
    w_proj = _proj_layout(w_in, vd_w).astype(BF16)
    gdn_pad = jnp.zeros((depth, w_in.shape[2] - rwkv_mu.shape[1]), F32)
    vmu = jnp.concatenate([jnp.zeros((1, RWKV_MV_LORA), F32), vres_mu], axis=0)
    mu_all = _proj_layout(jnp.concatenate([rwkv_mu, gdn_pad], axis=1), vmu)
    w_up = _pad_rows(rwkv_w_up, 128).astype(BF16)
    a_up = _pad_rows(rwkv_a_up, 128).astype(BF16)
    g_up = _pad_rows(rwkv_g_up, 256).astype(BF16)
    v_up = _pad_rows(vres_up, 128).astype(BF16)
    w_out_b = w_out.astype(BF16)
    wg_b = ffn_w_gate.astype(BF16)
    wu_b = ffn_w_up.astype(BF16)
    wd_b = ffn_w_down.astype(BF16)
    a_log = _pad_cols(gdn_A_log, LANES)
    dt_bias = _pad_cols(gdn_dt_bias, LANES)
    r_k = rwkv_r_k.reshape(depth, RWKV_DIM)

    tm = min(1024, m)
    tb = min(256, seq)
    tt = min(512, seq)
    row = lambda a, l: a[l][None, :]

    xf = x.reshape(m, d)
    v_first = None
    for l in range(depth):
        proj = _norm_matmul(xf, row(attn_norm_w, l), w_proj[l], tm, 768)
        lp = dict(mu_rkv=mu_all[l][None, COL_RKV:COL_RKV + 3 * RWKV_DIM],
                  mu_sm=mu_all[l][None, COL_SM:COL_SM + SM_W],
                  w0=row(rwkv_w0, l), w_up=w_up[l], a0=row(rwkv_a0, l), a_up=a_up[l], g_up=g_up[l],
                  k_k=row(rwkv_k_k, l), k_a=row(rwkv_k_a, l), r_k=row(r_k, l),
                  ln_w=row(rwkv_ln_w, l), ln_b=row(rwkv_ln_b, l),
                  conv_w=gdn_conv_w[l], a_log=row(a_log, l), dt_bias=row(dt_bias, l),
                  norm_w=row(gdn_norm_w, l))
        if l > 0:
            lp["v_up"] = v_up[l - 1]
            lp["v0"] = row(vres_v0, l - 1)
        r, k, v, lw, kr, eta, g = _rwkv_prep(proj, v_first, lp, batch, seq, tb)
        if l == 0:
            v_first = v
        y_a = _rwkv_rec(r, k, v, lw, kr, eta, g, lp, batch, seq, tt, 2)
        gq, gk, gv, g_b, beta_b = _gdn_prep(proj, lp, batch, seq, tb)
        y_b = _gdn_rec(gq, gk, gv, proj, g_b, beta_b, lp, batch, seq, tt)
        xf = _matmul2_res(y_a, y_b, w_out_b[l], xf, tm, 512)
        hmid = _norm_swiglu(xf, row(ffn_norm_w, l), wg_b[l], wu_b[l], tm, 512)
        xf = _matmul_res(hmid, wd_b[l], xf, min(512, m), 512)
    out = _rmsnorm(xf, final_norm_w[None, :], min(512, m))
    return out.reshape(batch, seq, d)
```

```python
import functools

import jax
import jax.numpy as jnp
from jax import lax
from jax.experimental import pallas as pl
from jax.experimental.pallas import tpu as pltpu

F32 = jnp.float32
BF16 = jnp.bfloat16

D_MODEL = 2048
RWKV_HEAD = 64
RWKV_DIM = 1024
RWKV_DECAY_LORA = 64
RWKV_AAA_LORA = 64
RWKV_MV_LORA = 32
RWKV_GATE_LORA = 160
RWKV_GN_EPS = RWKV_HEAD * 1e-5
RWKV_L2_EPS = 1e-12
GDN_HEAD = 128
GDN_V_DIM = 1024
GDN_V_HEADS = 8
GDN_QK_HEADS = 4
GDN_QK_DIM = 512
GDN_CONV = 4
GDN_L2_EPS = 1e-6
GDN_NORM_EPS = 1e-6
D_FF = 5632
NORM_EPS = 1e-5

LANES = 128
SUBLANES = 8
CHUNK = 64
PAIR = 2 * CHUNK
VMEM_LIMIT = 56 * 1024 * 1024

COL_RKV = 0
COL_GQK = 3 * RWKV_DIM
COL_GV = COL_GQK + 2 * GDN_QK_DIM
COL_GZ = COL_GV + GDN_V_DIM
COL_SM = COL_GZ + GDN_V_DIM
SM_WD, SM_AD, SM_GD, SM_VD, SM_AB = 0, 128, 256, 512, 640
SM_W = 768
N_PROJ = COL_SM + SM_W


def _mm(a, b):
    return jnp.dot(a.astype(BF16), b.astype(BF16), preferred_element_type=F32)


def _mm_nt(a, b):
    return lax.dot_general(a.astype(BF16), b.astype(BF16), (((1,), (1,)), ((), ())),
                           preferred_element_type=F32)


def _mm_tn(a, b):
    return lax.dot_general(a.astype(BF16), b.astype(BF16), (((0,), (0,)), ((), ())),
                           preferred_element_type=F32)


def _split3(x):
    hi = x.astype(BF16)
    r1 = x - hi.astype(F32)
    mid = r1.astype(BF16)
    lo = (r1 - mid.astype(F32)).astype(BF16)
    return hi, mid, lo


def _mm_exact_lhs(a01, x):
    n = x.shape[1]
    hi, mid, lo = _split3(x)
    y = jnp.dot(a01, jnp.concatenate([hi, mid, lo], axis=1), preferred_element_type=F32)
    return y[:, :n] + y[:, n:2 * n] + y[:, 2 * n:]


def _mm_exact_rhs(x, b01):
    m = x.shape[0]
    hi, mid, lo = _split3(x)
    y = jnp.dot(jnp.concatenate([hi, mid, lo], axis=0), b01, preferred_element_type=F32)
    return y[:m] + y[m:2 * m] + y[2 * m:]


def _sigmoid(x):
    return 1.0 / (1.0 + jnp.exp(-x))


def _softplus(x):
    return jnp.maximum(x, 0.0) + jnp.log(1.0 + jnp.exp(-jnp.abs(x)))


def _silu(x):
    return x * _sigmoid(x)


def _iota2(shape):
    return (lax.broadcasted_iota(jnp.int32, shape, 0), lax.broadcasted_iota(jnp.int32, shape, 1))


def _unit_lower_inverse(n_mats):
    sz = n_mats[0].shape[0]
    i, j = _iota2((sz, sz))
    eye = (i == j).astype(F32)
    blk8 = (i // 8) == (j // 8)
    d1 = [jnp.where(blk8, n, 0.0) for n in n_mats]
    d2 = [_mm(x, x) for x in d1]
    d4 = [_mm(x, x) for x in d2]
    p = [eye + x for x in d1]
    p = [x + _mm(x, y) for x, y in zip(p, d2)]
    p = [x + _mm(x, y) for x, y in zip(p, d4)]
    for s in (8, 16, 32):
        off = ((i // (2 * s)) == (j // (2 * s))) & ((i // s) == (j // s) + 1)
        q = [_mm(jnp.where(off, n, 0.0), x) for n, x in zip(n_mats, p)]
        p = [x + _mm(x, y) for x, y in zip(p, q)]
    return p


def _norm_matmul_kernel(x_ref, nw_ref, w_ref, o_ref, h_ref):
    @pl.when(pl.program_id(1) == 0)
    def _():
        x = x_ref[...]
        ms = jnp.mean(x * x, axis=-1, keepdims=True)
        h_ref[...] = (x * lax.rsqrt(ms + NORM_EPS) * nw_ref[...]).astype(BF16)

    o_ref[...] = jnp.dot(h_ref[...], w_ref[...], preferred_element_type=F32)


def _norm_matmul(x, nw, w, tm, tn):
    m, d = x.shape
    n = w.shape[1]
    return pl.pallas_call(
        _norm_matmul_kernel,
        grid=(m // tm, n // tn),
        in_specs=[pl.BlockSpec((tm, d), lambda i, j: (i, 0)),
                  pl.BlockSpec((1, d), lambda i, j: (0, 0)),
                  pl.BlockSpec((d, tn), lambda i, j: (0, j))],
        out_specs=pl.BlockSpec((tm, tn), lambda i, j: (i, j)),
        out_shape=jax.ShapeDtypeStruct((m, n), F32),
        scratch_shapes=[pltpu.VMEM((tm, d), BF16)],
        compiler_params=pltpu.CompilerParams(
            dimension_semantics=("parallel", "arbitrary"), vmem_limit_bytes=VMEM_LIMIT),
        name="norm_in_proj",
    )(x, nw, w)


def _norm_swiglu_kernel(x_ref, nw_ref, wg_ref, wu_ref, o_ref, h_ref):
    @pl.when(pl.program_id(1) == 0)
    def _():
        x = x_ref[...]
        ms = jnp.mean(x * x, axis=-1, keepdims=True)
        h_ref[...] = (x * lax.rsqrt(ms + NORM_EPS) * nw_ref[...]).astype(BF16)

    h = h_ref[...]
    g = jnp.dot(h, wg_ref[...], preferred_element_type=F32)
    u = jnp.dot(h, wu_ref[...], preferred_element_type=F32)
    o_ref[...] = (_silu(g) * u).astype(BF16)


def _norm_swiglu(x, nw, wg, wu, tm, tn):
    m, d = x.shape
    n = wg.shape[1]
    return pl.pallas_call(
        _norm_swiglu_kernel,
        grid=(m // tm, n // tn),
        in_specs=[pl.BlockSpec((tm, d), lambda i, j: (i, 0)),
                  pl.BlockSpec((1, d), lambda i, j: (0, 0)),
                  pl.BlockSpec((d, tn), lambda i, j: (0, j)),
                  pl.BlockSpec((d, tn), lambda i, j: (0, j))],
        out_specs=pl.BlockSpec((tm, tn), lambda i, j: (i, j)),
        out_shape=jax.ShapeDtypeStruct((m, n), BF16),
        scratch_shapes=[pltpu.VMEM((tm, d), BF16)],
        compiler_params=pltpu.CompilerParams(
            dimension_semantics=("parallel", "arbitrary"), vmem_limit_bytes=VMEM_LIMIT),
        name="norm_swiglu",
    )(x, nw, wg, wu)


def _matmul_res_kernel(a_ref, w_ref, r_ref, o_ref):
    o_ref[...] = r_ref[...] + jnp.dot(a_ref[...], w_ref[...], preferred_element_type=F32)


def _matmul_res(a, w, res, tm, tn):
    m, k = a.shape
    n = w.shape[1]
    return pl.pallas_call(
        _matmul_res_kernel,
        grid=(m // tm, n // tn),
        in_specs=[pl.BlockSpec((tm, k), lambda i, j: (i, 0)),
                  pl.BlockSpec((k, tn), lambda i, j: (0, j)),
                  pl.BlockSpec((tm, tn), lambda i, j: (i, j))],
        out_specs=pl.BlockSpec((tm, tn), lambda i, j: (i, j)),
        out_shape=jax.ShapeDtypeStruct((m, n), F32),
        compiler_params=pltpu.CompilerParams(
            dimension_semantics=("parallel", "arbitrary"), vmem_limit_bytes=VMEM_LIMIT),
        name="matmul_residual",
    )(a, w, res)


def _matmul2_res_kernel(a_ref, b_ref, wa_ref, wb_ref, r_ref, o_ref):
    acc = jnp.dot(a_ref[...], wa_ref[...], preferred_element_type=F32)
    acc = acc + jnp.dot(b_ref[...], wb_ref[...], preferred_element_type=F32)
    o_ref[...] = r_ref[...] + acc


def _matmul2_res(a, b, w, res, tm, tn):
    m, k = a.shape
    n = w.shape[1]
    return pl.pallas_call(
        _matmul2_res_kernel,
        grid=(m // tm, n // tn),
        in_specs=[pl.BlockSpec((tm, k), lambda i, j: (i, 0)),
                  pl.BlockSpec((tm, k), lambda i, j: (i, 0)),
                  pl.BlockSpec((k, tn), lambda i, j: (0, j)),
                  pl.BlockSpec((k, tn), lambda i, j: (1, j)),
                  pl.BlockSpec((tm, tn), lambda i, j: (i, j))],
        out_specs=pl.BlockSpec((tm, tn), lambda i, j: (i, j)),
        out_shape=jax.ShapeDtypeStruct((m, n), F32),
        compiler_params=pltpu.CompilerParams(
            dimension_semantics=("parallel", "arbitrary"), vmem_limit_bytes=VMEM_LIMIT),
        name="out_proj_residual",
    )(a, b, w, w, res)


def _rmsnorm_kernel(x_ref, nw_ref, o_ref):
    x = x_ref[...]
    ms = jnp.mean(x * x, axis=-1, keepdims=True)
    o_ref[...] = x * lax.rsqrt(ms + NORM_EPS) * nw_ref[...]


def _rmsnorm(x, nw, tm):
    m, d = x.shape
    return pl.pallas_call(
        _rmsnorm_kernel,
        grid=(m // tm,),
        in_specs=[pl.BlockSpec((tm, d), lambda i: (i, 0)),
                  pl.BlockSpec((1, d), lambda i: (0, 0))],
        out_specs=pl.BlockSpec((tm, d), lambda i: (i, 0)),
        out_shape=jax.ShapeDtypeStruct((m, d), F32),
        compiler_params=pltpu.CompilerParams(dimension_semantics=("parallel",)),
        name="final_rmsnorm",
    )(x, nw)


def _shifted(ext_ref, p, n_back, first_block):
    tb = p.shape[0]

    @pl.when(first_block)
    def _():
        ext_ref[0:SUBLANES, :] = jnp.zeros((SUBLANES, p.shape[1]), F32)

    ext_ref[SUBLANES:SUBLANES + tb, :] = p
    views = [ext_ref[SUBLANES - s:SUBLANES - s + tb, :] for s in range(1, n_back + 1)]
    tail = ext_ref[tb:tb + SUBLANES, :]
    return views, tail


def _rwkv_prep_kernel(has_vres, *refs):
    if has_vres:
        (rkv_ref, sm_ref, vf_ref, mu_rkv_ref, mu_sm_ref, w0_ref, wup_ref, a0_ref, aup_ref, gup_ref,
         kk_ref, ka_ref, vup_ref, v0_ref,
         r_o, k_o, v_o, lw_o, kr_o, eta_o, g_o, ext_rkv, ext_sm) = refs
    else:
        (rkv_ref, sm_ref, mu_rkv_ref, mu_sm_ref, w0_ref, wup_ref, a0_ref, aup_ref, gup_ref,
         kk_ref, ka_ref,
         r_o, k_o, v_o, lw_o, kr_o, eta_o, g_o, ext_rkv, ext_sm) = refs
    first = pl.program_id(1) == 0

    p = rkv_ref[...]
    (prev,), tail = _shifted(ext_rkv, p, 1, first)
    sh = p + (prev - p) * mu_rkv_ref[...]
    ext_rkv[0:SUBLANES, :] = tail

    ps = sm_ref[...]
    (prev_s,), tail_s = _shifted(ext_sm, ps, 1, first)
    shs = ps + (prev_s - ps) * mu_sm_ref[...]
    ext_sm[0:SUBLANES, :] = tail_s

    r = sh[:, 0:RWKV_DIM]
    k = sh[:, RWKV_DIM:2 * RWKV_DIM]
    v = sh[:, 2 * RWKV_DIM:3 * RWKV_DIM]
    wd = shs[:, SM_WD:SM_WD + 128]
    ad = shs[:, SM_AD:SM_AD + 128]
    gd = shs[:, SM_GD:SM_GD + 256]

    w = -_softplus(-(w0_ref[...] + _mm(jnp.tanh(wd), wup_ref[...]))) - 0.5
    lw_o[...] = -jnp.exp(w)
    eta = _sigmoid(a0_ref[...] + _mm(ad, aup_ref[...]))
    g_o[...] = _mm(_sigmoid(gd), gup_ref[...])
    if has_vres:
        vd = shs[:, SM_VD:SM_VD + 128]
        v = v + (vf_ref[...] - v) * _sigmoid(v0_ref[...] + _mm(vd, vup_ref[...]))
    r_o[...] = r
    v_o[...] = v
    kr_o[...] = k * kk_ref[...]
    k_o[...] = k * (1.0 + (eta - 1.0) * ka_ref[...])
    eta_o[...] = eta


def _rwkv_prep(proj, v_first, lp, batch, seq, tb):
    m = proj.shape[0]
    nt = seq // tb
    has_vres = v_first is not None
    row = lambda b, t: (b * nt + t, 0)
    full = lambda b, t: (0, 0)
    wide = pl.BlockSpec((tb, RWKV_DIM), row)
    vec = pl.BlockSpec((1, RWKV_DIM), full)
    in_specs = [pl.BlockSpec((tb, 3 * RWKV_DIM), lambda b, t: (b * nt + t, COL_RKV // (3 * RWKV_DIM))),
                pl.BlockSpec((tb, SM_W), lambda b, t: (b * nt + t, COL_SM // SM_W))]
    args = [proj, proj]
    if has_vres:
        in_specs.append(wide)
        args.append(v_first)
    in_specs += [pl.BlockSpec((1, 3 * RWKV_DIM), full), pl.BlockSpec((1, SM_W), full),
                 vec, pl.BlockSpec((128, RWKV_DIM), full), vec, pl.BlockSpec((128, RWKV_DIM), full),
                 pl.BlockSpec((256, RWKV_DIM), full), vec, vec]
    args += [lp["mu_rkv"], lp["mu_sm"], lp["w0"], lp["w_up"], lp["a0"], lp["a_up"], lp["g_up"],
             lp["k_k"], lp["k_a"]]
    if has_vres:
        in_specs += [pl.BlockSpec((128, RWKV_DIM), full), vec]
        args += [lp["v_up"], lp["v0"]]
    out = jax.ShapeDtypeStruct((m, RWKV_DIM), F32)
    return pl.pallas_call(
        functools.partial(_rwkv_prep_kernel, has_vres),
        grid=(batch, nt),
        in_specs=in_specs,
        out_specs=[wide] * 7,
        out_shape=[out] * 7,
        scratch_shapes=[pltpu.VMEM((tb + SUBLANES, 3 * RWKV_DIM), F32),
                        pltpu.VMEM((tb + SUBLANES, SM_W), F32)],
        compiler_params=pltpu.CompilerParams(
            dimension_semantics=("parallel", "arbitrary"), vmem_limit_bytes=VMEM_LIMIT),
        name="rwkv_prep",
    )(*args)


def _rwkv_chunk_units(units, ones_bd, tri, m0):
    c = CHUNK
    n2 = 2 * c
    ss = [_mm_exact_rhs(u[4] * u[4], ones_bd) for u in units]
    log_w = [_mm_exact_lhs(tri, u[3]) for u in units]
    bon = [_mm_exact_rhs(u[0] * u[1] * u[6], ones_bd) for u in units]

    def bd(x):
        return jnp.concatenate([jnp.where(m0, x, 0.0), jnp.where(m0, 0.0, x)], axis=0).astype(BF16)

    lhs, rhs, bk, v_bd, w_last = [], [], [], [], []
    for (r, k, v, lw, kraw, eta, _, _), s2, lg in zip(units, ss, log_w):
        kk = kraw * lax.rsqrt(s2 + RWKV_L2_EPS)
        beta = kk * eta
        lw_last = lg[c - 1:c, :]
        w_inv = jnp.exp(-lg)
        w_end = jnp.exp(lw_last - lg)
        lhs.append(jnp.concatenate([bd(-kk * jnp.exp(lg - lw)), bd(r * jnp.exp(lg))], axis=0))
        rhs.append(jnp.concatenate([bd(beta * w_inv), bd(k * w_inv)], axis=0))
        bk.append(jnp.concatenate([bd(beta * w_end), bd(k * w_end)], axis=0))
        v_bd.append(bd(v))
        w_last.append(jnp.exp(lw_last))

    sc = [_mm_nt(a, b) for a, b in zip(lhs, rhs)]
    xr = [_mm_nt(a, u[7]) for a, u in zip(lhs, units)]
    i, j = _iota2((n2, n2))
    strict = i > j
    incl = i >= j
    a_ab = [jnp.where(strict, x[:n2, :n2], 0.0) for x in sc]
    av = [_mm(jnp.where(strict, x[:n2, n2:], 0.0), vb) for x, vb in zip(sc, v_bd)]
    a_r = [jnp.where(jnp.concatenate([incl, incl], axis=1), x[n2:, :], 0.0).astype(BF16) for x in sc]
    t_inv = _unit_lower_inverse(a_ab)
    u_mat = [_mm(t, x[:n2] + a) for t, x, a in zip(t_inv, xr, av)]
    uv = [jnp.concatenate([x.astype(BF16), vb], axis=0) for x, vb in zip(u_mat, v_bd)]
    y_bd = [x[n2:] + _mm(a, z) for x, a, z in zip(xr, a_r, uv)]
    s_new = [u[7] * wl + _mm_tn(z, b) for u, wl, z, b in zip(units, w_last, uv, bk)]
    y = [x[:c] + x[c:] for x in y_bd]
    return y, bon, s_new


def _rwkv_rec_kernel(n_pairs, r_ref, k_ref, v_ref, lw_ref, kr_ref, eta_ref, g_ref,
                     rk_ref, lnw_ref, lnb_ref, o_ref, s_ref):
    tt = r_ref.shape[0]

    @pl.when(pl.program_id(2) == 0)
    def _():
        s_ref[...] = jnp.zeros(s_ref.shape, F32)

    li, lj = _iota2((LANES, LANES))
    ones_bd = ((li // RWKV_HEAD) == (lj // RWKV_HEAD)).astype(BF16)
    ti, tj = _iota2((CHUNK, CHUNK))
    tri = (ti >= tj).astype(BF16)
    m0 = lax.broadcasted_iota(jnp.int32, (CHUNK, LANES), 1) < RWKV_HEAD
    inv_n = 1.0 / RWKV_HEAD
    col = [slice(p * LANES, (p + 1) * LANES) for p in range(n_pairs)]

    def body(ci, carry):
        rows = pl.ds(pl.multiple_of(ci * CHUNK, CHUNK), CHUNK)
        units = [tuple(ref[rows, cs] for ref in (r_ref, k_ref, v_ref, lw_ref, kr_ref, eta_ref))
                 + (rk_ref[:, cs], s_ref[p]) for p, cs in enumerate(col)]
        y, bon, s_new = _rwkv_chunk_units(units, ones_bd, tri, m0)
        mean = [_mm_exact_rhs(x, ones_bd) * inv_n for x in y]
        d = [x - mu for x, mu in zip(y, mean)]
        var = [_mm_exact_rhs(x * x, ones_bd) * inv_n for x in d]
        for p, cs in enumerate(col):
            yn = d[p] * lax.rsqrt(var[p] + RWKV_GN_EPS) * lnw_ref[:, cs] + lnb_ref[:, cs]
            o_ref[rows, cs] = ((yn + bon[p] * units[p][2]) * g_ref[rows, cs]).astype(BF16)
            s_ref[p] = s_new[p]
        return carry

    lax.fori_loop(0, tt // CHUNK, body, 0)


def _rwkv_rec(r, k, v, lw, kr, eta, g, lp, batch, seq, tt, n_pairs):
    m = r.shape[0]
    nt = seq // tt
    gw = n_pairs * LANES
    blk = pl.BlockSpec((tt, gw), lambda b, h, t: (b * nt + t, h))
    vec = pl.BlockSpec((1, gw), lambda b, h, t: (0, h))
    return pl.pallas_call(
        functools.partial(_rwkv_rec_kernel, n_pairs),
        grid=(batch, RWKV_DIM // gw, nt),
        in_specs=[blk] * 7 + [vec] * 3,
        out_specs=blk,
        out_shape=jax.ShapeDtypeStruct((m, RWKV_DIM), BF16),
        scratch_shapes=[pltpu.VMEM((n_pairs, LANES, LANES), F32)],
        compiler_params=pltpu.CompilerParams(
            dimension_semantics=("parallel", "parallel", "arbitrary"), vmem_limit_bytes=VMEM_LIMIT),
        name="rwkv_recurrence",
    )(r, k, v, lw, kr, eta, g, lp["r_k"], lp["ln_w"], lp["ln_b"])


def _gdn_prep_kernel(qk_ref, v_ref, sm_ref, cw_ref, alog_ref, dtb_ref,
                     q_o, k_o, v_o, g_o, b_o, ext_ref):
    tb = qk_ref.shape[0]
    first = pl.program_id(1) == 0
    x = jnp.concatenate([qk_ref[...], v_ref[...]], axis=1)
    (x1, x2, x3), tail = _shifted(ext_ref, x, GDN_CONV - 1, first)
    cw = cw_ref[...]
    y = x * cw[3:4, :] + x1 * cw[2:3, :] + x2 * cw[1:2, :] + x3 * cw[0:1, :]
    ext_ref[0:SUBLANES, :] = tail
    y = _silu(y)
    for h in range(GDN_QK_HEADS):
        cq = slice(h * GDN_HEAD, (h + 1) * GDN_HEAD)
        ck = slice(GDN_QK_DIM + h * GDN_HEAD, GDN_QK_DIM + (h + 1) * GDN_HEAD)
        qh = y[:, cq]
        kh = y[:, ck]
        q_o[:, cq] = qh * (lax.rsqrt(jnp.sum(qh * qh, axis=-1, keepdims=True) + GDN_L2_EPS)
                           * (GDN_HEAD ** -0.5))
        k_o[:, cq] = kh * lax.rsqrt(jnp.sum(kh * kh, axis=-1, keepdims=True) + GDN_L2_EPS)
    v_o[...] = y[:, 2 * GDN_QK_DIM:]

    ab = sm_ref[:, SM_AB:SM_AB + LANES]
    g = -jnp.exp(alog_ref[...]) * _softplus(ab + dtb_ref[...])
    beta = _sigmoid(ab)
    ti, tj = _iota2((tb, tb))
    tri = ((ti >= tj) & ((ti // CHUNK) == (tj // CHUNK))).astype(BF16)
    g_cum = _mm_exact_lhs(tri, g)
    ei, ej = _iota2((LANES, GDN_V_DIM))
    e_g = (ei == ej // GDN_HEAD).astype(BF16)
    e_b = (ei == ej // GDN_HEAD + GDN_V_HEADS).astype(BF16)
    g_o[...] = _mm_exact_rhs(g_cum, e_g)
    b_o[...] = _mm_exact_rhs(beta, e_b)


def _gdn_prep(proj, lp, batch, seq, tb):
    m = proj.shape[0]
    nt = seq // tb
    full = lambda b, t: (0, 0)
    qk_w = 2 * GDN_QK_DIM
    return pl.pallas_call(
        _gdn_prep_kernel,
        grid=(batch, nt),
        in_specs=[pl.BlockSpec((tb, qk_w), lambda b, t: (b * nt + t, COL_GQK // qk_w)),
                  pl.BlockSpec((tb, GDN_V_DIM), lambda b, t: (b * nt + t, COL_GV // GDN_V_DIM)),
                  pl.BlockSpec((tb, SM_W), lambda b, t: (b * nt + t, COL_SM // SM_W)),
                  pl.BlockSpec((GDN_CONV, qk_w + GDN_V_DIM), full),
                  pl.BlockSpec((1, LANES), full),
                  pl.BlockSpec((1, LANES), full)],
        out_specs=[pl.BlockSpec((tb, GDN_QK_DIM), lambda b, t: (b * nt + t, 0)),
                   pl.BlockSpec((tb, GDN_QK_DIM), lambda b, t: (b * nt + t, 0)),
                   pl.BlockSpec((tb, GDN_V_DIM), lambda b, t: (b * nt + t, 0)),
                   pl.BlockSpec((tb, GDN_V_DIM), lambda b, t: (b * nt + t, 0)),
                   pl.BlockSpec((tb, GDN_V_DIM), lambda b, t: (b * nt + t, 0))],
        out_shape=[jax.ShapeDtypeStruct((m, GDN_QK_DIM), F32),
                   jax.ShapeDtypeStruct((m, GDN_QK_DIM), F32),
                   jax.ShapeDtypeStruct((m, GDN_V_DIM), F32),
                   jax.ShapeDtypeStruct((m, GDN_V_DIM), F32),
                   jax.ShapeDtypeStruct((m, GDN_V_DIM), F32)],
        scratch_shapes=[pltpu.VMEM((tb + SUBLANES, qk_w + GDN_V_DIM), F32)],
        compiler_params=pltpu.CompilerParams(
            dimension_semantics=("parallel", "arbitrary"), vmem_limit_bytes=VMEM_LIMIT),
        name="gdn_prep",
    )(proj, proj, proj, lp["conv_w"], lp["a_log"], lp["dt_bias"])


def _gdn_chunk_units(units):
    c = CHUNK
    n2 = 2 * c
    st = lambda x: jnp.concatenate([x[:, :GDN_HEAD], x[:, GDN_HEAD:]], axis=0)
    i, j = _iota2((n2, n2))
    incl = ((i // c) == (j // c)) & (i >= j)
    strict = i > j
    top = lax.broadcasted_iota(jnp.int32, (n2, GDN_HEAD), 0) < c
    pick = lambda x: jnp.where(top, x[:, :GDN_HEAD], x[:, GDN_HEAD:])

    g_c = [st(u[3]) for u in units]
    b_c = [st(u[4]) for u in units]
    k2 = [jnp.concatenate([u[1], u[1]], axis=0) for u in units]
    q2 = [jnp.concatenate([u[0], u[0]], axis=0) for u in units]
    kb = [a * b for a, b in zip(k2, b_c)]
    kq = [_mm_nt(jnp.concatenate([a, b], axis=0), kk) for a, b, kk in zip(kb, q2, k2)]
    gamma = [jnp.exp(jnp.where(incl, g - g.T, -jnp.inf)) for g in g_c]
    l_neg = [jnp.where(strict, -(x[:n2] * gm), 0.0) for x, gm in zip(kq, gamma)]
    a_qk = [(x[n2:] * gm).astype(BF16) for x, gm in zip(kq, gamma)]
    t_inv = _unit_lower_inverse(l_neg)
    e_g = [jnp.exp(g) for g in g_c]
    uw = [_mm(t, jnp.concatenate([st(u[2]) * b, kbi * eg], axis=1))
          for t, u, b, kbi, eg in zip(t_inv, units, b_c, kb, e_g)]
    wq = [_mm(jnp.concatenate([x[:, GDN_HEAD:], qq * eg], axis=0), u[5])
          for x, qq, eg, u in zip(uw, q2, e_g, units)]
    v_new = [x[:, :GDN_HEAD] - pick(w[:n2]) for x, w in zip(uw, wq)]
    o_s = [pick(w[n2:]) + _mm(a, vn) for w, a, vn in zip(wq, a_qk, v_new)]
    outs, s_new = [], []
    for g, kk, vn, o, u in zip(g_c, k2, v_new, o_s, units):
        gl0 = g[c - 1:c, :]
        gl1 = g[n2 - 1:n2, :]
        k_g = kk * jnp.exp(jnp.where(top, gl0, gl1) - g)
        v_bd = jnp.concatenate([jnp.where(top, vn, 0.0), jnp.where(top, 0.0, vn)], axis=1)
        decay = jnp.concatenate([jnp.exp(gl0), jnp.exp(gl1)], axis=1)
        s_new.append(u[5] * decay + _mm_tn(k_g, v_bd))
        outs.append(jnp.concatenate([o[:c], o[c:]], axis=1))
    return outs, s_new


def _gdn_rec_kernel(q_ref, k_ref, v_ref, z_ref, g_ref, b_ref, nw_ref, o_ref, s_ref):
    bb, tt = q_ref.shape[0], q_ref.shape[1]
    pw = 2 * GDN_HEAD

    @pl.when(pl.program_id(1) == 0)
    def _():
        s_ref[...] = jnp.zeros(s_ref.shape, F32)

    ids = [(bi, h) for bi in range(bb) for h in range(GDN_QK_HEADS)]

    def body(ci, carry):
        rows = pl.ds(pl.multiple_of(ci * CHUNK, CHUNK), CHUNK)
        units = []
        for n, (bi, h) in enumerate(ids):
            cq = slice(h * GDN_HEAD, (h + 1) * GDN_HEAD)
            cv = slice(h * pw, (h + 1) * pw)
            units.append((q_ref[bi, rows, cq], k_ref[bi, rows, cq], v_ref[bi, rows, cv],
                          g_ref[bi, rows, cv], b_ref[bi, rows, cv], s_ref[n]))
        outs, s_new = _gdn_chunk_units(units)
        for n, (bi, h) in enumerate(ids):
            s_ref[n] = s_new[n]
            for e in range(2):
                cols = slice(h * pw + e * GDN_HEAD, h * pw + (e + 1) * GDN_HEAD)
                oh = outs[n][:, e * GDN_HEAD:(e + 1) * GDN_HEAD]
                oh = oh * lax.rsqrt(jnp.mean(oh * oh, axis=-1, keepdims=True) + GDN_NORM_EPS) * nw_ref[...]
                o_ref[bi, rows, cols] = (oh * _silu(z_ref[bi, rows, cols])).astype(BF16)
        return carry

    lax.fori_loop(0, tt // CHUNK, body, 0)


def _gdn_rec(q, k, v, proj, g_b, beta_b, lp, batch, seq, tt, bb):
    nt = seq // tt
    qk_blk = pl.BlockSpec((bb, tt, GDN_QK_DIM), lambda b, t: (b, t, 0))
    v_blk = pl.BlockSpec((bb, tt, GDN_V_DIM), lambda b, t: (b, t, 0))
    z_blk = pl.BlockSpec((bb, tt, GDN_V_DIM), lambda b, t: (b, t, COL_GZ // GDN_V_DIM))
    return pl.pallas_call(
        _gdn_rec_kernel,
        grid=(batch // bb, nt),
        in_specs=[qk_blk, qk_blk, v_blk, z_blk, v_blk, v_blk,
                  pl.BlockSpec((1, GDN_HEAD), lambda b, t: (0, 0))],
        out_specs=v_blk,
        out_shape=jax.ShapeDtypeStruct((batch, seq, GDN_V_DIM), BF16),
        scratch_shapes=[pltpu.VMEM((bb * GDN_QK_HEADS, GDN_HEAD, 2 * GDN_HEAD), F32)],
        compiler_params=pltpu.CompilerParams(
            dimension_semantics=("parallel", "arbitrary"), vmem_limit_bytes=VMEM_LIMIT),
        name="gdn_recurrence",
    )(q, k, v, proj, g_b, beta_b, lp["norm_w"])


def _pad_cols(a, n):
    return jnp.pad(a, [(0, 0)] * (a.ndim - 1) + [(0, n - a.shape[-1])])


def _pad_rows(a, n):
    return jnp.pad(a, [(0, 0)] * (a.ndim - 2) + [(0, n - a.shape[-2]), (0, 0)])


def _proj_layout(main, vd):
    o = 0
    parts = {}
    for name, n in (("r", RWKV_DIM), ("k", RWKV_DIM), ("v", RWKV_DIM), ("wd", RWKV_DECAY_LORA),
                    ("ad", RWKV_AAA_LORA), ("gd", RWKV_GATE_LORA), ("gq", GDN_QK_DIM), ("gk", GDN_QK_DIM),
                    ("gv", GDN_V_DIM), ("gz", GDN_V_DIM), ("ga", GDN_V_HEADS), ("gb", GDN_V_HEADS)):
        parts[name] = main[..., o:o + n]
        o += n
    return jnp.concatenate(
        [parts["r"], parts["k"], parts["v"], parts["gq"], parts["gk"], parts["gv"], parts["gz"],
         _pad_cols(parts["wd"], 128), _pad_cols(parts["ad"], 128), _pad_cols(parts["gd"], 256),
         _pad_cols(vd, 128), _pad_cols(jnp.concatenate([parts["ga"], parts["gb"]], axis=-1), 128)],
        axis=-1)


def kernel(x, attn_norm_w, w_in, rwkv_mu, rwkv_w0, rwkv_w_up, rwkv_a0, rwkv_a_up, rwkv_g_up,
           rwkv_k_k, rwkv_k_a, rwkv_r_k, rwkv_ln_w, rwkv_ln_b, vres_down, vres_mu, vres_up, vres_v0,
           gdn_conv_w, gdn_A_log, gdn_dt_bias, gdn_norm_w, w_out, ffn_norm_w, ffn_w_gate, ffn_w_up,
           ffn_w_down, final_norm_w):
    batch, seq, d = x.shape
    depth = w_in.shape[0]
    m = batch * seq

    vd_w = jnp.concatenate([jnp.zeros((1, d, RWKV_MV_LORA), F32), vres_down], axis=0)
    w_proj = _proj_layout(w_in, vd_w).astype(BF16)
    gdn_pad = jnp.zeros((depth, w_in.shape[2] - rwkv_mu.shape[1]), F32)
    vmu = jnp.concatenate([jnp.zeros((1, RWKV_MV_LORA), F32), vres_mu], axis=0)
    mu_all = _proj_layout(jnp.concatenate([rwkv_mu, gdn_pad], axis=1), vmu)
    w_up = _pad_rows(rwkv_w_up, 128).astype(BF16)
    a_up = _pad_rows(rwkv_a_up, 128).astype(BF16)
    g_up = _pad_rows(rwkv_g_up, 256).astype(BF16)
    v_up = _pad_rows(vres_up, 128).astype(BF16)
    w_out_b = w_out.astype(BF16)
    wg_b = ffn_w_gate.astype(BF16)
    wu_b = ffn_w_up.astype(BF16)
    wd_b = ffn_w_down.astype(BF16)
    a_log = _pad_cols(gdn_A_log, LANES)
    dt_bias = _pad_cols(gdn_dt_bias, LANES)
    r_k = rwkv_r_k.reshape(depth, RWKV_DIM)

    tm = min(1024, m)
    tb = min(256, seq)
    tt = min(512, seq)
    row = lambda a, l: a[l][None, :]

    xf = x.reshape(m, d)
    v_first = None
    for l in range(depth):
        proj = _norm_matmul(xf, row(attn_norm_w, l), w_proj[l], tm, 768)
        lp = dict(mu_rkv=mu_all[l][None, COL_RKV:COL_RKV + 3 * RWKV_DIM],
                  mu_sm=mu_all[l][None, COL_SM:COL_SM + SM_W],
                  w0=row(rwkv_w0, l), w_up=w_up[l], a0=row(rwkv_a0, l), a_up=a_up[l], g_up=g_up[l],
                  k_k=row(rwkv_k_k, l), k_a=row(rwkv_k_a, l), r_k=row(r_k, l),
                  ln_w=row(rwkv_ln_w, l), ln_b=row(rwkv_ln_b, l),
                  conv_w=gdn_conv_w[l], a_log=row(a_log, l), dt_bias=row(dt_bias, l),
                  norm_w=row(gdn_norm_w, l))
        if l > 0:
            lp["v_up"] = v_up[l - 1]
            lp["v0"] = row(vres_v0, l - 1)
        r, k, v, lw, kr, eta, g = _rwkv_prep(proj, v_first, lp, batch, seq, tb)
        if l == 0:
            v_first = v
        y_a = _rwkv_rec(r, k, v, lw, kr, eta, g, lp, batch, seq, min(256, seq), 8)
        gq, gk, gv, g_b, beta_b = _gdn_prep(proj, lp, batch, seq, tb)
        bb = 2 if batch % 2 == 0 else 1
        as3 = lambda a: a.reshape(batch, seq, a.shape[-1])
        y_b = _gdn_rec(as3(gq), as3(gk), as3(gv), as3(proj), as3(g_b), as3(beta_b), lp,
                       batch, seq, min(256, seq), bb).reshape(m, GDN_V_DIM)
        xf = _matmul2_res(y_a, y_b, w_out_b[l], xf, tm, 512)
        hmid = _norm_swiglu(xf, row(ffn_norm_w, l), wg_b[l], wu_b[l], tm, 512)
        xf = _matmul_res(hmid, wd_b[l], xf, min(512, m), 512)
    out = _rmsnorm(xf, final_norm_w[None, :], min(512, m))
    return out.reshape(batch, seq, d)
```

```python
import functools

import jax
import jax.numpy as jnp
from jax import lax
from jax.experimental import pallas as pl
from jax.experimental.pallas import tpu as pltpu

F32 = jnp.float32
BF16 = jnp.bfloat16

D_MODEL = 2048
RWKV_HEAD = 64
RWKV_DIM = 1024
RWKV_DECAY_LORA = 64
RWKV_AAA_LORA = 64
RWKV_MV_LORA = 32
RWKV_GATE_LORA = 160
RWKV_GN_EPS = RWKV_HEAD * 1e-5
RWKV_L2_EPS = 1e-12
GDN_HEAD = 128
GDN_V_DIM = 1024
GDN_V_HEADS = 8
GDN_QK_HEADS = 4
GDN_QK_DIM = 512
GDN_CONV = 4
GDN_L2_EPS = 1e-6
GDN_NORM_EPS = 1e-6
D_FF = 5632
NORM_EPS = 1e-5

LANES = 128
SUBLANES = 8
CHUNK = 64
PAIR = 2 * CHUNK
VMEM_LIMIT = 56 * 1024 * 1024

COL_RKV = 0
COL_GQK = 3 * RWKV_DIM
COL_GV = COL_GQK + 2 * GDN_QK_DIM
COL_GZ = COL_GV + GDN_V_DIM
COL_SM = COL_GZ + GDN_V_DIM
SM_WD, SM_AD, SM_GD, SM_VD, SM_AB = 0, 128, 256, 512, 640
SM_W = 768
N_PROJ = COL_SM + SM_W


def _mm(a, b):
    return jnp.dot(a.astype(BF16), b.astype(BF16), preferred_element_type=F32)


def _mm_nt(a, b):
    return lax.dot_general(a.astype(BF16), b.astype(BF16), (((1,), (1,)), ((), ())),
                           preferred_element_type=F32)


def _mm_tn(a, b):
    return lax.dot_general(a.astype(BF16), b.astype(BF16), (((0,), (0,)), ((), ())),
                           preferred_element_type=F32)


def _split3(x):
    hi = x.astype(BF16)
    r1 = x - hi.astype(F32)
    mid = r1.astype(BF16)
    lo = (r1 - mid.astype(F32)).astype(BF16)
    return hi, mid, lo


def _mm_exact_lhs(a01, x):
    n = x.shape[1]
    hi, mid, lo = _split3(x)
    y = jnp.dot(a01, jnp.concatenate([hi, mid, lo], axis=1), preferred_element_type=F32)
    return y[:, :n] + y[:, n:2 * n] + y[:, 2 * n:]


def _mm_exact_rhs(x, b01):
    m = x.shape[0]
    hi, mid, lo = _split3(x)
    y = jnp.dot(jnp.concatenate([hi, mid, lo], axis=0), b01, preferred_element_type=F32)
    return y[:m] + y[m:2 * m] + y[2 * m:]


def _sigmoid(x):
    return 1.0 / (1.0 + jnp.exp(-x))


def _softplus(x):
    return jnp.maximum(x, 0.0) + jnp.log(1.0 + jnp.exp(-jnp.abs(x)))


def _silu(x):
    return x * _sigmoid(x)


def _iota2(shape):
    return (lax.broadcasted_iota(jnp.int32, shape, 0), lax.broadcasted_iota(jnp.int32, shape, 1))


def _unit_lower_inverse(n_mats):
    sz = n_mats[0].shape[0]
    i, j = _iota2((sz, sz))
    eye = (i == j).astype(F32)
    blk8 = (i // 8) == (j // 8)
    d1 = [jnp.where(blk8, n, 0.0) for n in n_mats]
    d2 = [_mm(x, x) for x in d1]
    d4 = [_mm(x, x) for x in d2]
    p = [eye + x for x in d1]
    p = [x + _mm(x, y) for x, y in zip(p, d2)]
    p = [x + _mm(x, y) for x, y in zip(p, d4)]
    for s in (8, 16, 32):
        off = ((i // (2 * s)) == (j // (2 * s))) & ((i // s) == (j // s) + 1)
        q = [_mm(jnp.where(off, n, 0.0), x) for n, x in zip(n_mats, p)]
        p = [x + _mm(x, y) for x, y in zip(p, q)]
    return p


def _norm_matmul_kernel(x_ref, nw_ref, w_ref, o_ref, h_ref):
    @pl.when(pl.program_id(1) == 0)
    def _():
        x = x_ref[...]
        ms = jnp.mean(x * x, axis=-1, keepdims=True)
        h_ref[...] = (x * lax.rsqrt(ms + NORM_EPS) * nw_ref[...]).astype(BF16)

    o_ref[...] = jnp.dot(h_ref[...], w_ref[...], preferred_element_type=F32)


def _norm_matmul(x, nw, w, tm, tn):
    m, d = x.shape
    n = w.shape[1]
    return pl.pallas_call(
        _norm_matmul_kernel,
        grid=(m // tm, n // tn),
        in_specs=[pl.BlockSpec((tm, d), lambda i, j: (i, 0)),
                  pl.BlockSpec((1, d), lambda i, j: (0, 0)),
                  pl.BlockSpec((d, tn), lambda i, j: (0, j))],
        out_specs=pl.BlockSpec((tm, tn), lambda i, j: (i, j)),
        out_shape=jax.ShapeDtypeStruct((m, n), F32),
        scratch_shapes=[pltpu.VMEM((tm, d), BF16)],
        compiler_params=pltpu.CompilerParams(
            dimension_semantics=("parallel", "arbitrary"), vmem_limit_bytes=VMEM_LIMIT),
        name="norm_in_proj",
    )(x, nw, w)


def _norm_swiglu_kernel(x_ref, nw_ref, wg_ref, wu_ref, o_ref, h_ref):
    @pl.when(pl.program_id(1) == 0)
    def _():
        x = x_ref[...]
        ms = jnp.mean(x * x, axis=-1, keepdims=True)
        h_ref[...] = (x * lax.rsqrt(ms + NORM_EPS) * nw_ref[...]).astype(BF16)

    h = h_ref[...]
    g = jnp.dot(h, wg_ref[...], preferred_element_type=F32)
    u = jnp.dot(h, wu_ref[...], preferred_element_type=F32)
    o_ref[...] = (_silu(g) * u).astype(BF16)


def _norm_swiglu(x, nw, wg, wu, tm, tn):
    m, d = x.shape
    n = wg.shape[1]
    return pl.pallas_call(
        _norm_swiglu_kernel,
        grid=(m // tm, n // tn),
        in_specs=[pl.BlockSpec((tm, d), lambda i, j: (i, 0)),
                  pl.BlockSpec((1, d), lambda i, j: (0, 0)),
                  pl.BlockSpec((d, tn), lambda i, j: (0, j)),
                  pl.BlockSpec((d, tn), lambda i, j: (0, j))],
        out_specs=pl.BlockSpec((tm, tn), lambda i, j: (i, j)),
        out_shape=jax.ShapeDtypeStruct((m, n), BF16),
        scratch_shapes=[pltpu.VMEM((tm, d), BF16)],
        compiler_params=pltpu.CompilerParams(
            dimension_semantics=("parallel", "arbitrary"), vmem_limit_bytes=VMEM_LIMIT),
        name="norm_swiglu",
    )(x, nw, wg, wu)


def _matmul_res_kernel(a_ref, w_ref, r_ref, o_ref):
    o_ref[...] = r_ref[...] + jnp.dot(a_ref[...], w_ref[...], preferred_element_type=F32)


def _matmul_res(a, w, res, tm, tn):
    m, k = a.shape
    n = w.shape[1]
    return pl.pallas_call(
        _matmul_res_kernel,
        grid=(m // tm, n // tn),
        in_specs=[pl.BlockSpec((tm, k), lambda i, j: (i, 0)),
                  pl.BlockSpec((k, tn), lambda i, j: (0, j)),
                  pl.BlockSpec((tm, tn), lambda i, j: (i, j))],
        out_specs=pl.BlockSpec((tm, tn), lambda i, j: (i, j)),
        out_shape=jax.ShapeDtypeStruct((m, n), F32),
        compiler_params=pltpu.CompilerParams(
            dimension_semantics=("parallel", "arbitrary"), vmem_limit_bytes=VMEM_LIMIT),
        name="matmul_residual",
    )(a, w, res)


def _matmul2_res_kernel(a_ref, b_ref, wa_ref, wb_ref, r_ref, o_ref):
    acc = jnp.dot(a_ref[...], wa_ref[...], preferred_element_type=F32)
    acc = acc + jnp.dot(b_ref[...], wb_ref[...], preferred_element_type=F32)
    o_ref[...] = r_ref[...] + acc


def _matmul2_res(a, b, w, res, tm, tn):
    m, k = a.shape
    n = w.shape[1]
    return pl.pallas_call(
        _matmul2_res_kernel,
        grid=(m // tm, n // tn),
        in_specs=[pl.BlockSpec((tm, k), lambda i, j: (i, 0)),
                  pl.BlockSpec((tm, k), lambda i, j: (i, 0)),
                  pl.BlockSpec((k, tn), lambda i, j: (0, j)),
                  pl.BlockSpec((k, tn), lambda i, j: (1, j)),
                  pl.BlockSpec((tm, tn), lambda i, j: (i, j))],
        out_specs=pl.BlockSpec((tm, tn), lambda i, j: (i, j)),
        out_shape=jax.ShapeDtypeStruct((m, n), F32),
        compiler_params=pltpu.CompilerParams(
            dimension_semantics=("parallel", "arbitrary"), vmem_limit_bytes=VMEM_LIMIT),
        name="out_proj_residual",
    )(a, b, w, w, res)


def _rmsnorm_kernel(x_ref, nw_ref, o_ref):
    x = x_ref[...]
    ms = jnp.mean(x * x, axis=-1, keepdims=True)
    o_ref[...] = x * lax.rsqrt(ms + NORM_EPS) * nw_ref[...]


def _rmsnorm(x, nw, tm):
    m, d = x.shape
    return pl.pallas_call(
        _rmsnorm_kernel,
        grid=(m // tm,),
        in_specs=[pl.BlockSpec((tm, d), lambda i: (i, 0)),
                  pl.BlockSpec((1, d), lambda i: (0, 0))],
        out_specs=pl.BlockSpec((tm, d), lambda i: (i, 0)),
        out_shape=jax.ShapeDtypeStruct((m, d), F32),
        compiler_params=pltpu.CompilerParams(dimension_semantics=("parallel",)),
        name="final_rmsnorm",
    )(x, nw)


def _shifted(ext_ref, p, n_back, first_block):
    tb = p.shape[0]

    @pl.when(first_block)
    def _():
        ext_ref[0:SUBLANES, :] = jnp.zeros((SUBLANES, p.shape[1]), F32)

    ext_ref[SUBLANES:SUBLANES + tb, :] = p
    views = [ext_ref[SUBLANES - s:SUBLANES - s + tb, :] for s in range(1, n_back + 1)]
    tail = ext_ref[tb:tb + SUBLANES, :]
    return views, tail


def _rwkv_prep_kernel(has_vres, *refs):
    if has_vres:
        (rkv_ref, sm_ref, vf_ref, mu_rkv_ref, mu_sm_ref, w0_ref, wup_ref, a0_ref, aup_ref, gup_ref,
         kk_ref, ka_ref, vup_ref, v0_ref,
         r_o, k_o, v_o, lw_o, kr_o, eta_o, g_o, ext_rkv, ext_sm) = refs
    else:
        (rkv_ref, sm_ref, mu_rkv_ref, mu_sm_ref, w0_ref, wup_ref, a0_ref, aup_ref, gup_ref,
         kk_ref, ka_ref,
         r_o, k_o, v_o, lw_o, kr_o, eta_o, g_o, ext_rkv, ext_sm) = refs
    first = pl.program_id(1) == 0

    p = rkv_ref[...]
    (prev,), tail = _shifted(ext_rkv, p, 1, first)
    sh = p + (prev - p) * mu_rkv_ref[...]
    ext_rkv[0:SUBLANES, :] = tail

    ps = sm_ref[...]
    (prev_s,), tail_s = _shifted(ext_sm, ps, 1, first)
    shs = ps + (prev_s - ps) * mu_sm_ref[...]
    ext_sm[0:SUBLANES, :] = tail_s

    r = sh[:, 0:RWKV_DIM]
    k = sh[:, RWKV_DIM:2 * RWKV_DIM]
    v = sh[:, 2 * RWKV_DIM:3 * RWKV_DIM]
    wd = shs[:, SM_WD:SM_WD + 128]
    ad = shs[:, SM_AD:SM_AD + 128]
    gd = shs[:, SM_GD:SM_GD + 256]

    w = -_softplus(-(w0_ref[...] + _mm(jnp.tanh(wd), wup_ref[...]))) - 0.5
    lw_o[...] = -jnp.exp(w)
    eta = _sigmoid(a0_ref[...] + _mm(ad, aup_ref[...]))
    g_o[...] = _mm(_sigmoid(gd), gup_ref[...])
    if has_vres:
        vd = shs[:, SM_VD:SM_VD + 128]
        v = v + (vf_ref[...] - v) * _sigmoid(v0_ref[...] + _mm(vd, vup_ref[...]))
    r_o[...] = r
    v_o[...] = v
    kr_o[...] = k * kk_ref[...]
    k_o[...] = k * (1.0 + (eta - 1.0) * ka_ref[...])
    eta_o[...] = eta


def _rwkv_prep(proj, v_first, lp, batch, seq, tb):
    m = proj.shape[0]
    nt = seq // tb
    has_vres = v_first is not None
    row = lambda b, t: (b * nt + t, 0)
    full = lambda b, t: (0, 0)
    wide = pl.BlockSpec((tb, RWKV_DIM), row)
    vec = pl.BlockSpec((1, RWKV_DIM), full)
    in_specs = [pl.BlockSpec((tb, 3 * RWKV_DIM), lambda b, t: (b * nt + t, COL_RKV // (3 * RWKV_DIM))),
                pl.BlockSpec((tb, SM_W), lambda b, t: (b * nt + t, COL_SM // SM_W))]
    args = [proj, proj]
    if has_vres:
        in_specs.append(wide)
        args.append(v_first)
    in_specs += [pl.BlockSpec((1, 3 * RWKV_DIM), full), pl.BlockSpec((1, SM_W), full),
                 vec, pl.BlockSpec((128, RWKV_DIM), full), vec, pl.BlockSpec((128, RWKV_DIM), full),
                 pl.BlockSpec((256, RWKV_DIM), full), vec, vec]
    args += [lp["mu_rkv"], lp["mu_sm"], lp["w0"], lp["w_up"], lp["a0"], lp["a_up"], lp["g_up"],
             lp["k_k"], lp["k_a"]]
    if has_vres:
        in_specs += [pl.BlockSpec((128, RWKV_DIM), full), vec]
        args += [lp["v_up"], lp["v0"]]
    out = jax.ShapeDtypeStruct((m, RWKV_DIM), F32)
    return pl.pallas_call(
        functools.partial(_rwkv_prep_kernel, has_vres),
        grid=(batch, nt),
        in_specs=in_specs,
        out_specs=[wide] * 7,
        out_shape=[out] * 7,
        scratch_shapes=[pltpu.VMEM((tb + SUBLANES, 3 * RWKV_DIM), F32),
                        pltpu.VMEM((tb + SUBLANES, SM_W), F32)],
        compiler_params=pltpu.CompilerParams(
            dimension_semantics=("parallel", "arbitrary"), vmem_limit_bytes=VMEM_LIMIT),
        name="rwkv_prep",
    )(*args)


def _rwkv_chunk_units(units, ones_bd, tri, m0):
    c = CHUNK
    n2 = 2 * c
    ss = [_mm(u[4] * u[4], ones_bd) for u in units]
    log_w = [_mm_exact_lhs(tri, u[3]) for u in units]
    bon = [_mm(u[0] * u[1] * u[6], ones_bd) for u in units]

    def bd(x):
        return jnp.concatenate([jnp.where(m0, x, 0.0), jnp.where(m0, 0.0, x)], axis=0).astype(BF16)

    lhs, rhs, bk, v_bd, w_last = [], [], [], [], []
    for (r, k, v, lw, kraw, eta, _, _), s2, lg in zip(units, ss, log_w):
        kk = kraw * lax.rsqrt(s2 + RWKV_L2_EPS)
        beta = kk * eta
        lw_last = lg[c - 1:c, :]
        w_inv = jnp.exp(-lg)
        w_end = jnp.exp(lw_last - lg)
        lhs.append(jnp.concatenate([bd(-kk * jnp.exp(lg - lw)), bd(r * jnp.exp(lg))], axis=0))
        rhs.append(jnp.concatenate([bd(beta * w_inv), bd(k * w_inv)], axis=0))
        bk.append(jnp.concatenate([bd(beta * w_end), bd(k * w_end)], axis=0))
        v_bd.append(bd(v))
        w_last.append(jnp.exp(lw_last))

    sc = [_mm_nt(a, b) for a, b in zip(lhs, rhs)]
    xr = [_mm_nt(a, u[7]) for a, u in zip(lhs, units)]
    i, j = _iota2((n2, n2))
    strict = i > j
    incl = i >= j
    a_ab = [jnp.where(strict, x[:n2, :n2], 0.0) for x in sc]
    av = [_mm(jnp.where(strict, x[:n2, n2:], 0.0), vb) for x, vb in zip(sc, v_bd)]
    a_r = [jnp.where(jnp.concatenate([incl, incl], axis=1), x[n2:, :], 0.0).astype(BF16) for x in sc]
    t_inv = _unit_lower_inverse(a_ab)
    u_mat = [_mm(t, x[:n2] + a) for t, x, a in zip(t_inv, xr, av)]
    uv = [jnp.concatenate([x.astype(BF16), vb], axis=0) for x, vb in zip(u_mat, v_bd)]
    y_bd = [x[n2:] + _mm(a, z) for x, a, z in zip(xr, a_r, uv)]
    s_new = [u[7] * wl + _mm_tn(z, b) for u, wl, z, b in zip(units, w_last, uv, bk)]
    y = [x[:c] + x[c:] for x in y_bd]
    return y, bon, s_new


def _rwkv_rec_kernel(r_ref, k_ref, v_ref, lw_ref, kr_ref, eta_ref, g_ref,
                     rk_ref, lnw_ref, lnb_ref, o_ref, s_ref):
    bb, tt = r_ref.shape[0], r_ref.shape[1]
    n_pairs = RWKV_DIM // LANES

    @pl.when(pl.program_id(1) == 0)
    def _():
        s_ref[...] = jnp.zeros(s_ref.shape, F32)

    li, lj = _iota2((LANES, LANES))
    ones_bd = ((li // RWKV_HEAD) == (lj // RWKV_HEAD)).astype(BF16)
    ti, tj = _iota2((CHUNK, CHUNK))
    tri = (ti >= tj).astype(BF16)
    m0 = lax.broadcasted_iota(jnp.int32, (CHUNK, LANES), 1) < RWKV_HEAD
    inv_n = 1.0 / RWKV_HEAD
    ids = [(bi, slice(p * LANES, (p + 1) * LANES)) for bi in range(bb) for p in range(n_pairs)]

    def body(ci, carry):
        rows = pl.ds(pl.multiple_of(ci * CHUNK, CHUNK), CHUNK)
        units = [tuple(ref[bi, rows, cs] for ref in (r_ref, k_ref, v_ref, lw_ref, kr_ref, eta_ref))
                 + (rk_ref[:, cs], s_ref[n]) for n, (bi, cs) in enumerate(ids)]
        y, bon, s_new = _rwkv_chunk_units(units, ones_bd, tri, m0)
        mean = [_mm(x, ones_bd) * inv_n for x in y]
        d = [x - mu for x, mu in zip(y, mean)]
        var = [_mm(x * x, ones_bd) * inv_n for x in d]
        for n, (bi, cs) in enumerate(ids):
            yn = d[n] * lax.rsqrt(var[n] + RWKV_GN_EPS) * lnw_ref[:, cs] + lnb_ref[:, cs]
            o_ref[bi, rows, cs] = ((yn + bon[n] * units[n][2]) * g_ref[bi, rows, cs]).astype(BF16)
            s_ref[n] = s_new[n]
        return carry

    lax.fori_loop(0, tt // CHUNK, body, 0)


def _rwkv_rec(r, k, v, lw, kr, eta, g, lp, batch, seq, tt, bb):
    nt = seq // tt
    blk = pl.BlockSpec((bb, tt, RWKV_DIM), lambda b, t: (b, t, 0))
    vec = pl.BlockSpec((1, RWKV_DIM), lambda b, t: (0, 0))
    return pl.pallas_call(
        _rwkv_rec_kernel,
        grid=(batch // bb, nt),
        in_specs=[blk] * 7 + [vec] * 3,
        out_specs=blk,
        out_shape=jax.ShapeDtypeStruct((batch, seq, RWKV_DIM), BF16),
        scratch_shapes=[pltpu.VMEM((bb * RWKV_DIM // LANES, LANES, LANES), F32)],
        compiler_params=pltpu.CompilerParams(
            dimension_semantics=("parallel", "arbitrary"), vmem_limit_bytes=VMEM_LIMIT),
        name="rwkv_recurrence",
    )(r, k, v, lw, kr, eta, g, lp["r_k"], lp["ln_w"], lp["ln_b"])


def _gdn_prep_kernel(qk_ref, v_ref, sm_ref, cw_ref, alog_ref, dtb_ref,
                     q_o, k_o, v_o, g_o, b_o, ext_ref):
    tb = qk_ref.shape[0]
    first = pl.program_id(1) == 0
    x = jnp.concatenate([qk_ref[...], v_ref[...]], axis=1)
    (x1, x2, x3), tail = _shifted(ext_ref, x, GDN_CONV - 1, first)
    cw = cw_ref[...]
    y = x * cw[3:4, :] + x1 * cw[2:3, :] + x2 * cw[1:2, :] + x3 * cw[0:1, :]
    ext_ref[0:SUBLANES, :] = tail
    y = _silu(y)
    for h in range(GDN_QK_HEADS):
        cq = slice(h * GDN_HEAD, (h + 1) * GDN_HEAD)
        ck = slice(GDN_QK_DIM + h * GDN_HEAD, GDN_QK_DIM + (h + 1) * GDN_HEAD)
        qh = y[:, cq]
        kh = y[:, ck]
        q_o[:, cq] = qh * (lax.rsqrt(jnp.sum(qh * qh, axis=-1, keepdims=True) + GDN_L2_EPS)
                           * (GDN_HEAD ** -0.5))
        k_o[:, cq] = kh * lax.rsqrt(jnp.sum(kh * kh, axis=-1, keepdims=True) + GDN_L2_EPS)
    v_o[...] = y[:, 2 * GDN_QK_DIM:]

    ab = sm_ref[:, SM_AB:SM_AB + LANES]
    g = -jnp.exp(alog_ref[...]) * _softplus(ab + dtb_ref[...])
    beta = _sigmoid(ab)
    ti, tj = _iota2((tb, tb))
    tri = ((ti >= tj) & ((ti // CHUNK) == (tj // CHUNK))).astype(BF16)
    g_cum = _mm_exact_lhs(tri, g)
    ei, ej = _iota2((LANES, GDN_V_DIM))
    e_g = (ei == ej // GDN_HEAD).astype(BF16)
    e_b = (ei == ej // GDN_HEAD + GDN_V_HEADS).astype(BF16)
    g_o[...] = _mm_exact_rhs(g_cum, e_g)
    b_o[...] = _mm_exact_rhs(beta, e_b)


def _gdn_prep(proj, lp, batch, seq, tb):
    m = proj.shape[0]
    nt = seq // tb
    full = lambda b, t: (0, 0)
    qk_w = 2 * GDN_QK_DIM
    return pl.pallas_call(
        _gdn_prep_kernel,
        grid=(batch, nt),
        in_specs=[pl.BlockSpec((tb, qk_w), lambda b, t: (b * nt + t, COL_GQK // qk_w)),
                  pl.BlockSpec((tb, GDN_V_DIM), lambda b, t: (b * nt + t, COL_GV // GDN_V_DIM)),
                  pl.BlockSpec((tb, SM_W), lambda b, t: (b * nt + t, COL_SM // SM_W)),
                  pl.BlockSpec((GDN_CONV, qk_w + GDN_V_DIM), full),
                  pl.BlockSpec((1, LANES), full),
                  pl.BlockSpec((1, LANES), full)],
        out_specs=[pl.BlockSpec((tb, GDN_QK_DIM), lambda b, t: (b * nt + t, 0)),
                   pl.BlockSpec((tb, GDN_QK_DIM), lambda b, t: (b * nt + t, 0)),
                   pl.BlockSpec((tb, GDN_V_DIM), lambda b, t: (b * nt + t, 0)),
                   pl.BlockSpec((tb, GDN_V_DIM), lambda b, t: (b * nt + t, 0)),
                   pl.BlockSpec((tb, GDN_V_DIM), lambda b, t: (b * nt + t, 0))],
        out_shape=[jax.ShapeDtypeStruct((m, GDN_QK_DIM), F32),
                   jax.ShapeDtypeStruct((m, GDN_QK_DIM), F32),
                   jax.ShapeDtypeStruct((m, GDN_V_DIM), F32),
                   jax.ShapeDtypeStruct((m, GDN_V_DIM), F32),
                   jax.ShapeDtypeStruct((m, GDN_V_DIM), F32)],
        scratch_shapes=[pltpu.VMEM((tb + SUBLANES, qk_w + GDN_V_DIM), F32)],
        compiler_params=pltpu.CompilerParams(
            dimension_semantics=("parallel", "arbitrary"), vmem_limit_bytes=VMEM_LIMIT),
        name="gdn_prep",
    )(proj, proj, proj, lp["conv_w"], lp["a_log"], lp["dt_bias"])


def _gdn_chunk_units(units):
    c = CHUNK
    n2 = 2 * c
    st = lambda x: jnp.concatenate([x[:, :GDN_HEAD], x[:, GDN_HEAD:]], axis=0)
    i, j = _iota2((n2, n2))
    incl = ((i // c) == (j // c)) & (i >= j)
    strict = i > j
    top = lax.broadcasted_iota(jnp.int32, (n2, GDN_HEAD), 0) < c
    pick = lambda x: jnp.where(top, x[:, :GDN_HEAD], x[:, GDN_HEAD:])

    g_c = [st(u[3]) for u in units]
    b_c = [st(u[4]) for u in units]
    k2 = [jnp.concatenate([u[1], u[1]], axis=0) for u in units]
    q2 = [jnp.concatenate([u[0], u[0]], axis=0) for u in units]
    kb = [a * b for a, b in zip(k2, b_c)]
    kq = [_mm_nt(jnp.concatenate([a, b], axis=0), kk) for a, b, kk in zip(kb, q2, k2)]
    gamma = [jnp.exp(jnp.where(incl, g - g.T, -jnp.inf)) for g in g_c]
    l_neg = [jnp.where(strict, -(x[:n2] * gm), 0.0) for x, gm in zip(kq, gamma)]
    a_qk = [(x[n2:] * gm).astype(BF16) for x, gm in zip(kq, gamma)]
    t_inv = _unit_lower_inverse(l_neg)
    e_g = [jnp.exp(g) for g in g_c]
    uw = [_mm(t, jnp.concatenate([st(u[2]) * b, kbi * eg], axis=1))
          for t, u, b, kbi, eg in zip(t_inv, units, b_c, kb, e_g)]
    wq = [_mm(jnp.concatenate([x[:, GDN_HEAD:], qq * eg], axis=0), u[5])
          for x, qq, eg, u in zip(uw, q2, e_g, units)]
    v_new = [x[:, :GDN_HEAD] - pick(w[:n2]) for x, w in zip(uw, wq)]
    o_s = [pick(w[n2:]) + _mm(a, vn) for w, a, vn in zip(wq, a_qk, v_new)]
    outs, s_new = [], []
    for g, kk, vn, o, u in zip(g_c, k2, v_new, o_s, units):
        gl0 = g[c - 1:c, :]
        gl1 = g[n2 - 1:n2, :]
        k_g = kk * jnp.exp(jnp.where(top, gl0, gl1) - g)
        v_bd = jnp.concatenate([jnp.where(top, vn, 0.0), jnp.where(top, 0.0, vn)], axis=1)
        decay = jnp.concatenate([jnp.exp(gl0), jnp.exp(gl1)], axis=1)
        s_new.append(u[5] * decay + _mm_tn(k_g, v_bd))
        outs.append(jnp.concatenate([o[:c], o[c:]], axis=1))
    return outs, s_new


def _gdn_rec_kernel(q_ref, k_ref, v_ref, z_ref, g_ref, b_ref, nw_ref, o_ref, s_ref):
    bb, tt = q_ref.shape[0], q_ref.shape[1]
    pw = 2 * GDN_HEAD

    @pl.when(pl.program_id(1) == 0)
    def _():
        s_ref[...] = jnp.zeros(s_ref.shape, F32)

    ids = [(bi, h) for bi in range(bb) for h in range(GDN_QK_HEADS)]

    def body(ci, carry):
        rows = pl.ds(pl.multiple_of(ci * CHUNK, CHUNK), CHUNK)
        units = []
        for n, (bi, h) in enumerate(ids):
            cq = slice(h * GDN_HEAD, (h + 1) * GDN_HEAD)
            cv = slice(h * pw, (h + 1) * pw)
            units.append((q_ref[bi, rows, cq], k_ref[bi, rows, cq], v_ref[bi, rows, cv],
                          g_ref[bi, rows, cv], b_ref[bi, rows, cv], s_ref[n]))
        outs, s_new = _gdn_chunk_units(units)
        for n, (bi, h) in enumerate(ids):
            s_ref[n] = s_new[n]
            for e in range(2):
                cols = slice(h * pw + e * GDN_HEAD, h * pw + (e + 1) * GDN_HEAD)
                oh = outs[n][:, e * GDN_HEAD:(e + 1) * GDN_HEAD]
                oh = oh * lax.rsqrt(jnp.mean(oh * oh, axis=-1, keepdims=True) + GDN_NORM_EPS) * nw_ref[...]
                o_ref[bi, rows, cols] = (oh * _silu(z_ref[bi, rows, cols])).astype(BF16)
        return carry

    lax.fori_loop(0, tt // CHUNK, body, 0)


def _gdn_rec(q, k, v, proj, g_b, beta_b, lp, batch, seq, tt, bb):
    nt = seq // tt
    qk_blk = pl.BlockSpec((bb, tt, GDN_QK_DIM), lambda b, t: (b, t, 0))
    v_blk = pl.BlockSpec((bb, tt, GDN_V_DIM), lambda b, t: (b, t, 0))
    z_blk = pl.BlockSpec((bb, tt, GDN_V_DIM), lambda b, t: (b, t, COL_GZ // GDN_V_DIM))
    return pl.pallas_call(
        _gdn_rec_kernel,
        grid=(batch // bb, nt),
        in_specs=[qk_blk, qk_blk, v_blk, z_blk, v_blk, v_blk,
                  pl.BlockSpec((1, GDN_HEAD), lambda b, t: (0, 0))],
        out_specs=v_blk,
        out_shape=jax.ShapeDtypeStruct((batch, seq, GDN_V_DIM), BF16),
        scratch_shapes=[pltpu.VMEM((bb * GDN_QK_HEADS, GDN_HEAD, 2 * GDN_HEAD), F32)],
        compiler_params=pltpu.CompilerParams(
            dimension_semantics=("parallel", "arbitrary"), vmem_limit_bytes=VMEM_LIMIT),
        name="gdn_recurrence",
    )(q, k, v, proj, g_b, beta_b, lp["norm_w"])


def _pad_cols(a, n):
    return jnp.pad(a, [(0, 0)] * (a.ndim - 1) + [(0, n - a.shape[-1])])


def _pad_rows(a, n):
    return jnp.pad(a, [(0, 0)] * (a.ndim - 2) + [(0, n - a.shape[-2]), (0, 0)])


def _proj_layout(main, vd):
    o = 0
    parts = {}
    for name, n in (("r", RWKV_DIM), ("k", RWKV_DIM), ("v", RWKV_DIM), ("wd", RWKV_DECAY_LORA),
                    ("ad", RWKV_AAA_LORA), ("gd", RWKV_GATE_LORA), ("gq", GDN_QK_DIM), ("gk", GDN_QK_DIM),
                    ("gv", GDN_V_DIM), ("gz", GDN_V_DIM), ("ga", GDN_V_HEADS), ("gb", GDN_V_HEADS)):
        parts[name] = main[..., o:o + n]
        o += n
    return jnp.concatenate(
        [parts["r"], parts["k"], parts["v"], parts["gq"], parts["gk"], parts["gv"], parts["gz"],
         _pad_cols(parts["wd"], 128), _pad_cols(parts["ad"], 128), _pad_cols(parts["gd"], 256),
         _pad_cols(vd, 128), _pad_cols(jnp.concatenate([parts["ga"], parts["gb"]], axis=-1), 128)],
        axis=-1)


def kernel(x, attn_norm_w, w_in, rwkv_mu, rwkv_w0, rwkv_w_up, rwkv_a0, rwkv_a_up, rwkv_g_up,
           rwkv_k_k, rwkv_k_a, rwkv_r_k, rwkv_ln_w, rwkv_ln_b, vres_down, vres_mu, vres_up, vres_v0,
           gdn_conv_w, gdn_A_log, gdn_dt_bias, gdn_norm_w, w_out, ffn_norm_w, ffn_w_gate, ffn_w_up,
           ffn_w_down, final_norm_w):
    batch, seq, d = x.shape
    depth = w_in.shape[0]
    m = batch * seq

    vd_w = jnp.concatenate([jnp.zeros((1, d, RWKV_MV_LORA), F32), vres_down], axis=0)
    w_proj = _proj_layout(w_in, vd_w).astype(BF16)
    gdn_pad = jnp.zeros((depth, w_in.shape[2] - rwkv_mu.shape[1]), F32)
    vmu = jnp.concatenate([jnp.zeros((1, RWKV_MV_LORA), F32), vres_mu], axis=0)
    mu_all = _proj_layout(jnp.concatenate([rwkv_mu, gdn_pad], axis=1), vmu)
    w_up = _pad_rows(rwkv_w_up, 128).astype(BF16)
    a_up = _pad_rows(rwkv_a_up, 128).astype(BF16)
    g_up = _pad_rows(rwkv_g_up, 256).astype(BF16)
    v_up = _pad_rows(vres_up, 128).astype(BF16)
    w_out_b = w_out.astype(BF16)
    wg_b = ffn_w_gate.astype(BF16)
    wu_b = ffn_w_up.astype(BF16)
    wd_b = ffn_w_down.astype(BF16)
    a_log = _pad_cols(gdn_A_log, LANES)
    dt_bias = _pad_cols(gdn_dt_bias, LANES)
    r_k = rwkv_r_k.reshape(depth, RWKV_DIM)

    tm = min(1024, m)
    tb = min(256, seq)
    tt = min(512, seq)
    row = lambda a, l: a[l][None, :]

    xf = x.reshape(m, d)
    v_first = None
    for l in range(depth):
        proj = _norm_matmul(xf, row(attn_norm_w, l), w_proj[l], tm, 1152)
        lp = dict(mu_rkv=mu_all[l][None, COL_RKV:COL_RKV + 3 * RWKV_DIM],
                  mu_sm=mu_all[l][None, COL_SM:COL_SM + SM_W],
                  w0=row(rwkv_w0, l), w_up=w_up[l], a0=row(rwkv_a0, l), a_up=a_up[l], g_up=g_up[l],
                  k_k=row(rwkv_k_k, l), k_a=row(rwkv_k_a, l), r_k=row(r_k, l),
                  ln_w=row(rwkv_ln_w, l), ln_b=row(rwkv_ln_b, l),
                  conv_w=gdn_conv_w[l], a_log=row(a_log, l), dt_bias=row(dt_bias, l),
                  norm_w=row(gdn_norm_w, l))
        if l > 0:
            lp["v_up"] = v_up[l - 1]
            lp["v0"] = row(vres_v0, l - 1)
        r, k, v, lw, kr, eta, g = _rwkv_prep(proj, v_first, lp, batch, seq, tb)
        if l == 0:
            v_first = v
        as3 = lambda a: a.reshape(batch, seq, a.shape[-1])
        bb_a = 2 if batch % 2 == 0 else 1
        y_a = _rwkv_rec(*map(as3, (r, k, v, lw, kr, eta, g)), lp, batch, seq, min(128, seq),
                        bb_a).reshape(m, RWKV_DIM)
        gq, gk, gv, g_b, beta_b = _gdn_prep(proj, lp, batch, seq, tb)
        bb_b = 4 if batch % 4 == 0 else bb_a
        y_b = _gdn_rec(as3(gq), as3(gk), as3(gv), as3(proj), as3(g_b), as3(beta_b), lp,
                       batch, seq, min(128, seq), bb_b).reshape(m, GDN_V_DIM)
        xf = _matmul2_res(y_a, y_b, w_out_b[l], xf, tm, 1024)
        hmid = _norm_swiglu(xf, row(ffn_norm_w, l), wg_b[l], wu_b[l], tm, 512)
        xf = _matmul_res(hmid, wd_b[l], xf, tm, 512)
    out = _rmsnorm(xf, final_norm_w[None, :], min(512, m))
    return out.reshape(batch, seq, d)
```

```python
import functools

import jax
import jax.numpy as jnp
from jax import lax
from jax.experimental import pallas as pl
from jax.experimental.pallas import tpu as pltpu

F32 = jnp.float32
BF16 = jnp.bfloat16

D_MODEL = 2048
RWKV_HEAD = 64
RWKV_DIM = 1024
RWKV_DECAY_LORA = 64
RWKV_AAA_LORA = 64
RWKV_MV_LORA = 32
RWKV_GATE_LORA = 160
RWKV_GN_EPS = RWKV_HEAD * 1e-5
RWKV_L2_EPS = 1e-12
GDN_HEAD = 128
GDN_V_DIM = 1024
GDN_V_HEADS = 8
GDN_QK_HEADS = 4
GDN_QK_DIM = 512
GDN_CONV = 4
GDN_L2_EPS = 1e-6
GDN_NORM_EPS = 1e-6
D_FF = 5632
NORM_EPS = 1e-5

LANES = 128
SUBLANES = 8
CHUNK = 64
PAIR = 2 * CHUNK
VMEM_LIMIT = 56 * 1024 * 1024

COL_RKV = 0
COL_GQK = 3 * RWKV_DIM
COL_GV = COL_GQK + 2 * GDN_QK_DIM
COL_GZ = COL_GV + GDN_V_DIM
COL_SM = COL_GZ + GDN_V_DIM
SM_WD, SM_AD, SM_GD, SM_VD, SM_AB = 0, 128, 256, 512, 640
SM_W = 768
N_PROJ = COL_SM + SM_W


def _mm(a, b):
    return jnp.dot(a.astype(BF16), b.astype(BF16), preferred_element_type=F32)


def _mm_nt(a, b):
    return lax.dot_general(a.astype(BF16), b.astype(BF16), (((1,), (1,)), ((), ())),
                           preferred_element_type=F32)


def _mm_tn(a, b):
    return lax.dot_general(a.astype(BF16), b.astype(BF16), (((0,), (0,)), ((), ())),
                           preferred_element_type=F32)


def _split3(x):
    hi = x.astype(BF16)
    r1 = x - hi.astype(F32)
    mid = r1.astype(BF16)
    lo = (r1 - mid.astype(F32)).astype(BF16)
    return hi, mid, lo


def _mm_exact_lhs(a01, x):
    n = x.shape[1]
    hi, mid, lo = _split3(x)
    y = jnp.dot(a01, jnp.concatenate([hi, mid, lo], axis=1), preferred_element_type=F32)
    return y[:, :n] + y[:, n:2 * n] + y[:, 2 * n:]


def _mm_exact_rhs(x, b01):
    m = x.shape[0]
    hi, mid, lo = _split3(x)
    y = jnp.dot(jnp.concatenate([hi, mid, lo], axis=0), b01, preferred_element_type=F32)
    return y[:m] + y[m:2 * m] + y[2 * m:]


def _sigmoid(x):
    return 1.0 / (1.0 + jnp.exp(-x))


def _softplus(x):
    return jnp.maximum(x, 0.0) + jnp.log(1.0 + jnp.exp(-jnp.abs(x)))


def _silu(x):
    return x * _sigmoid(x)


def _iota2(shape):
    return (lax.broadcasted_iota(jnp.int32, shape, 0), lax.broadcasted_iota(jnp.int32, shape, 1))


def _unit_lower_inverse(n_mats):
    sz = n_mats[0].shape[0]
    i, j = _iota2((sz, sz))
    eye = (i == j).astype(F32)
    blk8 = (i // 8) == (j // 8)
    d1 = [jnp.where(blk8, n, 0.0) for n in n_mats]
    d2 = [_mm(x, x) for x in d1]
    d4 = [_mm(x, x) for x in d2]
    p = [eye + x for x in d1]
    p = [x + _mm(x, y) for x, y in zip(p, d2)]
    p = [x + _mm(x, y) for x, y in zip(p, d4)]
    for s in (8, 16, 32):
        off = ((i // (2 * s)) == (j // (2 * s))) & ((i // s) == (j // s) + 1)
        q = [_mm(jnp.where(off, n, 0.0), x) for n, x in zip(n_mats, p)]
        p = [x + _mm(x, y) for x, y in zip(p, q)]
    return p


def _norm_matmul_kernel(x_ref, nw_ref, w_ref, o_ref, h_ref):
    @pl.when(pl.program_id(1) == 0)
    def _():
        x = x_ref[...]
        ms = jnp.mean(x * x, axis=-1, keepdims=True)
        h_ref[...] = (x * lax.rsqrt(ms + NORM_EPS) * nw_ref[...]).astype(BF16)

    o_ref[...] = jnp.dot(h_ref[...], w_ref[...], preferred_element_type=F32)


def _norm_matmul(x, nw, w, tm, tn):
    m, d = x.shape
    n = w.shape[1]
    return pl.pallas_call(
        _norm_matmul_kernel,
        grid=(m // tm, n // tn),
        in_specs=[pl.BlockSpec((tm, d), lambda i, j: (i, 0)),
                  pl.BlockSpec((1, d), lambda i, j: (0, 0)),
                  pl.BlockSpec((d, tn), lambda i, j: (0, j))],
        out_specs=pl.BlockSpec((tm, tn), lambda i, j: (i, j)),
        out_shape=jax.ShapeDtypeStruct((m, n), F32),
        scratch_shapes=[pltpu.VMEM((tm, d), BF16)],
        compiler_params=pltpu.CompilerParams(
            dimension_semantics=("parallel", "arbitrary"), vmem_limit_bytes=VMEM_LIMIT),
        name="norm_in_proj",
    )(x, nw, w)


def _norm_swiglu_kernel(x_ref, nw_ref, wg_ref, wu_ref, o_ref, h_ref):
    @pl.when(pl.program_id(1) == 0)
    def _():
        x = x_ref[...]
        ms = jnp.mean(x * x, axis=-1, keepdims=True)
        h_ref[...] = (x * lax.rsqrt(ms + NORM_EPS) * nw_ref[...]).astype(BF16)

    h = h_ref[...]
    g = jnp.dot(h, wg_ref[...], preferred_element_type=F32)
    u = jnp.dot(h, wu_ref[...], preferred_element_type=F32)
    o_ref[...] = (_silu(g) * u).astype(BF16)


def _norm_swiglu(x, nw, wg, wu, tm, tn):
    m, d = x.shape
    n = wg.shape[1]
    return pl.pallas_call(
        _norm_swiglu_kernel,
        grid=(m // tm, n // tn),
        in_specs=[pl.BlockSpec((tm, d), lambda i, j: (i, 0)),
                  pl.BlockSpec((1, d), lambda i, j: (0, 0)),
                  pl.BlockSpec((d, tn), lambda i, j: (0, j)),
                  pl.BlockSpec((d, tn), lambda i, j: (0, j))],
        out_specs=pl.BlockSpec((tm, tn), lambda i, j: (i, j)),
        out_shape=jax.ShapeDtypeStruct((m, n), BF16),
        scratch_shapes=[pltpu.VMEM((tm, d), BF16)],
        compiler_params=pltpu.CompilerParams(
            dimension_semantics=("parallel", "arbitrary"), vmem_limit_bytes=VMEM_LIMIT),
        name="norm_swiglu",
    )(x, nw, wg, wu)


def _matmul_res_kernel(a_ref, w_ref, r_ref, o_ref):
    o_ref[...] = r_ref[...] + jnp.dot(a_ref[...], w_ref[...], preferred_element_type=F32)


def _matmul_res(a, w, res, tm, tn):
    m, k = a.shape
    n = w.shape[1]
    return pl.pallas_call(
        _matmul_res_kernel,
        grid=(m // tm, n // tn),
        in_specs=[pl.BlockSpec((tm, k), lambda i, j: (i, 0)),
                  pl.BlockSpec((k, tn), lambda i, j: (0, j)),
                  pl.BlockSpec((tm, tn), lambda i, j: (i, j))],
        out_specs=pl.BlockSpec((tm, tn), lambda i, j: (i, j)),
        out_shape=jax.ShapeDtypeStruct((m, n), F32),
        compiler_params=pltpu.CompilerParams(
            dimension_semantics=("parallel", "arbitrary"), vmem_limit_bytes=VMEM_LIMIT),
        name="matmul_residual",
    )(a, w, res)


def _matmul2_res_kernel(a_ref, b_ref, wa_ref, wb_ref, r_ref, o_ref):
    acc = jnp.dot(a_ref[...], wa_ref[...], preferred_element_type=F32)
    acc = acc + jnp.dot(b_ref[...], wb_ref[...], preferred_element_type=F32)
    o_ref[...] = r_ref[...] + acc


def _matmul2_res(a, b, w, res, tm, tn):
    m, k = a.shape
    n = w.shape[1]
    return pl.pallas_call(
        _matmul2_res_kernel,
        grid=(m // tm, n // tn),
        in_specs=[pl.BlockSpec((tm, k), lambda i, j: (i, 0)),
                  pl.BlockSpec((tm, k), lambda i, j: (i, 0)),
                  pl.BlockSpec((k, tn), lambda i, j: (0, j)),
                  pl.BlockSpec((k, tn), lambda i, j: (1, j)),
                  pl.BlockSpec((tm, tn), lambda i, j: (i, j))],
        out_specs=pl.BlockSpec((tm, tn), lambda i, j: (i, j)),
        out_shape=jax.ShapeDtypeStruct((m, n), F32),
        compiler_params=pltpu.CompilerParams(
            dimension_semantics=("parallel", "arbitrary"), vmem_limit_bytes=VMEM_LIMIT),
        name="out_proj_residual",
    )(a, b, w, w, res)


def _rmsnorm_kernel(x_ref, nw_ref, o_ref):
    x = x_ref[...]
    ms = jnp.mean(x * x, axis=-1, keepdims=True)
    o_ref[...] = x * lax.rsqrt(ms + NORM_EPS) * nw_ref[...]


def _rmsnorm(x, nw, tm):
    m, d = x.shape
    return pl.pallas_call(
        _rmsnorm_kernel,
        grid=(m // tm,),
        in_specs=[pl.BlockSpec((tm, d), lambda i: (i, 0)),
                  pl.BlockSpec((1, d), lambda i: (0, 0))],
        out_specs=pl.BlockSpec((tm, d), lambda i: (i, 0)),
        out_shape=jax.ShapeDtypeStruct((m, d), F32),
        compiler_params=pltpu.CompilerParams(dimension_semantics=("parallel",)),
        name="final_rmsnorm",
    )(x, nw)


def _rows_back(x, tail, n_back):
    c = x.shape[0]
    ext = jnp.concatenate([tail, x], axis=0)
    return [pltpu.roll(ext, s, 0)[SUBLANES:SUBLANES + c, :] for s in range(1, n_back + 1)]


def _rwkv_chunk_units(units, ones_bd, tri, m0):
    c = CHUNK
    n2 = 2 * c
    ss = [_mm(u[4] * u[4], ones_bd) for u in units]
    log_w = [_mm_exact_lhs(tri, u[3]) for u in units]
    bon = [_mm(u[0] * u[1] * u[6], ones_bd) for u in units]

    def bd(x):
        return jnp.concatenate([jnp.where(m0, x, 0.0), jnp.where(m0, 0.0, x)], axis=0).astype(BF16)

    lhs, rhs, bk, v_bd, w_last = [], [], [], [], []
    for (r, k, v, lw, kraw, eta, _, _), s2, lg in zip(units, ss, log_w):
        kk = kraw * lax.rsqrt(s2 + RWKV_L2_EPS)
        beta = kk * eta
        lw_last = lg[c - 1:c, :]
        w_inv = jnp.exp(-lg)
        w_end = jnp.exp(lw_last - lg)
        lhs.append(jnp.concatenate([bd(-kk * jnp.exp(lg - lw)), bd(r * jnp.exp(lg))], axis=0))
        rhs.append(jnp.concatenate([bd(beta * w_inv), bd(k * w_inv)], axis=0))
        bk.append(jnp.concatenate([bd(beta * w_end), bd(k * w_end)], axis=0))
        v_bd.append(bd(v))
        w_last.append(jnp.exp(lw_last))

    sc = [_mm_nt(a, b) for a, b in zip(lhs, rhs)]
    xr = [_mm_nt(a, u[7]) for a, u in zip(lhs, units)]
    i, j = _iota2((n2, n2))
    strict = i > j
    incl = i >= j
    a_ab = [jnp.where(strict, x[:n2, :n2], 0.0) for x in sc]
    av = [_mm(jnp.where(strict, x[:n2, n2:], 0.0), vb) for x, vb in zip(sc, v_bd)]
    a_r = [jnp.where(jnp.concatenate([incl, incl], axis=1), x[n2:, :], 0.0).astype(BF16) for x in sc]
    t_inv = _unit_lower_inverse(a_ab)
    u_mat = [_mm(t, x[:n2] + a) for t, x, a in zip(t_inv, xr, av)]
    uv = [jnp.concatenate([x.astype(BF16), vb], axis=0) for x, vb in zip(u_mat, v_bd)]
    y_bd = [x[n2:] + _mm(a, z) for x, a, z in zip(xr, a_r, uv)]
    s_new = [u[7] * wl + _mm_tn(z, b) for u, wl, z, b in zip(units, w_last, uv, bk)]
    y = [x[:c] + x[c:] for x in y_bd]
    return y, bon, s_new


def _rwkv_mix_kernel(has_vres, *refs):
    if has_vres:
        (rkv_ref, sm_ref, vf_ref, mu_rkv_ref, mu_sm_ref, w0_ref, wup_ref, a0_ref, aup_ref, gup_ref,
         kk_ref, ka_ref, rk_ref, lnw_ref, lnb_ref, vup_ref, v0_ref,
         o_ref, s_ref, tail_rkv, tail_sm) = refs
        vout_ref = None
    else:
        (rkv_ref, sm_ref, mu_rkv_ref, mu_sm_ref, w0_ref, wup_ref, a0_ref, aup_ref, gup_ref,
         kk_ref, ka_ref, rk_ref, lnw_ref, lnb_ref,
         o_ref, vout_ref, s_ref, tail_rkv, tail_sm) = refs
    bb, tt = rkv_ref.shape[0], rkv_ref.shape[1]
    n_pairs = RWKV_DIM // LANES

    @pl.when(pl.program_id(1) == 0)
    def _():
        s_ref[...] = jnp.zeros(s_ref.shape, F32)
        tail_rkv[...] = jnp.zeros(tail_rkv.shape, F32)
        tail_sm[...] = jnp.zeros(tail_sm.shape, F32)

    li, lj = _iota2((LANES, LANES))
    ones_bd = ((li // RWKV_HEAD) == (lj // RWKV_HEAD)).astype(BF16)
    ti, tj = _iota2((CHUNK, CHUNK))
    tri = (ti >= tj).astype(BF16)
    m0 = lax.broadcasted_iota(jnp.int32, (CHUNK, LANES), 1) < RWKV_HEAD
    inv_n = 1.0 / RWKV_HEAD
    ids = [(bi, slice(p * LANES, (p + 1) * LANES)) for bi in range(bb) for p in range(n_pairs)]

    def body(ci, carry):
        rows = pl.ds(pl.multiple_of(ci * CHUNK, CHUNK), CHUNK)
        sh, shs = [], []
        for bi in range(bb):
            p = rkv_ref[bi, rows, :]
            (prev,) = _rows_back(p, tail_rkv[bi], 1)
            tail_rkv[bi] = p[CHUNK - SUBLANES:, :]
            sh.append(p + (prev - p) * mu_rkv_ref[...])
            ps = sm_ref[bi, rows, :]
            (prev_s,) = _rows_back(ps, tail_sm[bi], 1)
            tail_sm[bi] = ps[CHUNK - SUBLANES:, :]
            shs.append(ps + (prev_s - ps) * mu_sm_ref[...])
        w_lo = [_mm(jnp.tanh(x[:, SM_WD:SM_WD + 128]), wup_ref[...]) for x in shs]
        a_lo = [_mm(x[:, SM_AD:SM_AD + 128], aup_ref[...]) for x in shs]
        g_all = [_mm(_sigmoid(x[:, SM_GD:SM_GD + 256]), gup_ref[...]) for x in shs]
        if has_vres:
            v_lo = [_mm(x[:, SM_VD:SM_VD + 128], vup_ref[...]) for x in shs]
        r_all, k_all, v_all, lw_all, kr_all, eta_all = [], [], [], [], [], []
        for bi in range(bb):
            w = -_softplus(-(w0_ref[...] + w_lo[bi])) - 0.5
            lw_all.append(-jnp.exp(w))
            eta = _sigmoid(a0_ref[...] + a_lo[bi])
            k = sh[bi][:, RWKV_DIM:2 * RWKV_DIM]
            v = sh[bi][:, 2 * RWKV_DIM:3 * RWKV_DIM]
            if has_vres:
                v = v + (vf_ref[bi, rows, :] - v) * _sigmoid(v0_ref[...] + v_lo[bi])
            else:
                vout_ref[bi, rows, :] = v
            r_all.append(sh[bi][:, 0:RWKV_DIM])
            v_all.append(v)
            kr_all.append(k * kk_ref[...])
            k_all.append(k * (1.0 + (eta - 1.0) * ka_ref[...]))
            eta_all.append(eta)
        units = [tuple(a[bi][:, cs] for a in (r_all, k_all, v_all, lw_all, kr_all, eta_all))
                 + (rk_ref[:, cs], s_ref[n]) for n, (bi, cs) in enumerate(ids)]
        y, bon, s_new = _rwkv_chunk_units(units, ones_bd, tri, m0)
        mean = [_mm(x, ones_bd) * inv_n for x in y]
        d = [x - mu for x, mu in zip(y, mean)]
        var = [_mm(x * x, ones_bd) * inv_n for x in d]
        for n, (bi, cs) in enumerate(ids):
            yn = d[n] * lax.rsqrt(var[n] + RWKV_GN_EPS) * lnw_ref[:, cs] + lnb_ref[:, cs]
            o_ref[bi, rows, cs] = ((yn + bon[n] * units[n][2]) * g_all[bi][:, cs]).astype(BF16)
            s_ref[n] = s_new[n]
        return carry

    lax.fori_loop(0, tt // CHUNK, body, 0)


def _rwkv_mix(proj, v_first, lp, batch, seq, tt, bb):
    nt = seq // tt
    has_vres = v_first is not None
    full = lambda b, t: (0, 0)
    wide = pl.BlockSpec((bb, tt, RWKV_DIM), lambda b, t: (b, t, 0))
    vec = pl.BlockSpec((1, RWKV_DIM), full)
    lora = pl.BlockSpec((128, RWKV_DIM), full)
    in_specs = [pl.BlockSpec((bb, tt, 3 * RWKV_DIM), lambda b, t: (b, t, COL_RKV // (3 * RWKV_DIM))),
                pl.BlockSpec((bb, tt, SM_W), lambda b, t: (b, t, COL_SM // SM_W))]
    args = [proj, proj]
    if has_vres:
        in_specs.append(wide)
        args.append(v_first)
    in_specs += [pl.BlockSpec((1, 3 * RWKV_DIM), full), pl.BlockSpec((1, SM_W), full),
                 vec, lora, vec, lora, pl.BlockSpec((256, RWKV_DIM), full), vec, vec, vec, vec, vec]
    args += [lp["mu_rkv"], lp["mu_sm"], lp["w0"], lp["w_up"], lp["a0"], lp["a_up"], lp["g_up"],
             lp["k_k"], lp["k_a"], lp["r_k"], lp["ln_w"], lp["ln_b"]]
    y_shape = jax.ShapeDtypeStruct((batch, seq, RWKV_DIM), BF16)
    if has_vres:
        in_specs += [lora, vec]
        args += [lp["v_up"], lp["v0"]]
        out_specs, out_shape = wide, y_shape
    else:
        out_specs = [wide, wide]
        out_shape = [y_shape, jax.ShapeDtypeStruct((batch, seq, RWKV_DIM), F32)]
    return pl.pallas_call(
        functools.partial(_rwkv_mix_kernel, has_vres),
        grid=(batch // bb, nt),
        in_specs=in_specs,
        out_specs=out_specs,
        out_shape=out_shape,
        scratch_shapes=[pltpu.VMEM((bb * RWKV_DIM // LANES, LANES, LANES), F32),
                        pltpu.VMEM((bb, SUBLANES, 3 * RWKV_DIM), F32),
                        pltpu.VMEM((bb, SUBLANES, SM_W), F32)],
        compiler_params=pltpu.CompilerParams(
            dimension_semantics=("parallel", "arbitrary"), vmem_limit_bytes=VMEM_LIMIT),
        name="rwkv_mix",
    )(*args)


def _gdn_chunk_prep(x, tail, ab, cw, a_log, dt_bias, tri, e_g, e_b):
    x1, x2, x3 = _rows_back(x, tail, GDN_CONV - 1)
    y = _silu(x * cw[3:4, :] + x1 * cw[2:3, :] + x2 * cw[1:2, :] + x3 * cw[0:1, :])
    q, k = [], []
    for h in range(GDN_QK_HEADS):
        qh = y[:, h * GDN_HEAD:(h + 1) * GDN_HEAD]
        kh = y[:, GDN_QK_DIM + h * GDN_HEAD:GDN_QK_DIM + (h + 1) * GDN_HEAD]
        q.append(qh * (lax.rsqrt(jnp.sum(qh * qh, axis=-1, keepdims=True) + GDN_L2_EPS)
                       * (GDN_HEAD ** -0.5)))
        k.append(kh * lax.rsqrt(jnp.sum(kh * kh, axis=-1, keepdims=True) + GDN_L2_EPS))
    g = -jnp.exp(a_log) * _softplus(ab + dt_bias)
    beta = _sigmoid(ab)
    g_b = _mm_exact_rhs(_mm_exact_lhs(tri, g), e_g)
    beta_b = _mm_exact_rhs(beta, e_b)
    return q, k, y[:, 2 * GDN_QK_DIM:], g_b, beta_b


def _gdn_chunk_units(units):
    c = CHUNK
    n2 = 2 * c
    st = lambda x: jnp.concatenate([x[:, :GDN_HEAD], x[:, GDN_HEAD:]], axis=0)
    i, j = _iota2((n2, n2))
    incl = ((i // c) == (j // c)) & (i >= j)
    strict = i > j
    top = lax.broadcasted_iota(jnp.int32, (n2, GDN_HEAD), 0) < c
    pick = lambda x: jnp.where(top, x[:, :GDN_HEAD], x[:, GDN_HEAD:])

    g_c = [st(u[3]) for u in units]
    b_c = [st(u[4]) for u in units]
    k2 = [jnp.concatenate([u[1], u[1]], axis=0) for u in units]
    q2 = [jnp.concatenate([u[0], u[0]], axis=0) for u in units]
    kb = [a * b for a, b in zip(k2, b_c)]
    kq = [_mm_nt(jnp.concatenate([a, b], axis=0), kk) for a, b, kk in zip(kb, q2, k2)]
    gamma = [jnp.exp(jnp.where(incl, g - g.T, -jnp.inf)) for g in g_c]
    l_neg = [jnp.where(strict, -(x[:n2] * gm), 0.0) for x, gm in zip(kq, gamma)]
    a_qk = [(x[n2:] * gm).astype(BF16) for x, gm in zip(kq, gamma)]
    t_inv = _unit_lower_inverse(l_neg)
    e_g = [jnp.exp(g) for g in g_c]
    uw = [_mm(t, jnp.concatenate([st(u[2]) * b, kbi * eg], axis=1))
          for t, u, b, kbi, eg in zip(t_inv, units, b_c, kb, e_g)]
    wq = [_mm(jnp.concatenate([x[:, GDN_HEAD:], qq * eg], axis=0), u[5])
          for x, qq, eg, u in zip(uw, q2, e_g, units)]
    v_new = [x[:, :GDN_HEAD] - pick(w[:n2]) for x, w in zip(uw, wq)]
    o_s = [pick(w[n2:]) + _mm(a, vn) for w, a, vn in zip(wq, a_qk, v_new)]
    outs, s_new = [], []
    for g, kk, vn, o, u in zip(g_c, k2, v_new, o_s, units):
        gl0 = g[c - 1:c, :]
        gl1 = g[n2 - 1:n2, :]
        k_g = kk * jnp.exp(jnp.where(top, gl0, gl1) - g)
        v_bd = jnp.concatenate([jnp.where(top, vn, 0.0), jnp.where(top, 0.0, vn)], axis=1)
        decay = jnp.concatenate([jnp.exp(gl0), jnp.exp(gl1)], axis=1)
        s_new.append(u[5] * decay + _mm_tn(k_g, v_bd))
        outs.append(jnp.concatenate([o[:c], o[c:]], axis=1))
    return outs, s_new


def _gdn_mix_kernel(qk_ref, v_ref, z_ref, sm_ref, cw_ref, alog_ref, dtb_ref, nw_ref,
                    o_ref, s_ref, tail_ref):
    bb, tt = qk_ref.shape[0], qk_ref.shape[1]
    pw = 2 * GDN_HEAD

    @pl.when(pl.program_id(1) == 0)
    def _():
        s_ref[...] = jnp.zeros(s_ref.shape, F32)
        tail_ref[...] = jnp.zeros(tail_ref.shape, F32)

    ids = [(bi, h) for bi in range(bb) for h in range(GDN_QK_HEADS)]
    ti, tj = _iota2((CHUNK, CHUNK))
    tri = (ti >= tj).astype(BF16)
    ei, ej = _iota2((LANES, GDN_V_DIM))
    e_g = (ei == ej // GDN_HEAD).astype(BF16)
    e_b = (ei == ej // GDN_HEAD + GDN_V_HEADS).astype(BF16)

    def body(ci, carry):
        rows = pl.ds(pl.multiple_of(ci * CHUNK, CHUNK), CHUNK)
        rowp = []
        for bi in range(bb):
            x = jnp.concatenate([qk_ref[bi, rows, :], v_ref[bi, rows, :]], axis=1)
            rowp.append(_gdn_chunk_prep(x, tail_ref[bi], sm_ref[bi, rows, SM_AB:SM_AB + LANES],
                                        cw_ref[...], alog_ref[...], dtb_ref[...], tri, e_g, e_b))
            tail_ref[bi] = x[CHUNK - SUBLANES:, :]
        units = []
        for n, (bi, h) in enumerate(ids):
            q, k, v, g_b, beta_b = rowp[bi]
            cv = slice(h * pw, (h + 1) * pw)
            units.append((q[h], k[h], v[:, cv], g_b[:, cv], beta_b[:, cv], s_ref[n]))
        outs, s_new = _gdn_chunk_units(units)
        for n, (bi, h) in enumerate(ids):
            s_ref[n] = s_new[n]
            for e in range(2):
                cols = slice(h * pw + e * GDN_HEAD, h * pw + (e + 1) * GDN_HEAD)
                oh = outs[n][:, e * GDN_HEAD:(e + 1) * GDN_HEAD]
                oh = oh * lax.rsqrt(jnp.mean(oh * oh, axis=-1, keepdims=True) + GDN_NORM_EPS) * nw_ref[...]
                o_ref[bi, rows, cols] = (oh * _silu(z_ref[bi, rows, cols])).astype(BF16)
        return carry

    lax.fori_loop(0, tt // CHUNK, body, 0)


def _gdn_mix(proj, lp, batch, seq, tt, bb):
    nt = seq // tt
    full = lambda b, t: (0, 0)
    qk_w = 2 * GDN_QK_DIM
    wide = lambda col: pl.BlockSpec((bb, tt, GDN_V_DIM), lambda b, t: (b, t, col // GDN_V_DIM))
    return pl.pallas_call(
        _gdn_mix_kernel,
        grid=(batch // bb, nt),
        in_specs=[wide(COL_GQK), wide(COL_GV), wide(COL_GZ),
                  pl.BlockSpec((bb, tt, SM_W), lambda b, t: (b, t, COL_SM // SM_W)),
                  pl.BlockSpec((GDN_CONV, qk_w + GDN_V_DIM), full),
                  pl.BlockSpec((1, LANES), full),
                  pl.BlockSpec((1, LANES), full),
                  pl.BlockSpec((1, GDN_HEAD), full)],
        out_specs=wide(0),
        out_shape=jax.ShapeDtypeStruct((batch, seq, GDN_V_DIM), BF16),
        scratch_shapes=[pltpu.VMEM((bb * GDN_QK_HEADS, GDN_HEAD, 2 * GDN_HEAD), F32),
                        pltpu.VMEM((bb, SUBLANES, qk_w + GDN_V_DIM), F32)],
        compiler_params=pltpu.CompilerParams(
            dimension_semantics=("parallel", "arbitrary"), vmem_limit_bytes=VMEM_LIMIT),
        name="gdn_mix",
    )(proj, proj, proj, proj, lp["conv_w"], lp["a_log"], lp["dt_bias"], lp["norm_w"])


def _pad_cols(a, n):
    return jnp.pad(a, [(0, 0)] * (a.ndim - 1) + [(0, n - a.shape[-1])])


def _pad_rows(a, n):
    return jnp.pad(a, [(0, 0)] * (a.ndim - 2) + [(0, n - a.shape[-2]), (0, 0)])


def _proj_layout(main, vd):
    o = 0
    parts = {}
    for name, n in (("r", RWKV_DIM), ("k", RWKV_DIM), ("v", RWKV_DIM), ("wd", RWKV_DECAY_LORA),
                    ("ad", RWKV_AAA_LORA), ("gd", RWKV_GATE_LORA), ("gq", GDN_QK_DIM), ("gk", GDN_QK_DIM),
                    ("gv", GDN_V_DIM), ("gz", GDN_V_DIM), ("ga", GDN_V_HEADS), ("gb", GDN_V_HEADS)):
        parts[name] = main[..., o:o + n]
        o += n
    return jnp.concatenate(
        [parts["r"], parts["k"], parts["v"], parts["gq"], parts["gk"], parts["gv"], parts["gz"],
         _pad_cols(parts["wd"], 128), _pad_cols(parts["ad"], 128), _pad_cols(parts["gd"], 256),
         _pad_cols(vd, 128), _pad_cols(jnp.concatenate([parts["ga"], parts["gb"]], axis=-1), 128)],
        axis=-1)


def kernel(x, attn_norm_w, w_in, rwkv_mu, rwkv_w0, rwkv_w_up, rwkv_a0, rwkv_a_up, rwkv_g_up,
           rwkv_k_k, rwkv_k_a, rwkv_r_k, rwkv_ln_w, rwkv_ln_b, vres_down, vres_mu, vres_up, vres_v0,
           gdn_conv_w, gdn_A_log, gdn_dt_bias, gdn_norm_w, w_out, ffn_norm_w, ffn_w_gate, ffn_w_up,
           ffn_w_down, final_norm_w):
    batch, seq, d = x.shape
    depth = w_in.shape[0]
    m = batch * seq

    vd_w = jnp.concatenate([jnp.zeros((1, d, RWKV_MV_LORA), F32), vres_down], axis=0)
    w_proj = _proj_layout(w_in, vd_w).astype(BF16)
    gdn_pad = jnp.zeros((depth, w_in.shape[2] - rwkv_mu.shape[1]), F32)
    vmu = jnp.concatenate([jnp.zeros((1, RWKV_MV_LORA), F32), vres_mu], axis=0)
    mu_all = _proj_layout(jnp.concatenate([rwkv_mu, gdn_pad], axis=1), vmu)
    w_up = _pad_rows(rwkv_w_up, 128).astype(BF16)
    a_up = _pad_rows(rwkv_a_up, 128).astype(BF16)
    g_up = _pad_rows(rwkv_g_up, 256).astype(BF16)
    v_up = _pad_rows(vres_up, 128).astype(BF16)
    w_out_b = w_out.astype(BF16)
    wg_b = ffn_w_gate.astype(BF16)
    wu_b = ffn_w_up.astype(BF16)
    wd_b = ffn_w_down.astype(BF16)
    a_log = _pad_cols(gdn_A_log, LANES)
    dt_bias = _pad_cols(gdn_dt_bias, LANES)
    r_k = rwkv_r_k.reshape(depth, RWKV_DIM)

    tm = min(1024, m)
    tb = min(256, seq)
    tt = min(512, seq)
    row = lambda a, l: a[l][None, :]

    xf = x.reshape(m, d)
    v_first = None
    for l in range(depth):
        proj = _norm_matmul(xf, row(attn_norm_w, l), w_proj[l], tm, 1152)
        lp = dict(mu_rkv=mu_all[l][None, COL_RKV:COL_RKV + 3 * RWKV_DIM],
                  mu_sm=mu_all[l][None, COL_SM:COL_SM + SM_W],
                  w0=row(rwkv_w0, l), w_up=w_up[l], a0=row(rwkv_a0, l), a_up=a_up[l], g_up=g_up[l],
                  k_k=row(rwkv_k_k, l), k_a=row(rwkv_k_a, l), r_k=row(r_k, l),
                  ln_w=row(rwkv_ln_w, l), ln_b=row(rwkv_ln_b, l),
                  conv_w=gdn_conv_w[l], a_log=row(a_log, l), dt_bias=row(dt_bias, l),
                  norm_w=row(gdn_norm_w, l))
        if l > 0:
            lp["v_up"] = v_up[l - 1]
            lp["v0"] = row(vres_v0, l - 1)
        as3 = lambda a: a.reshape(batch, seq, a.shape[-1])
        bb_a = 2 if batch % 2 == 0 else 1
        if l == 0:
            y_a, v_first = _rwkv_mix(as3(proj), None, lp, batch, seq, min(128, seq), bb_a)
        else:
            y_a = _rwkv_mix(as3(proj), v_first, lp, batch, seq, min(128, seq), bb_a)
        y_a = y_a.reshape(m, RWKV_DIM)
        bb_b = 4 if batch % 4 == 0 else bb_a
        y_b = _gdn_mix(as3(proj), lp, batch, seq, min(128, seq), bb_b).reshape(m, GDN_V_DIM)
        xf = _matmul2_res(y_a, y_b, w_out_b[l], xf, tm, 1024)
        hmid = _norm_swiglu(xf, row(ffn_norm_w, l), wg_b[l], wu_b[l], tm, 512)
        xf = _matmul_res(hmid, wd_b[l], xf, tm, 512)
    out = _rmsnorm(xf, final_norm_w[None, :], min(512, m))
    return out.reshape(batch, seq, d)
```

```python
import functools

import jax
import jax.numpy as jnp
from jax import lax
from jax.experimental import pallas as pl
from jax.experimental.pallas import tpu as pltpu

F32 = jnp.float32
BF16 = jnp.bfloat16

D_MODEL = 2048
RWKV_HEAD = 64
RWKV_DIM = 1024
RWKV_DECAY_LORA = 64
RWKV_AAA_LORA = 64
RWKV_MV_LORA = 32
RWKV_GATE_LORA = 160
RWKV_GN_EPS = RWKV_HEAD * 1e-5
RWKV_L2_EPS = 1e-12
GDN_HEAD = 128
GDN_V_DIM = 1024
GDN_V_HEADS = 8
GDN_QK_HEADS = 4
GDN_QK_DIM = 512
GDN_CONV = 4
GDN_L2_EPS = 1e-6
GDN_NORM_EPS = 1e-6
D_FF = 5632
NORM_EPS = 1e-5

LANES = 128
SUBLANES = 8
CHUNK = 64
PAIR = 2 * CHUNK
VMEM_LIMIT = 56 * 1024 * 1024

COL_RKV = 0
COL_GQK = 3 * RWKV_DIM
COL_GV = COL_GQK + 2 * GDN_QK_DIM
COL_GZ = COL_GV + GDN_V_DIM
COL_SM = COL_GZ + GDN_V_DIM
SM_WD, SM_AD, SM_GD, SM_VD, SM_AB = 0, 128, 256, 512, 640
SM_W = 768
N_PROJ = COL_SM + SM_W


def _mm(a, b):
    return jnp.dot(a.astype(BF16), b.astype(BF16), preferred_element_type=F32)


def _mm_nt(a, b):
    return lax.dot_general(a.astype(BF16), b.astype(BF16), (((1,), (1,)), ((), ())),
                           preferred_element_type=F32)


def _mm_tn(a, b):
    return lax.dot_general(a.astype(BF16), b.astype(BF16), (((0,), (0,)), ((), ())),
                           preferred_element_type=F32)


def _split3(x):
    hi = x.astype(BF16)
    r1 = x - hi.astype(F32)
    mid = r1.astype(BF16)
    lo = (r1 - mid.astype(F32)).astype(BF16)
    return hi, mid, lo


def _mm_exact_lhs(a01, x):
    n = x.shape[1]
    hi, mid, lo = _split3(x)
    y = jnp.dot(a01, jnp.concatenate([hi, mid, lo], axis=1), preferred_element_type=F32)
    return y[:, :n] + y[:, n:2 * n] + y[:, 2 * n:]


def _mm_exact_rhs(x, b01):
    m = x.shape[0]
    hi, mid, lo = _split3(x)
    y = jnp.dot(jnp.concatenate([hi, mid, lo], axis=0), b01, preferred_element_type=F32)
    return y[:m] + y[m:2 * m] + y[2 * m:]


def _sigmoid(x):
    return 1.0 / (1.0 + jnp.exp(-x))


def _softplus(x):
    return jnp.maximum(x, 0.0) + jnp.log(1.0 + jnp.exp(-jnp.abs(x)))


def _silu(x):
    return x * _sigmoid(x)


def _iota2(shape):
    return (lax.broadcasted_iota(jnp.int32, shape, 0), lax.broadcasted_iota(jnp.int32, shape, 1))


def _unit_lower_inverse(n_mats):
    sz = n_mats[0].shape[0]
    i, j = _iota2((sz, sz))
    eye = (i == j).astype(F32)
    blk8 = (i // 8) == (j // 8)
    d1 = [jnp.where(blk8, n, 0.0) for n in n_mats]
    d2 = [_mm(x, x) for x in d1]
    d4 = [_mm(x, x) for x in d2]
    p = [eye + x for x in d1]
    p = [x + _mm(x, y) for x, y in zip(p, d2)]
    p = [x + _mm(x, y) for x, y in zip(p, d4)]
    for s in (8, 16, 32):
        off = ((i // (2 * s)) == (j // (2 * s))) & ((i // s) == (j // s) + 1)
        q = [_mm(jnp.where(off, n, 0.0), x) for n, x in zip(n_mats, p)]
        p = [x + _mm(x, y) for x, y in zip(p, q)]
    return p


def _norm_matmul_kernel(x_ref, nw_ref, w_ref, o_ref, h_ref):
    @pl.when(pl.program_id(1) == 0)
    def _():
        x = x_ref[...]
        ms = jnp.mean(x * x, axis=-1, keepdims=True)
        h_ref[...] = (x * lax.rsqrt(ms + NORM_EPS) * nw_ref[...]).astype(BF16)

    o_ref[...] = jnp.dot(h_ref[...], w_ref[...], preferred_element_type=F32)


def _norm_matmul(x, nw, w, tm, tn):
    m, d = x.shape
    n = w.shape[1]
    return pl.pallas_call(
        _norm_matmul_kernel,
        grid=(m // tm, n // tn),
        in_specs=[pl.BlockSpec((tm, d), lambda i, j: (i, 0)),
                  pl.BlockSpec((1, d), lambda i, j: (0, 0)),
                  pl.BlockSpec((d, tn), lambda i, j: (0, j))],
        out_specs=pl.BlockSpec((tm, tn), lambda i, j: (i, j)),
        out_shape=jax.ShapeDtypeStruct((m, n), F32),
        scratch_shapes=[pltpu.VMEM((tm, d), BF16)],
        compiler_params=pltpu.CompilerParams(
            dimension_semantics=("parallel", "arbitrary"), vmem_limit_bytes=VMEM_LIMIT),
        name="norm_in_proj",
    )(x, nw, w)


def _norm_swiglu_kernel(x_ref, nw_ref, wg_ref, wu_ref, o_ref, h_ref):
    @pl.when(pl.program_id(1) == 0)
    def _():
        x = x_ref[...]
        ms = jnp.mean(x * x, axis=-1, keepdims=True)
        h_ref[...] = (x * lax.rsqrt(ms + NORM_EPS) * nw_ref[...]).astype(BF16)

    h = h_ref[...]
    g = jnp.dot(h, wg_ref[...], preferred_element_type=F32)
    u = jnp.dot(h, wu_ref[...], preferred_element_type=F32)
    o_ref[...] = (_silu(g) * u).astype(BF16)


def _norm_swiglu(x, nw, wg, wu, tm, tn):
    m, d = x.shape
    n = wg.shape[1]
    return pl.pallas_call(
        _norm_swiglu_kernel,
        grid=(m // tm, n // tn),
        in_specs=[pl.BlockSpec((tm, d), lambda i, j: (i, 0)),
                  pl.BlockSpec((1, d), lambda i, j: (0, 0)),
                  pl.BlockSpec((d, tn), lambda i, j: (0, j)),
                  pl.BlockSpec((d, tn), lambda i, j: (0, j))],
        out_specs=pl.BlockSpec((tm, tn), lambda i, j: (i, j)),
        out_shape=jax.ShapeDtypeStruct((m, n), BF16),
        scratch_shapes=[pltpu.VMEM((tm, d), BF16)],
        compiler_params=pltpu.CompilerParams(
            dimension_semantics=("parallel", "arbitrary"), vmem_limit_bytes=VMEM_LIMIT),
        name="norm_swiglu",
    )(x, nw, wg, wu)


def _matmul_res_kernel(a_ref, w_ref, r_ref, o_ref):
    o_ref[...] = r_ref[...] + jnp.dot(a_ref[...], w_ref[...], preferred_element_type=F32)


def _matmul_res(a, w, res, tm, tn):
    m, k = a.shape
    n = w.shape[1]
    return pl.pallas_call(
        _matmul_res_kernel,
        grid=(m // tm, n // tn),
        in_specs=[pl.BlockSpec((tm, k), lambda i, j: (i, 0)),
                  pl.BlockSpec((k, tn), lambda i, j: (0, j)),
                  pl.BlockSpec((tm, tn), lambda i, j: (i, j))],
        out_specs=pl.BlockSpec((tm, tn), lambda i, j: (i, j)),
        out_shape=jax.ShapeDtypeStruct((m, n), F32),
        compiler_params=pltpu.CompilerParams(
            dimension_semantics=("parallel", "arbitrary"), vmem_limit_bytes=VMEM_LIMIT),
        name="matmul_residual",
    )(a, w, res)


def _matmul2_res_kernel(a_ref, b_ref, wa_ref, wb_ref, r_ref, o_ref):
    acc = jnp.dot(a_ref[...], wa_ref[...], preferred_element_type=F32)
    acc = acc + jnp.dot(b_ref[...], wb_ref[...], preferred_element_type=F32)
    o_ref[...] = r_ref[...] + acc


def _matmul2_res(a, b, w, res, tm, tn):
    m, k = a.shape
    n = w.shape[1]
    return pl.pallas_call(
        _matmul2_res_kernel,
        grid=(m // tm, n // tn),
        in_specs=[pl.BlockSpec((tm, k), lambda i, j: (i, 0)),
                  pl.BlockSpec((tm, k), lambda i, j: (i, 0)),
                  pl.BlockSpec((k, tn), lambda i, j: (0, j)),
                  pl.BlockSpec((k, tn), lambda i, j: (1, j)),
                  pl.BlockSpec((tm, tn), lambda i, j: (i, j))],
        out_specs=pl.BlockSpec((tm, tn), lambda i, j: (i, j)),
        out_shape=jax.ShapeDtypeStruct((m, n), F32),
        compiler_params=pltpu.CompilerParams(
            dimension_semantics=("parallel", "arbitrary"), vmem_limit_bytes=VMEM_LIMIT),
        name="out_proj_residual",
    )(a, b, w, w, res)


def _rmsnorm_kernel(x_ref, nw_ref, o_ref):
    x = x_ref[...]
    ms = jnp.mean(x * x, axis=-1, keepdims=True)
    o_ref[...] = x * lax.rsqrt(ms + NORM_EPS) * nw_ref[...]


def _rmsnorm(x, nw, tm):
    m, d = x.shape
    return pl.pallas_call(
        _rmsnorm_kernel,
        grid=(m // tm,),
        in_specs=[pl.BlockSpec((tm, d), lambda i: (i, 0)),
                  pl.BlockSpec((1, d), lambda i: (0, 0))],
        out_specs=pl.BlockSpec((tm, d), lambda i: (i, 0)),
        out_shape=jax.ShapeDtypeStruct((m, d), F32),
        compiler_params=pltpu.CompilerParams(dimension_semantics=("parallel",)),
        name="final_rmsnorm",
    )(x, nw)


def _rows_back(x, tail, n_back):
    c = x.shape[0]
    ext = jnp.concatenate([tail, x], axis=0)
    return [pltpu.roll(ext, s, 0)[SUBLANES:SUBLANES + c, :] for s in range(1, n_back + 1)]


def _rwkv_chunk_units(units, ones_bd, tri, m0):
    c = CHUNK
    n2 = 2 * c
    ss = [_mm(u[4] * u[4], ones_bd) for u in units]
    log_w = [_mm_exact_lhs(tri, u[3]) for u in units]
    bon = [_mm(u[0] * u[1] * u[6], ones_bd) for u in units]

    def bd(x):
        return jnp.concatenate([jnp.where(m0, x, 0.0), jnp.where(m0, 0.0, x)], axis=0).astype(BF16)

    lhs, rhs, bk, v_bd, w_last = [], [], [], [], []
    for (r, k, v, lw, kraw, eta, _, _), s2, lg in zip(units, ss, log_w):
        kk = kraw * lax.rsqrt(s2 + RWKV_L2_EPS)
        beta = kk * eta
        lw_last = lg[c - 1:c, :]
        w_inv = jnp.exp(-lg)
        w_end = jnp.exp(lw_last - lg)
        lhs.append(jnp.concatenate([bd(-kk * jnp.exp(lg - lw)), bd(r * jnp.exp(lg))], axis=0))
        rhs.append(jnp.concatenate([bd(beta * w_inv), bd(k * w_inv)], axis=0))
        bk.append(jnp.concatenate([bd(beta * w_end), bd(k * w_end)], axis=0))
        v_bd.append(bd(v))
        w_last.append(jnp.exp(lw_last))

    sc = [_mm_nt(a, b) for a, b in zip(lhs, rhs)]
    xr = [_mm_nt(a, u[7]) for a, u in zip(lhs, units)]
    i, j = _iota2((n2, n2))
    strict = i > j
    incl = i >= j
    a_ab = [jnp.where(strict, x[:n2, :n2], 0.0) for x in sc]
    av = [_mm(jnp.where(strict, x[:n2, n2:], 0.0), vb) for x, vb in zip(sc, v_bd)]
    a_r = [jnp.where(jnp.concatenate([incl, incl], axis=1), x[n2:, :], 0.0).astype(BF16) for x in sc]
    t_inv = _unit_lower_inverse(a_ab)
    u_mat = [_mm(t, x[:n2] + a) for t, x, a in zip(t_inv, xr, av)]
    uv = [jnp.concatenate([x.astype(BF16), vb], axis=0) for x, vb in zip(u_mat, v_bd)]
    y_bd = [x[n2:] + _mm(a, z) for x, a, z in zip(xr, a_r, uv)]
    s_new = [u[7] * wl + _mm_tn(z, b) for u, wl, z, b in zip(units, w_last, uv, bk)]
    y = [x[:c] + x[c:] for x in y_bd]
    return y, bon, s_new


def _rwkv_mix_kernel(has_vres, *refs):
    if has_vres:
        (rkv_ref, sm_ref, vf_ref, mu_rkv_ref, mu_sm_ref, w0_ref, wup_ref, a0_ref, aup_ref, gup_ref,
         kk_ref, ka_ref, rk_ref, lnw_ref, lnb_ref, vup_ref, v0_ref,
         o_ref, s_ref, tail_rkv, tail_sm) = refs
        vout_ref = None
    else:
        (rkv_ref, sm_ref, mu_rkv_ref, mu_sm_ref, w0_ref, wup_ref, a0_ref, aup_ref, gup_ref,
         kk_ref, ka_ref, rk_ref, lnw_ref, lnb_ref,
         o_ref, vout_ref, s_ref, tail_rkv, tail_sm) = refs
    bb, tt = rkv_ref.shape[0], rkv_ref.shape[1]
    n_pairs = RWKV_DIM // LANES

    @pl.when(pl.program_id(1) == 0)
    def _():
        s_ref[...] = jnp.zeros(s_ref.shape, F32)
        tail_rkv[...] = jnp.zeros(tail_rkv.shape, F32)
        tail_sm[...] = jnp.zeros(tail_sm.shape, F32)

    li, lj = _iota2((LANES, LANES))
    ones_bd = ((li // RWKV_HEAD) == (lj // RWKV_HEAD)).astype(BF16)
    ti, tj = _iota2((CHUNK, CHUNK))
    tri = (ti >= tj).astype(BF16)
    m0 = lax.broadcasted_iota(jnp.int32, (CHUNK, LANES), 1) < RWKV_HEAD
    inv_n = 1.0 / RWKV_HEAD
    ids = [(bi, slice(p * LANES, (p + 1) * LANES)) for bi in range(bb) for p in range(n_pairs)]

    def body(ci, carry):
        rows = pl.ds(pl.multiple_of(ci * CHUNK, CHUNK), CHUNK)
        sh, shs = [], []
        for bi in range(bb):
            p = rkv_ref[bi, rows, :]
            (prev,) = _rows_back(p, tail_rkv[bi], 1)
            tail_rkv[bi] = p[CHUNK - SUBLANES:, :]
            sh.append(p + (prev - p) * mu_rkv_ref[...])
            ps = sm_ref[bi, rows, :]
            (prev_s,) = _rows_back(ps, tail_sm[bi], 1)
            tail_sm[bi] = ps[CHUNK - SUBLANES:, :]
            shs.append(ps + (prev_s - ps) * mu_sm_ref[...])
        w_lo = [_mm(jnp.tanh(x[:, SM_WD:SM_WD + 128]), wup_ref[...]) for x in shs]
        a_lo = [_mm(x[:, SM_AD:SM_AD + 128], aup_ref[...]) for x in shs]
        g_all = [_mm(_sigmoid(x[:, SM_GD:SM_GD + 256]), gup_ref[...]) for x in shs]
        if has_vres:
            v_lo = [_mm(x[:, SM_VD:SM_VD + 128], vup_ref[...]) for x in shs]
        r_all, k_all, v_all, lw_all, kr_all, eta_all = [], [], [], [], [], []
        for bi in range(bb):
            w = -_softplus(-(w0_ref[...] + w_lo[bi])) - 0.5
            lw_all.append(-jnp.exp(w))
            eta = _sigmoid(a0_ref[...] + a_lo[bi])
            k = sh[bi][:, RWKV_DIM:2 * RWKV_DIM]
            v = sh[bi][:, 2 * RWKV_DIM:3 * RWKV_DIM]
            if has_vres:
                v = v + (vf_ref[bi, rows, :] - v) * _sigmoid(v0_ref[...] + v_lo[bi])
            else:
                vout_ref[bi, rows, :] = v
            r_all.append(sh[bi][:, 0:RWKV_DIM])
            v_all.append(v)
            kr_all.append(k * kk_ref[...])
            k_all.append(k * (1.0 + (eta - 1.0) * ka_ref[...]))
            eta_all.append(eta)
        units = [tuple(a[bi][:, cs] for a in (r_all, k_all, v_all, lw_all, kr_all, eta_all))
                 + (rk_ref[:, cs], s_ref[n]) for n, (bi, cs) in enumerate(ids)]
        y, bon, s_new = _rwkv_chunk_units(units, ones_bd, tri, m0)
        mean = [_mm(x, ones_bd) * inv_n for x in y]
        d = [x - mu for x, mu in zip(y, mean)]
        var = [_mm(x * x, ones_bd) * inv_n for x in d]
        for n, (bi, cs) in enumerate(ids):
            yn = d[n] * lax.rsqrt(var[n] + RWKV_GN_EPS) * lnw_ref[:, cs] + lnb_ref[:, cs]
            o_ref[bi, rows, cs] = ((yn + bon[n] * units[n][2]) * g_all[bi][:, cs]).astype(BF16)
            s_ref[n] = s_new[n]
        return carry

    lax.fori_loop(0, tt // CHUNK, body, 0)


def _rwkv_mix(proj, v_first, lp, batch, seq, tt, bb):
    nt = seq // tt
    has_vres = v_first is not None
    full = lambda b, t: (0, 0)
    wide = pl.BlockSpec((bb, tt, RWKV_DIM), lambda b, t: (b, t, 0))
    vec = pl.BlockSpec((1, RWKV_DIM), full)
    lora = pl.BlockSpec((128, RWKV_DIM), full)
    in_specs = [pl.BlockSpec((bb, tt, 3 * RWKV_DIM), lambda b, t: (b, t, COL_RKV // (3 * RWKV_DIM))),
                pl.BlockSpec((bb, tt, SM_W), lambda b, t: (b, t, COL_SM // SM_W))]
    args = [proj, proj]
    if has_vres:
        in_specs.append(wide)
        args.append(v_first)
    in_specs += [pl.BlockSpec((1, 3 * RWKV_DIM), full), pl.BlockSpec((1, SM_W), full),
                 vec, lora, vec, lora, pl.BlockSpec((256, RWKV_DIM), full), vec, vec, vec, vec, vec]
    args += [lp["mu_rkv"], lp["mu_sm"], lp["w0"], lp["w_up"], lp["a0"], lp["a_up"], lp["g_up"],
             lp["k_k"], lp["k_a"], lp["r_k"], lp["ln_w"], lp["ln_b"]]
    y_shape = jax.ShapeDtypeStruct((batch, seq, RWKV_DIM), BF16)
    if has_vres:
        in_specs += [lora, vec]
        args += [lp["v_up"], lp["v0"]]
        out_specs, out_shape = wide, y_shape
    else:
        out_specs = [wide, wide]
        out_shape = [y_shape, jax.ShapeDtypeStruct((batch, seq, RWKV_DIM), F32)]
    return pl.pallas_call(
        functools.partial(_rwkv_mix_kernel, has_vres),
        grid=(batch // bb, nt),
        in_specs=in_specs,
        out_specs=out_specs,
        out_shape=out_shape,
        scratch_shapes=[pltpu.VMEM((bb * RWKV_DIM // LANES, LANES, LANES), F32),
                        pltpu.VMEM((bb, SUBLANES, 3 * RWKV_DIM), F32),
                        pltpu.VMEM((bb, SUBLANES, SM_W), F32)],
        compiler_params=pltpu.CompilerParams(
            dimension_semantics=("parallel", "arbitrary"), vmem_limit_bytes=VMEM_LIMIT),
        name="rwkv_mix",
    )(*args)


def _gdn_chunk_prep(x, tail, ab, cw, a_log, dt_bias, tri, e_g, e_b):
    x1, x2, x3 = _rows_back(x, tail, GDN_CONV - 1)
    y = _silu(x * cw[3:4, :] + x1 * cw[2:3, :] + x2 * cw[1:2, :] + x3 * cw[0:1, :])
    q, k = [], []
    for h in range(GDN_QK_HEADS):
        qh = y[:, h * GDN_HEAD:(h + 1) * GDN_HEAD]
        kh = y[:, GDN_QK_DIM + h * GDN_HEAD:GDN_QK_DIM + (h + 1) * GDN_HEAD]
        q.append(qh * (lax.rsqrt(jnp.sum(qh * qh, axis=-1, keepdims=True) + GDN_L2_EPS)
                       * (GDN_HEAD ** -0.5)))
        k.append(kh * lax.rsqrt(jnp.sum(kh * kh, axis=-1, keepdims=True) + GDN_L2_EPS))
    g = -jnp.exp(a_log) * _softplus(ab + dt_bias)
    beta = _sigmoid(ab)
    g_cum = _mm_exact_lhs(tri, g)
    bc = lambda a, j: jnp.broadcast_to(a[:, j:j + 1], (a.shape[0], GDN_HEAD))
    g_b = jnp.concatenate([bc(g_cum, j) for j in range(GDN_V_HEADS)], axis=1)
    beta_b = jnp.concatenate([bc(beta, GDN_V_HEADS + j) for j in range(GDN_V_HEADS)], axis=1)
    return q, k, y[:, 2 * GDN_QK_DIM:], g_b, beta_b


def _gdn_chunk_units(units):
    c = CHUNK
    n2 = 2 * c
    st = lambda x: jnp.concatenate([x[:, :GDN_HEAD], x[:, GDN_HEAD:]], axis=0)
    i, j = _iota2((n2, n2))
    incl = ((i // c) == (j // c)) & (i >= j)
    strict = i > j
    top = lax.broadcasted_iota(jnp.int32, (n2, GDN_HEAD), 0) < c
    pick = lambda x: jnp.where(top, x[:, :GDN_HEAD], x[:, GDN_HEAD:])

    g_c = [st(u[3]) for u in units]
    b_c = [st(u[4]) for u in units]
    k2 = [jnp.concatenate([u[1], u[1]], axis=0) for u in units]
    q2 = [jnp.concatenate([u[0], u[0]], axis=0) for u in units]
    kb = [a * b for a, b in zip(k2, b_c)]
    kq = [_mm_nt(jnp.concatenate([a, b], axis=0), kk) for a, b, kk in zip(kb, q2, k2)]
    gamma = [jnp.exp(jnp.where(incl, g - g.T, -jnp.inf)) for g in g_c]
    l_neg = [jnp.where(strict, -(x[:n2] * gm), 0.0) for x, gm in zip(kq, gamma)]
    a_qk = [(x[n2:] * gm).astype(BF16) for x, gm in zip(kq, gamma)]
    t_inv = _unit_lower_inverse(l_neg)
    e_g = [jnp.exp(g) for g in g_c]
    uw = [_mm(t, jnp.concatenate([st(u[2]) * b, kbi * eg], axis=1))
          for t, u, b, kbi, eg in zip(t_inv, units, b_c, kb, e_g)]
    wq = [_mm(jnp.concatenate([x[:, GDN_HEAD:], qq * eg], axis=0), u[5])
          for x, qq, eg, u in zip(uw, q2, e_g, units)]
    v_new = [x[:, :GDN_HEAD] - pick(w[:n2]) for x, w in zip(uw, wq)]
    o_s = [pick(w[n2:]) + _mm(a, vn) for w, a, vn in zip(wq, a_qk, v_new)]
    outs, s_new = [], []
    for g, kk, vn, o, u in zip(g_c, k2, v_new, o_s, units):
        gl0 = g[c - 1:c, :]
        gl1 = g[n2 - 1:n2, :]
        k_g = kk * jnp.exp(jnp.where(top, gl0, gl1) - g)
        v_bd = jnp.concatenate([jnp.where(top, vn, 0.0), jnp.where(top, 0.0, vn)], axis=1)
        decay = jnp.concatenate([jnp.exp(gl0), jnp.exp(gl1)], axis=1)
        s_new.append(u[5] * decay + _mm_tn(k_g, v_bd))
        outs.append(jnp.concatenate([o[:c], o[c:]], axis=1))
    return outs, s_new


def _gdn_mix_kernel(qk_ref, v_ref, z_ref, sm_ref, cw_ref, alog_ref, dtb_ref, nw_ref,
                    o_ref, s_ref, tail_ref):
    bb, tt = qk_ref.shape[0], qk_ref.shape[1]
    pw = 2 * GDN_HEAD

    @pl.when(pl.program_id(1) == 0)
    def _():
        s_ref[...] = jnp.zeros(s_ref.shape, F32)
        tail_ref[...] = jnp.zeros(tail_ref.shape, F32)

    ids = [(bi, h) for bi in range(bb) for h in range(GDN_QK_HEADS)]
    ti, tj = _iota2((CHUNK, CHUNK))
    tri = (ti >= tj).astype(BF16)
    ei, ej = _iota2((LANES, GDN_V_DIM))
    e_g = (ei == ej // GDN_HEAD).astype(BF16)
    e_b = (ei == ej // GDN_HEAD + GDN_V_HEADS).astype(BF16)

    def body(ci, carry):
        rows = pl.ds(pl.multiple_of(ci * CHUNK, CHUNK), CHUNK)
        rowp = []
        for bi in range(bb):
            x = jnp.concatenate([qk_ref[bi, rows, :], v_ref[bi, rows, :]], axis=1)
            rowp.append(_gdn_chunk_prep(x, tail_ref[bi], sm_ref[bi, rows, SM_AB:SM_AB + LANES],
                                        cw_ref[...], alog_ref[...], dtb_ref[...], tri, e_g, e_b))
            tail_ref[bi] = x[CHUNK - SUBLANES:, :]
        units = []
        for n, (bi, h) in enumerate(ids):
            q, k, v, g_b, beta_b = rowp[bi]
            cv = slice(h * pw, (h + 1) * pw)
            units.append((q[h], k[h], v[:, cv], g_b[:, cv], beta_b[:, cv], s_ref[n]))
        outs, s_new = _gdn_chunk_units(units)
        for n, (bi, h) in enumerate(ids):
            s_ref[n] = s_new[n]
            for e in range(2):
                cols = slice(h * pw + e * GDN_HEAD, h * pw + (e + 1) * GDN_HEAD)
                oh = outs[n][:, e * GDN_HEAD:(e + 1) * GDN_HEAD]
                oh = oh * lax.rsqrt(jnp.mean(oh * oh, axis=-1, keepdims=True) + GDN_NORM_EPS) * nw_ref[...]
                o_ref[bi, rows, cols] = (oh * _silu(z_ref[bi, rows, cols])).astype(BF16)
        return carry

    lax.fori_loop(0, tt // CHUNK, body, 0)


def _gdn_mix(proj, lp, batch, seq, tt, bb):
    nt = seq // tt
    full = lambda b, t: (0, 0)
    qk_w = 2 * GDN_QK_DIM
    wide = lambda col: pl.BlockSpec((bb, tt, GDN_V_DIM), lambda b, t: (b, t, col // GDN_V_DIM))
    return pl.pallas_call(
        _gdn_mix_kernel,
        grid=(batch // bb, nt),
        in_specs=[wide(COL_GQK), wide(COL_GV), wide(COL_GZ),
                  pl.BlockSpec((bb, tt, SM_W), lambda b, t: (b, t, COL_SM // SM_W)),
                  pl.BlockSpec((GDN_CONV, qk_w + GDN_V_DIM), full),
                  pl.BlockSpec((1, LANES), full),
                  pl.BlockSpec((1, LANES), full),
                  pl.BlockSpec((1, GDN_HEAD), full)],
        out_specs=wide(0),
        out_shape=jax.ShapeDtypeStruct((batch, seq, GDN_V_DIM), BF16),
        scratch_shapes=[pltpu.VMEM((bb * GDN_QK_HEADS, GDN_HEAD, 2 * GDN_HEAD), F32),
                        pltpu.VMEM((bb, SUBLANES, qk_w + GDN_V_DIM), F32)],
        compiler_params=pltpu.CompilerParams(
            dimension_semantics=("parallel", "arbitrary"), vmem_limit_bytes=VMEM_LIMIT),
        name="gdn_mix",
    )(proj, proj, proj, proj, lp["conv_w"], lp["a_log"], lp["dt_bias"], lp["norm_w"])


def _pad_cols(a, n):
    return jnp.pad(a, [(0, 0)] * (a.ndim - 1) + [(0, n - a.shape[-1])])


def _pad_rows(a, n):
    return jnp.pad(a, [(0, 0)] * (a.ndim - 2) + [(0, n - a.shape[-2]), (0, 0)])


def _proj_layout(main, vd):
    o = 0
    parts = {}
    for name, n in (("r", RWKV_DIM), ("k", RWKV_DIM), ("v", RWKV_DIM), ("wd", RWKV_DECAY_LORA),
                    ("ad", RWKV_AAA_LORA), ("gd", RWKV_GATE_LORA), ("gq", GDN_QK_DIM), ("gk", GDN_QK_DIM),
                    ("gv", GDN_V_DIM), ("gz", GDN_V_DIM), ("ga", GDN_V_HEADS), ("gb", GDN_V_HEADS)):
        parts[name] = main[..., o:o + n]
        o += n
    return jnp.concatenate(
        [parts["r"], parts["k"], parts["v"], parts["gq"], parts["gk"], parts["gv"], parts["gz"],
         _pad_cols(parts["wd"], 128), _pad_cols(parts["ad"], 128), _pad_cols(parts["gd"], 256),
         _pad_cols(vd, 128), _pad_cols(jnp.concatenate([parts["ga"], parts["gb"]], axis=-1), 128)],
        axis=-1)


def kernel(x, attn_norm_w, w_in, rwkv_mu, rwkv_w0, rwkv_w_up, rwkv_a0, rwkv_a_up, rwkv_g_up,
           rwkv_k_k, rwkv_k_a, rwkv_r_k, rwkv_ln_w, rwkv_ln_b, vres_down, vres_mu, vres_up, vres_v0,
           gdn_conv_w, gdn_A_log, gdn_dt_bias, gdn_norm_w, w_out, ffn_norm_w, ffn_w_gate, ffn_w_up,
           ffn_w_down, final_norm_w):
    batch, seq, d = x.shape
    depth = w_in.shape[0]
    m = batch * seq

    gdn_pad = jnp.zeros((depth, w_in.shape[2] - rwkv_mu.shape[1]), F32)
    vmu = jnp.concatenate([jnp.zeros((1, RWKV_MV_LORA), F32), vres_mu], axis=0)
    mu_all = _proj_layout(jnp.concatenate([rwkv_mu, gdn_pad], axis=1), vmu)
    w_up = _pad_rows(rwkv_w_up, 128).astype(BF16)
    a_up = _pad_rows(rwkv_a_up, 128).astype(BF16)
    g_up = _pad_rows(rwkv_g_up, 256).astype(BF16)
    v_up = _pad_rows(vres_up, 128).astype(BF16)
    a_log = _pad_cols(gdn_A_log, LANES)
    dt_bias = _pad_cols(gdn_dt_bias, LANES)
    r_k = rwkv_r_k.reshape(depth, RWKV_DIM)

    tm = min(1024, m)
    row = lambda a, l: a[l][None, :]

    xf = x.reshape(m, d)
    v_first = None
    for l in range(depth):
        vd_w = (vres_down[l - 1].astype(BF16) if l > 0 else jnp.zeros((d, RWKV_MV_LORA), BF16))
        w_proj = _proj_layout(w_in[l].astype(BF16), vd_w)
        proj = _norm_matmul(xf, row(attn_norm_w, l), w_proj, tm, 1152)
        lp = dict(mu_rkv=mu_all[l][None, COL_RKV:COL_RKV + 3 * RWKV_DIM],
                  mu_sm=mu_all[l][None, COL_SM:COL_SM + SM_W],
                  w0=row(rwkv_w0, l), w_up=w_up[l], a0=row(rwkv_a0, l), a_up=a_up[l], g_up=g_up[l],
                  k_k=row(rwkv_k_k, l), k_a=row(rwkv_k_a, l), r_k=row(r_k, l),
                  ln_w=row(rwkv_ln_w, l), ln_b=row(rwkv_ln_b, l),
                  conv_w=gdn_conv_w[l], a_log=row(a_log, l), dt_bias=row(dt_bias, l),
                  norm_w=row(gdn_norm_w, l))
        if l > 0:
            lp["v_up"] = v_up[l - 1]
            lp["v0"] = row(vres_v0, l - 1)
        as3 = lambda a: a.reshape(batch, seq, a.shape[-1])
        bb_a = 2 if batch % 2 == 0 else 1
        if l == 0:
            y_a, v_first = _rwkv_mix(as3(proj), None, lp, batch, seq, min(128, seq), bb_a)
        else:
            y_a = _rwkv_mix(as3(proj), v_first, lp, batch, seq, min(128, seq), bb_a)
        y_a = y_a.reshape(m, RWKV_DIM)
        bb_b = 4 if batch % 4 == 0 else bb_a
        y_b = _gdn_mix(as3(proj), lp, batch, seq, min(128, seq), bb_b).reshape(m, GDN_V_DIM)
        xf = _matmul2_res(y_a, y_b, w_out[l].astype(BF16), xf, tm, 1024)
        hmid = _norm_swiglu(xf, row(ffn_norm_w, l), ffn_w_gate[l].astype(BF16),
                            ffn_w_up[l].astype(BF16), tm, 512)
        xf = _matmul_res(hmid, ffn_w_down[l].astype(BF16), xf, tm, 512)
    out = _rmsnorm(xf, final_norm_w[None, :], min(512, m))
    return out.reshape(batch, seq, d)
```

```python
import functools

import jax
import jax.numpy as jnp
from jax import lax
from jax.experimental import pallas as pl
from jax.experimental.pallas import tpu as pltpu

F32 = jnp.float32
BF16 = jnp.bfloat16

D_MODEL = 2048
RWKV_HEAD = 64
RWKV_DIM = 1024
RWKV_DECAY_LORA = 64
RWKV_AAA_LORA = 64
RWKV_MV_LORA = 32
RWKV_GATE_LORA = 160
RWKV_GN_EPS = RWKV_HEAD * 1e-5
RWKV_L2_EPS = 1e-12
GDN_HEAD = 128
GDN_V_DIM = 1024
GDN_V_HEADS = 8
GDN_QK_HEADS = 4
GDN_QK_DIM = 512
GDN_CONV = 4
GDN_L2_EPS = 1e-6
GDN_NORM_EPS = 1e-6
D_FF = 5632
NORM_EPS = 1e-5

LANES = 128
SUBLANES = 8
CHUNK = 64
PAIR = 2 * CHUNK
VMEM_LIMIT = 56 * 1024 * 1024

COL_RKV = 0
COL_GQK = 3 * RWKV_DIM
COL_GV = COL_GQK + 2 * GDN_QK_DIM
COL_GZ = COL_GV + GDN_V_DIM
COL_SM = COL_GZ + GDN_V_DIM
SM_WD, SM_AD, SM_GD, SM_VD, SM_AB = 0, 128, 256, 512, 640
SM_W = 768
N_PROJ = COL_SM + SM_W


def _mm(a, b):
    return jnp.dot(a.astype(BF16), b.astype(BF16), preferred_element_type=F32)


def _mm_nt(a, b):
    return lax.dot_general(a.astype(BF16), b.astype(BF16), (((1,), (1,)), ((), ())),
                           preferred_element_type=F32)


def _mm_tn(a, b):
    return lax.dot_general(a.astype(BF16), b.astype(BF16), (((0,), (0,)), ((), ())),
                           preferred_element_type=F32)


def _split3(x):
    hi = x.astype(BF16)
    r1 = x - hi.astype(F32)
    mid = r1.astype(BF16)
    lo = (r1 - mid.astype(F32)).astype(BF16)
    return hi, mid, lo


def _mm_exact_lhs(a01, x):
    n = x.shape[1]
    hi, mid, lo = _split3(x)
    y = jnp.dot(a01, jnp.concatenate([hi, mid, lo], axis=1), preferred_element_type=F32)
    return y[:, :n] + y[:, n:2 * n] + y[:, 2 * n:]


def _mm_exact_rhs(x, b01):
    m = x.shape[0]
    hi, mid, lo = _split3(x)
    y = jnp.dot(jnp.concatenate([hi, mid, lo], axis=0), b01, preferred_element_type=F32)
    return y[:m] + y[m:2 * m] + y[2 * m:]


def _sigmoid(x):
    return 1.0 / (1.0 + jnp.exp(-x))


def _softplus(x):
    return jnp.maximum(x, 0.0) + jnp.log(1.0 + jnp.exp(-jnp.abs(x)))


def _silu(x):
    return x * _sigmoid(x)


def _iota2(shape):
    return (lax.broadcasted_iota(jnp.int32, shape, 0), lax.broadcasted_iota(jnp.int32, shape, 1))


def _unit_lower_inverse(n_mats):
    sz = n_mats[0].shape[0]
    i, j = _iota2((sz, sz))
    eye = (i == j).astype(F32)
    blk8 = (i // 8) == (j // 8)
    d1 = [jnp.where(blk8, n, 0.0) for n in n_mats]
    d2 = [_mm(x, x) for x in d1]
    d4 = [_mm(x, x) for x in d2]
    p = [eye + x for x in d1]
    p = [x + _mm(x, y) for x, y in zip(p, d2)]
    p = [x + _mm(x, y) for x, y in zip(p, d4)]
    for s in (8, 16, 32):
        off = ((i // (2 * s)) == (j // (2 * s))) & ((i // s) == (j // s) + 1)
        q = [_mm(jnp.where(off, n, 0.0), x) for n, x in zip(n_mats, p)]
        p = [x + _mm(x, y) for x, y in zip(p, q)]
    return p


def _norm_matmul_kernel(x_ref, nw_ref, w_ref, o_ref, h_ref):
    @pl.when(pl.program_id(1) == 0)
    def _():
        x = x_ref[...]
        ms = jnp.mean(x * x, axis=-1, keepdims=True)
        h_ref[...] = (x * lax.rsqrt(ms + NORM_EPS) * nw_ref[...]).astype(BF16)

    o_ref[...] = jnp.dot(h_ref[...], w_ref[...], preferred_element_type=F32)


def _norm_matmul(x, nw, w, tm, tn):
    m, d = x.shape
    n = w.shape[1]
    return pl.pallas_call(
        _norm_matmul_kernel,
        grid=(m // tm, n // tn),
        in_specs=[pl.BlockSpec((tm, d), lambda i, j: (i, 0)),
                  pl.BlockSpec((1, d), lambda i, j: (0, 0)),
                  pl.BlockSpec((d, tn), lambda i, j: (0, j))],
        out_specs=pl.BlockSpec((tm, tn), lambda i, j: (i, j)),
        out_shape=jax.ShapeDtypeStruct((m, n), F32),
        scratch_shapes=[pltpu.VMEM((tm, d), BF16)],
        compiler_params=pltpu.CompilerParams(
            dimension_semantics=("parallel", "arbitrary"), vmem_limit_bytes=VMEM_LIMIT),
        name="norm_in_proj",
    )(x, nw, w)


def _norm_swiglu_kernel(x_ref, nw_ref, wg_ref, wu_ref, o_ref, h_ref):
    @pl.when(pl.program_id(1) == 0)
    def _():
        x = x_ref[...]
        ms = jnp.mean(x * x, axis=-1, keepdims=True)
        h_ref[...] = (x * lax.rsqrt(ms + NORM_EPS) * nw_ref[...]).astype(BF16)

    h = h_ref[...]
    g = jnp.dot(h, wg_ref[...], preferred_element_type=F32)
    u = jnp.dot(h, wu_ref[...], preferred_element_type=F32)
    o_ref[...] = (_silu(g) * u).astype(BF16)


def _norm_swiglu(x, nw, wg, wu, tm, tn):
    m, d = x.shape
    n = wg.shape[1]
    return pl.pallas_call(
        _norm_swiglu_kernel,
        grid=(m // tm, n // tn),
        in_specs=[pl.BlockSpec((tm, d), lambda i, j: (i, 0)),
                  pl.BlockSpec((1, d), lambda i, j: (0, 0)),
                  pl.BlockSpec((d, tn), lambda i, j: (0, j)),
                  pl.BlockSpec((d, tn), lambda i, j: (0, j))],
        out_specs=pl.BlockSpec((tm, tn), lambda i, j: (i, j)),
        out_shape=jax.ShapeDtypeStruct((m, n), BF16),
        scratch_shapes=[pltpu.VMEM((tm, d), BF16)],
        compiler_params=pltpu.CompilerParams(
            dimension_semantics=("parallel", "arbitrary"), vmem_limit_bytes=VMEM_LIMIT),
        name="norm_swiglu",
    )(x, nw, wg, wu)


def _matmul_res_kernel(a_ref, w_ref, r_ref, o_ref):
    o_ref[...] = r_ref[...] + jnp.dot(a_ref[...], w_ref[...], preferred_element_type=F32)


def _matmul_res(a, w, res, tm, tn):
    m, k = a.shape
    n = w.shape[1]
    return pl.pallas_call(
        _matmul_res_kernel,
        grid=(n // tn, m // tm),
        in_specs=[pl.BlockSpec((tm, k), lambda j, i: (i, 0)),
                  pl.BlockSpec((k, tn), lambda j, i: (0, j)),
                  pl.BlockSpec((tm, tn), lambda j, i: (i, j))],
        out_specs=pl.BlockSpec((tm, tn), lambda j, i: (i, j)),
        out_shape=jax.ShapeDtypeStruct((m, n), F32),
        compiler_params=pltpu.CompilerParams(
            dimension_semantics=("parallel", "arbitrary"), vmem_limit_bytes=VMEM_LIMIT),
        name="matmul_residual",
    )(a, w, res)


def _matmul2_res_kernel(a_ref, b_ref, wa_ref, wb_ref, r_ref, o_ref):
    acc = jnp.dot(a_ref[...], wa_ref[...], preferred_element_type=F32)
    acc = acc + jnp.dot(b_ref[...], wb_ref[...], preferred_element_type=F32)
    o_ref[...] = r_ref[...] + acc


def _matmul2_res(a, b, w, res, tm, tn):
    m, k = a.shape
    n = w.shape[1]
    return pl.pallas_call(
        _matmul2_res_kernel,
        grid=(n // tn, m // tm),
        in_specs=[pl.BlockSpec((tm, k), lambda j, i: (i, 0)),
                  pl.BlockSpec((tm, k), lambda j, i: (i, 0)),
                  pl.BlockSpec((k, tn), lambda j, i: (0, j)),
                  pl.BlockSpec((k, tn), lambda j, i: (1, j)),
                  pl.BlockSpec((tm, tn), lambda j, i: (i, j))],
        out_specs=pl.BlockSpec((tm, tn), lambda j, i: (i, j)),
        out_shape=jax.ShapeDtypeStruct((m, n), F32),
        compiler_params=pltpu.CompilerParams(
            dimension_semantics=("parallel", "arbitrary"), vmem_limit_bytes=VMEM_LIMIT),
        name="out_proj_residual",
    )(a, b, w, w, res)


def _rmsnorm_kernel(x_ref, nw_ref, o_ref):
    x = x_ref[...]
    ms = jnp.mean(x * x, axis=-1, keepdims=True)
    o_ref[...] = x * lax.rsqrt(ms + NORM_EPS) * nw_ref[...]


def _rmsnorm(x, nw, tm):
    m, d = x.shape
    return pl.pallas_call(
        _rmsnorm_kernel,
        grid=(m // tm,),
        in_specs=[pl.BlockSpec((tm, d), lambda i: (i, 0)),
                  pl.BlockSpec((1, d), lambda i: (0, 0))],
        out_specs=pl.BlockSpec((tm, d), lambda i: (i, 0)),
        out_shape=jax.ShapeDtypeStruct((m, d), F32),
        compiler_params=pltpu.CompilerParams(dimension_semantics=("parallel",)),
        name="final_rmsnorm",
    )(x, nw)


def _rows_back(x, tail, n_back):
    c = x.shape[0]
    ext = jnp.concatenate([tail, x], axis=0)
    return [pltpu.roll(ext, s, 0)[SUBLANES:SUBLANES + c, :] for s in range(1, n_back + 1)]


def _rwkv_chunk_units(units, ones_bd, tri, m0):
    c = CHUNK
    n2 = 2 * c
    ss = [_mm(u[4] * u[4], ones_bd) for u in units]
    log_w = [_mm_exact_lhs(tri, u[3]) for u in units]
    bon = [_mm(u[0] * u[1] * u[6], ones_bd) for u in units]

    def bd(x):
        return jnp.concatenate([jnp.where(m0, x, 0.0), jnp.where(m0, 0.0, x)], axis=0).astype(BF16)

    lhs, rhs, bk, v_bd, w_last = [], [], [], [], []
    for (r, k, v, lw, kraw, eta, _, _), s2, lg in zip(units, ss, log_w):
        kk = kraw * lax.rsqrt(s2 + RWKV_L2_EPS)
        beta = kk * eta
        lw_last = lg[c - 1:c, :]
        w_inv = jnp.exp(-lg)
        w_end = jnp.exp(lw_last - lg)
        lhs.append(jnp.concatenate([bd(-kk * jnp.exp(lg - lw)), bd(r * jnp.exp(lg))], axis=0))
        rhs.append(jnp.concatenate([bd(beta * w_inv), bd(k * w_inv)], axis=0))
        bk.append(jnp.concatenate([bd(beta * w_end), bd(k * w_end)], axis=0))
        v_bd.append(bd(v))
        w_last.append(jnp.exp(lw_last))

    sc = [_mm_nt(a, b) for a, b in zip(lhs, rhs)]
    xr = [_mm_nt(a, u[7]) for a, u in zip(lhs, units)]
    i, j = _iota2((n2, n2))
    strict = i > j
    incl = i >= j
    a_ab = [jnp.where(strict, x[:n2, :n2], 0.0) for x in sc]
    av = [_mm(jnp.where(strict, x[:n2, n2:], 0.0), vb) for x, vb in zip(sc, v_bd)]
    a_r = [jnp.where(jnp.concatenate([incl, incl], axis=1), x[n2:, :], 0.0).astype(BF16) for x in sc]
    t_inv = _unit_lower_inverse(a_ab)
    u_mat = [_mm(t, x[:n2] + a) for t, x, a in zip(t_inv, xr, av)]
    uv = [jnp.concatenate([x.astype(BF16), vb], axis=0) for x, vb in zip(u_mat, v_bd)]
    y_bd = [x[n2:] + _mm(a, z) for x, a, z in zip(xr, a_r, uv)]
    s_new = [u[7] * wl + _mm_tn(z, b) for u, wl, z, b in zip(units, w_last, uv, bk)]
    y = [x[:c] + x[c:] for x in y_bd]
    return y, bon, s_new


def _rwkv_mix_kernel(has_vres, *refs):
    if has_vres:
        (rkv_ref, sm_ref, vf_ref, mu_rkv_ref, mu_sm_ref, w0_ref, wup_ref, a0_ref, aup_ref, gup_ref,
         kk_ref, ka_ref, rk_ref, lnw_ref, lnb_ref, vup_ref, v0_ref,
         o_ref, s_ref, tail_rkv, tail_sm) = refs
        vout_ref = None
    else:
        (rkv_ref, sm_ref, mu_rkv_ref, mu_sm_ref, w0_ref, wup_ref, a0_ref, aup_ref, gup_ref,
         kk_ref, ka_ref, rk_ref, lnw_ref, lnb_ref,
         o_ref, vout_ref, s_ref, tail_rkv, tail_sm) = refs
    bb, tt = rkv_ref.shape[0], rkv_ref.shape[1]
    n_pairs = RWKV_DIM // LANES

    @pl.when(pl.program_id(1) == 0)
    def _():
        s_ref[...] = jnp.zeros(s_ref.shape, F32)
        tail_rkv[...] = jnp.zeros(tail_rkv.shape, F32)
        tail_sm[...] = jnp.zeros(tail_sm.shape, F32)

    li, lj = _iota2((LANES, LANES))
    ones_bd = ((li // RWKV_HEAD) == (lj // RWKV_HEAD)).astype(BF16)
    ti, tj = _iota2((CHUNK, CHUNK))
    tri = (ti >= tj).astype(BF16)
    m0 = lax.broadcasted_iota(jnp.int32, (CHUNK, LANES), 1) < RWKV_HEAD
    inv_n = 1.0 / RWKV_HEAD
    ids = [(bi, slice(p * LANES, (p + 1) * LANES)) for bi in range(bb) for p in range(n_pairs)]

    def body(ci, carry):
        rows = pl.ds(pl.multiple_of(ci * CHUNK, CHUNK), CHUNK)
        sh, shs = [], []
        for bi in range(bb):
            p = rkv_ref[bi, rows, :]
            (prev,) = _rows_back(p, tail_rkv[bi], 1)
            tail_rkv[bi] = p[CHUNK - SUBLANES:, :]
            sh.append(p + (prev - p) * mu_rkv_ref[...])
            ps = sm_ref[bi, rows, :]
            (prev_s,) = _rows_back(ps, tail_sm[bi], 1)
            tail_sm[bi] = ps[CHUNK - SUBLANES:, :]
            shs.append(ps + (prev_s - ps) * mu_sm_ref[...])
        w_lo = [_mm(jnp.tanh(x[:, SM_WD:SM_WD + 128]), wup_ref[...]) for x in shs]
        a_lo = [_mm(x[:, SM_AD:SM_AD + 128], aup_ref[...]) for x in shs]
        g_all = [_mm(_sigmoid(x[:, SM_GD:SM_GD + 256]), gup_ref[...]) for x in shs]
        if has_vres:
            v_lo = [_mm(x[:, SM_VD:SM_VD + 128], vup_ref[...]) for x in shs]
        r_all, k_all, v_all, lw_all, kr_all, eta_all = [], [], [], [], [], []
        for bi in range(bb):
            w = -_softplus(-(w0_ref[...] + w_lo[bi])) - 0.5
            lw_all.append(-jnp.exp(w))
            eta = _sigmoid(a0_ref[...] + a_lo[bi])
            k = sh[bi][:, RWKV_DIM:2 * RWKV_DIM]
            v = sh[bi][:, 2 * RWKV_DIM:3 * RWKV_DIM]
            if has_vres:
                v = v + (vf_ref[bi, rows, :] - v) * _sigmoid(v0_ref[...] + v_lo[bi])
            else:
                vout_ref[bi, rows, :] = v
            r_all.append(sh[bi][:, 0:RWKV_DIM])
            v_all.append(v)
            kr_all.append(k * kk_ref[...])
            k_all.append(k * (1.0 + (eta - 1.0) * ka_ref[...]))
            eta_all.append(eta)
        units = [tuple(a[bi][:, cs] for a in (r_all, k_all, v_all, lw_all, kr_all, eta_all))
                 + (rk_ref[:, cs], s_ref[n]) for n, (bi, cs) in enumerate(ids)]
        y, bon, s_new = _rwkv_chunk_units(units, ones_bd, tri, m0)
        mean = [_mm(x, ones_bd) * inv_n for x in y]
        d = [x - mu for x, mu in zip(y, mean)]
        var = [_mm(x * x, ones_bd) * inv_n for x in d]
        for n, (bi, cs) in enumerate(ids):
            yn = d[n] * lax.rsqrt(var[n] + RWKV_GN_EPS) * lnw_ref[:, cs] + lnb_ref[:, cs]
            o_ref[bi, rows, cs] = ((yn + bon[n] * units[n][2]) * g_all[bi][:, cs]).astype(BF16)
            s_ref[n] = s_new[n]
        return carry

    lax.fori_loop(0, tt // CHUNK, body, 0)


def _rwkv_mix(proj, v_first, lp, batch, seq, tt, bb):
    nt = seq // tt
    has_vres = v_first is not None
    full = lambda b, t: (0, 0)
    wide = pl.BlockSpec((bb, tt, RWKV_DIM), lambda b, t: (b, t, 0))
    vec = pl.BlockSpec((1, RWKV_DIM), full)
    lora = pl.BlockSpec((128, RWKV_DIM), full)
    in_specs = [pl.BlockSpec((bb, tt, 3 * RWKV_DIM), lambda b, t: (b, t, COL_RKV // (3 * RWKV_DIM))),
                pl.BlockSpec((bb, tt, SM_W), lambda b, t: (b, t, COL_SM // SM_W))]
    args = [proj, proj]
    if has_vres:
        in_specs.append(wide)
        args.append(v_first)
    in_specs += [pl.BlockSpec((1, 3 * RWKV_DIM), full), pl.BlockSpec((1, SM_W), full),
                 vec, lora, vec, lora, pl.BlockSpec((256, RWKV_DIM), full), vec, vec, vec, vec, vec]
    args += [lp["mu_rkv"], lp["mu_sm"], lp["w0"], lp["w_up"], lp["a0"], lp["a_up"], lp["g_up"],
             lp["k_k"], lp["k_a"], lp["r_k"], lp["ln_w"], lp["ln_b"]]
    y_shape = jax.ShapeDtypeStruct((batch, seq, RWKV_DIM), BF16)
    if has_vres:
        in_specs += [lora, vec]
        args += [lp["v_up"], lp["v0"]]
        out_specs, out_shape = wide, y_shape
    else:
        out_specs = [wide, wide]
        out_shape = [y_shape, jax.ShapeDtypeStruct((batch, seq, RWKV_DIM), F32)]
    return pl.pallas_call(
        functools.partial(_rwkv_mix_kernel, has_vres),
        grid=(batch // bb, nt),
        in_specs=in_specs,
        out_specs=out_specs,
        out_shape=out_shape,
        scratch_shapes=[pltpu.VMEM((bb * RWKV_DIM // LANES, LANES, LANES), F32),
                        pltpu.VMEM((bb, SUBLANES, 3 * RWKV_DIM), F32),
                        pltpu.VMEM((bb, SUBLANES, SM_W), F32)],
        compiler_params=pltpu.CompilerParams(
            dimension_semantics=("parallel", "arbitrary"), vmem_limit_bytes=VMEM_LIMIT),
        name="rwkv_mix",
    )(*args)


def _gdn_chunk_prep(x, tail, ab, cw, a_log, dt_bias, tri, e_g, e_b):
    x1, x2, x3 = _rows_back(x, tail, GDN_CONV - 1)
    y = _silu(x * cw[3:4, :] + x1 * cw[2:3, :] + x2 * cw[1:2, :] + x3 * cw[0:1, :])
    q, k = [], []
    for h in range(GDN_QK_HEADS):
        qh = y[:, h * GDN_HEAD:(h + 1) * GDN_HEAD]
        kh = y[:, GDN_QK_DIM + h * GDN_HEAD:GDN_QK_DIM + (h + 1) * GDN_HEAD]
        q.append(qh * (lax.rsqrt(jnp.sum(qh * qh, axis=-1, keepdims=True) + GDN_L2_EPS)
                       * (GDN_HEAD ** -0.5)))
        k.append(kh * lax.rsqrt(jnp.sum(kh * kh, axis=-1, keepdims=True) + GDN_L2_EPS))
    g = -jnp.exp(a_log) * _softplus(ab + dt_bias)
    beta = _sigmoid(ab)
    g_cum = _mm_exact_lhs(tri, g)
    bc = lambda a, j: jnp.broadcast_to(a[:, j:j + 1], (a.shape[0], GDN_HEAD))
    g_b = jnp.concatenate([bc(g_cum, j) for j in range(GDN_V_HEADS)], axis=1)
    beta_b = jnp.concatenate([bc(beta, GDN_V_HEADS + j) for j in range(GDN_V_HEADS)], axis=1)
    return q, k, y[:, 2 * GDN_QK_DIM:], g_b, beta_b


def _gdn_chunk_units(units):
    c = CHUNK
    n2 = 2 * c
    st = lambda x: jnp.concatenate([x[:, :GDN_HEAD], x[:, GDN_HEAD:]], axis=0)
    i, j = _iota2((n2, n2))
    incl = ((i // c) == (j // c)) & (i >= j)
    strict = i > j
    top = lax.broadcasted_iota(jnp.int32, (n2, GDN_HEAD), 0) < c
    pick = lambda x: jnp.where(top, x[:, :GDN_HEAD], x[:, GDN_HEAD:])

    g_c = [st(u[3]) for u in units]
    b_c = [st(u[4]) for u in units]
    k2 = [jnp.concatenate([u[1], u[1]], axis=0) for u in units]
    q2 = [jnp.concatenate([u[0], u[0]], axis=0) for u in units]
    kb = [a * b for a, b in zip(k2, b_c)]
    kq = [_mm_nt(jnp.concatenate([a, b], axis=0), kk) for a, b, kk in zip(kb, q2, k2)]
    gamma = [jnp.exp(jnp.where(incl, g - g.T, -jnp.inf)) for g in g_c]
    l_neg = [jnp.where(strict, -(x[:n2] * gm), 0.0) for x, gm in zip(kq, gamma)]
    a_qk = [(x[n2:] * gm).astype(BF16) for x, gm in zip(kq, gamma)]
    t_inv = _unit_lower_inverse(l_neg)
    e_g = [jnp.exp(g) for g in g_c]
    uw = [_mm(t, jnp.concatenate([st(u[2]) * b, kbi * eg], axis=1))
          for t, u, b, kbi, eg in zip(t_inv, units, b_c, kb, e_g)]
    wq = [_mm(jnp.concatenate([x[:, GDN_HEAD:], qq * eg], axis=0), u[5])
          for x, qq, eg, u in zip(uw, q2, e_g, units)]
    v_new = [x[:, :GDN_HEAD] - pick(w[:n2]) for x, w in zip(uw, wq)]
    o_s = [pick(w[n2:]) + _mm(a, vn) for w, a, vn in zip(wq, a_qk, v_new)]
    outs, s_new = [], []
    for g, kk, vn, o, u in zip(g_c, k2, v_new, o_s, units):
        gl0 = g[c - 1:c, :]
        gl1 = g[n2 - 1:n2, :]
        k_g = kk * jnp.exp(jnp.where(top, gl0, gl1) - g)
        v_bd = jnp.concatenate([jnp.where(top, vn, 0.0), jnp.where(top, 0.0, vn)], axis=1)
        decay = jnp.concatenate([jnp.exp(gl0), jnp.exp(gl1)], axis=1)
        s_new.append(u[5] * decay + _mm_tn(k_g, v_bd))
        outs.append(jnp.concatenate([o[:c], o[c:]], axis=1))
    return outs, s_new


def _gdn_mix_kernel(qk_ref, v_ref, z_ref, sm_ref, cw_ref, alog_ref, dtb_ref, nw_ref,
                    o_ref, s_ref, tail_ref):
    bb, tt = qk_ref.shape[0], qk_ref.shape[1]
    pw = 2 * GDN_HEAD

    @pl.when(pl.program_id(1) == 0)
    def _():
        s_ref[...] = jnp.zeros(s_ref.shape, F32)
        tail_ref[...] = jnp.zeros(tail_ref.shape, F32)

    ids = [(bi, h) for bi in range(bb) for h in range(GDN_QK_HEADS)]
    ti, tj = _iota2((CHUNK, CHUNK))
    tri = (ti >= tj).astype(BF16)
    ei, ej = _iota2((LANES, GDN_V_DIM))
    e_g = (ei == ej // GDN_HEAD).astype(BF16)
    e_b = (ei == ej // GDN_HEAD + GDN_V_HEADS).astype(BF16)

    def body(ci, carry):
        rows = pl.ds(pl.multiple_of(ci * CHUNK, CHUNK), CHUNK)
        rowp = []
        for bi in range(bb):
            x = jnp.concatenate([qk_ref[bi, rows, :], v_ref[bi, rows, :]], axis=1)
            rowp.append(_gdn_chunk_prep(x, tail_ref[bi], sm_ref[bi, rows, SM_AB:SM_AB + LANES],
                                        cw_ref[...], alog_ref[...], dtb_ref[...], tri, e_g, e_b))
            tail_ref[bi] = x[CHUNK - SUBLANES:, :]
        units = []
        for n, (bi, h) in enumerate(ids):
            q, k, v, g_b, beta_b = rowp[bi]
            cv = slice(h * pw, (h + 1) * pw)
            units.append((q[h], k[h], v[:, cv], g_b[:, cv], beta_b[:, cv], s_ref[n]))
        outs, s_new = _gdn_chunk_units(units)
        for n, (bi, h) in enumerate(ids):
            s_ref[n] = s_new[n]
            for e in range(2):
                cols = slice(h * pw + e * GDN_HEAD, h * pw + (e + 1) * GDN_HEAD)
                oh = outs[n][:, e * GDN_HEAD:(e + 1) * GDN_HEAD]
                oh = oh * lax.rsqrt(jnp.mean(oh * oh, axis=-1, keepdims=True) + GDN_NORM_EPS) * nw_ref[...]
                o_ref[bi, rows, cols] = (oh * _silu(z_ref[bi, rows, cols])).astype(BF16)
        return carry

    lax.fori_loop(0, tt // CHUNK, body, 0)


def _gdn_mix(proj, lp, batch, seq, tt, bb):
    nt = seq // tt
    full = lambda b, t: (0, 0)
    qk_w = 2 * GDN_QK_DIM
    wide = lambda col: pl.BlockSpec((bb, tt, GDN_V_DIM), lambda b, t: (b, t, col // GDN_V_DIM))
    return pl.pallas_call(
        _gdn_mix_kernel,
        grid=(batch // bb, nt),
        in_specs=[wide(COL_GQK), wide(COL_GV), wide(COL_GZ),
                  pl.BlockSpec((bb, tt, SM_W), lambda b, t: (b, t, COL_SM // SM_W)),
                  pl.BlockSpec((GDN_CONV, qk_w + GDN_V_DIM), full),
                  pl.BlockSpec((1, LANES), full),
                  pl.BlockSpec((1, LANES), full),
                  pl.BlockSpec((1, GDN_HEAD), full)],
        out_specs=wide(0),
        out_shape=jax.ShapeDtypeStruct((batch, seq, GDN_V_DIM), BF16),
        scratch_shapes=[pltpu.VMEM((bb * GDN_QK_HEADS, GDN_HEAD, 2 * GDN_HEAD), F32),
                        pltpu.VMEM((bb, SUBLANES, qk_w + GDN_V_DIM), F32)],
        compiler_params=pltpu.CompilerParams(
            dimension_semantics=("parallel", "arbitrary"), vmem_limit_bytes=VMEM_LIMIT),
        name="gdn_mix",
    )(proj, proj, proj, proj, lp["conv_w"], lp["a_log"], lp["dt_bias"], lp["norm_w"])


def _pad_cols(a, n):
    return jnp.pad(a, [(0, 0)] * (a.ndim - 1) + [(0, n - a.shape[-1])])


def _pad_rows(a, n):
    return jnp.pad(a, [(0, 0)] * (a.ndim - 2) + [(0, n - a.shape[-2]), (0, 0)])


def _proj_layout(main, vd):
    o = 0
    parts = {}
    for name, n in (("r", RWKV_DIM), ("k", RWKV_DIM), ("v", RWKV_DIM), ("wd", RWKV_DECAY_LORA),
                    ("ad", RWKV_AAA_LORA), ("gd", RWKV_GATE_LORA), ("gq", GDN_QK_DIM), ("gk", GDN_QK_DIM),
                    ("gv", GDN_V_DIM), ("gz", GDN_V_DIM), ("ga", GDN_V_HEADS), ("gb", GDN_V_HEADS)):
        parts[name] = main[..., o:o + n]
        o += n
    return jnp.concatenate(
        [parts["r"], parts["k"], parts["v"], parts["gq"], parts["gk"], parts["gv"], parts["gz"],
         _pad_cols(parts["wd"], 128), _pad_cols(parts["ad"], 128), _pad_cols(parts["gd"], 256),
         _pad_cols(vd, 128), _pad_cols(jnp.concatenate([parts["ga"], parts["gb"]], axis=-1), 128)],
        axis=-1)


def kernel(x, attn_norm_w, w_in, rwkv_mu, rwkv_w0, rwkv_w_up, rwkv_a0, rwkv_a_up, rwkv_g_up,
           rwkv_k_k, rwkv_k_a, rwkv_r_k, rwkv_ln_w, rwkv_ln_b, vres_down, vres_mu, vres_up, vres_v0,
           gdn_conv_w, gdn_A_log, gdn_dt_bias, gdn_norm_w, w_out, ffn_norm_w, ffn_w_gate, ffn_w_up,
           ffn_w_down, final_norm_w):
    batch, seq, d = x.shape
    depth = w_in.shape[0]
    m = batch * seq

    gdn_pad = jnp.zeros((depth, w_in.shape[2] - rwkv_mu.shape[1]), F32)
    vmu = jnp.concatenate([jnp.zeros((1, RWKV_MV_LORA), F32), vres_mu], axis=0)
    mu_all = _proj_layout(jnp.concatenate([rwkv_mu, gdn_pad], axis=1), vmu)
    w_up = _pad_rows(rwkv_w_up, 128).astype(BF16)
    a_up = _pad_rows(rwkv_a_up, 128).astype(BF16)
    g_up = _pad_rows(rwkv_g_up, 256).astype(BF16)
    v_up = _pad_rows(vres_up, 128).astype(BF16)
    a_log = _pad_cols(gdn_A_log, LANES)
    dt_bias = _pad_cols(gdn_dt_bias, LANES)
    r_k = rwkv_r_k.reshape(depth, RWKV_DIM)

    tm = min(1024, m)
    row = lambda a, l: a[l][None, :]

    xf = x.reshape(m, d)
    v_first = None
    for l in range(depth):
        vd_w = (vres_down[l - 1].astype(BF16) if l > 0 else jnp.zeros((d, RWKV_MV_LORA), BF16))
        w_proj = _proj_layout(w_in[l].astype(BF16), vd_w)
        proj = _norm_matmul(xf, row(attn_norm_w, l), w_proj, tm, 1152)
        lp = dict(mu_rkv=mu_all[l][None, COL_RKV:COL_RKV + 3 * RWKV_DIM],
                  mu_sm=mu_all[l][None, COL_SM:COL_SM + SM_W],
                  w0=row(rwkv_w0, l), w_up=w_up[l], a0=row(rwkv_a0, l), a_up=a_up[l], g_up=g_up[l],
                  k_k=row(rwkv_k_k, l), k_a=row(rwkv_k_a, l), r_k=row(r_k, l),
                  ln_w=row(rwkv_ln_w, l), ln_b=row(rwkv_ln_b, l),
                  conv_w=gdn_conv_w[l], a_log=row(a_log, l), dt_bias=row(dt_bias, l),
                  norm_w=row(gdn_norm_w, l))
        if l > 0:
            lp["v_up"] = v_up[l - 1]
            lp["v0"] = row(vres_v0, l - 1)
        as3 = lambda a: a.reshape(batch, seq, a.shape[-1])
        bb_a = 2 if batch % 2 == 0 else 1
        if l == 0:
            y_a, v_first = _rwkv_mix(as3(proj), None, lp, batch, seq, min(128, seq), bb_a)
        else:
            y_a = _rwkv_mix(as3(proj), v_first, lp, batch, seq, min(128, seq), bb_a)
        y_a = y_a.reshape(m, RWKV_DIM)
        bb_b = 4 if batch % 4 == 0 else bb_a
        y_b = _gdn_mix(as3(proj), lp, batch, seq, min(128, seq), bb_b).reshape(m, GDN_V_DIM)
        xf = _matmul2_res(y_a, y_b, w_out[l].astype(BF16), xf, tm, 1024)
        hmid = _norm_swiglu(xf, row(ffn_norm_w, l), ffn_w_gate[l].astype(BF16),
                            ffn_w_up[l].astype(BF16), tm, 512)
        xf = _matmul_res(hmid, ffn_w_down[l].astype(BF16), xf, min(512, m), 1024)
    out = _rmsnorm(xf, final_norm_w[None, :], min(512, m))
    return out.reshape(batch, seq, d)
```

```python
import functools

import jax
import jax.numpy as jnp
from jax import lax
from jax.experimental import pallas as pl
from jax.experimental.pallas import tpu as pltpu

F32 = jnp.float32
BF16 = jnp.bfloat16

D_MODEL = 2048
RWKV_HEAD = 64
RWKV_DIM = 1024
RWKV_DECAY_LORA = 64
RWKV_AAA_LORA = 64
RWKV_MV_LORA = 32
RWKV_GATE_LORA = 160
RWKV_GN_EPS = RWKV_HEAD * 1e-5
RWKV_L2_EPS = 1e-12
GDN_HEAD = 128
GDN_V_DIM = 1024
GDN_V_HEADS = 8
GDN_QK_HEADS = 4
GDN_QK_DIM = 512
GDN_CONV = 4
GDN_L2_EPS = 1e-6
GDN_NORM_EPS = 1e-6
D_FF = 5632
NORM_EPS = 1e-5

LANES = 128
SUBLANES = 8
CHUNK = 64
PAIR = 2 * CHUNK
VMEM_LIMIT = 56 * 1024 * 1024

COL_RKV = 0
COL_GQK = 3 * RWKV_DIM
COL_GV = COL_GQK + 2 * GDN_QK_DIM
COL_GZ = COL_GV + GDN_V_DIM
COL_AB = COL_GZ + GDN_V_DIM
COL_WA = COL_AB + 128
COL_GD = COL_WA + 128
N_PROJ = COL_GD + 256
AB_VD = 16
SM_WA, SM_GD, SM_AB = 0, 128, 384
SM_W = 512


def _mm(a, b):
    return jnp.dot(a.astype(BF16), b.astype(BF16), preferred_element_type=F32)


def _mm_nt(a, b):
    return lax.dot_general(a.astype(BF16), b.astype(BF16), (((1,), (1,)), ((), ())),
                           preferred_element_type=F32)


def _mm_tn(a, b):
    return lax.dot_general(a.astype(BF16), b.astype(BF16), (((0,), (0,)), ((), ())),
                           preferred_element_type=F32)


def _split3(x):
    hi = x.astype(BF16)
    r1 = x - hi.astype(F32)
    mid = r1.astype(BF16)
    lo = (r1 - mid.astype(F32)).astype(BF16)
    return hi, mid, lo


def _mm_exact_lhs(a01, x):
    n = x.shape[1]
    hi, mid, lo = _split3(x)
    y = jnp.dot(a01, jnp.concatenate([hi, mid, lo], axis=1), preferred_element_type=F32)
    return y[:, :n] + y[:, n:2 * n] + y[:, 2 * n:]


def _mm_exact_rhs(x, b01):
    m = x.shape[0]
    hi, mid, lo = _split3(x)
    y = jnp.dot(jnp.concatenate([hi, mid, lo], axis=0), b01, preferred_element_type=F32)
    return y[:m] + y[m:2 * m] + y[2 * m:]


def _sigmoid(x):
    return 1.0 / (1.0 + jnp.exp(-x))


def _softplus(x):
    return jnp.maximum(x, 0.0) + jnp.log(1.0 + jnp.exp(-jnp.abs(x)))


def _silu(x):
    return x * _sigmoid(x)


def _iota2(shape):
    return (lax.broadcasted_iota(jnp.int32, shape, 0), lax.broadcasted_iota(jnp.int32, shape, 1))


def _unit_lower_inverse(n_mats):
    sz = n_mats[0].shape[0]
    i, j = _iota2((sz, sz))
    eye = (i == j).astype(F32)
    blk8 = (i // 8) == (j // 8)
    d1 = [jnp.where(blk8, n, 0.0) for n in n_mats]
    d2 = [_mm(x, x) for x in d1]
    d4 = [_mm(x, x) for x in d2]
    p = [eye + x for x in d1]
    p = [x + _mm(x, y) for x, y in zip(p, d2)]
    p = [x + _mm(x, y) for x, y in zip(p, d4)]
    for s in (8, 16, 32):
        off = ((i // (2 * s)) == (j // (2 * s))) & ((i // s) == (j // s) + 1)
        q = [_mm(jnp.where(off, n, 0.0), x) for n, x in zip(n_mats, p)]
        p = [x + _mm(x, y) for x, y in zip(p, q)]
    return p


def _norm_matmul_kernel(x_ref, nw_ref, w_ref, o_ref, h_ref):
    @pl.when(pl.program_id(1) == 0)
    def _():
        x = x_ref[...]
        ms = jnp.mean(x * x, axis=-1, keepdims=True)
        h_ref[...] = (x * lax.rsqrt(ms + NORM_EPS) * nw_ref[...]).astype(BF16)

    o_ref[...] = jnp.dot(h_ref[...], w_ref[...], preferred_element_type=F32)


def _norm_matmul(x, nw, w, tm, tn):
    m, d = x.shape
    n = w.shape[1]
    return pl.pallas_call(
        _norm_matmul_kernel,
        grid=(m // tm, n // tn),
        in_specs=[pl.BlockSpec((tm, d), lambda i, j: (i, 0)),
                  pl.BlockSpec((1, d), lambda i, j: (0, 0)),
                  pl.BlockSpec((d, tn), lambda i, j: (0, j))],
        out_specs=pl.BlockSpec((tm, tn), lambda i, j: (i, j)),
        out_shape=jax.ShapeDtypeStruct((m, n), F32),
        scratch_shapes=[pltpu.VMEM((tm, d), BF16)],
        compiler_params=pltpu.CompilerParams(
            dimension_semantics=("parallel", "arbitrary"), vmem_limit_bytes=VMEM_LIMIT),
        name="norm_in_proj",
    )(x, nw, w)


def _norm_swiglu_kernel(x_ref, nw_ref, wg_ref, wu_ref, o_ref, h_ref):
    @pl.when(pl.program_id(1) == 0)
    def _():
        x = x_ref[...]
        ms = jnp.mean(x * x, axis=-1, keepdims=True)
        h_ref[...] = (x * lax.rsqrt(ms + NORM_EPS) * nw_ref[...]).astype(BF16)

    h = h_ref[...]
    g = jnp.dot(h, wg_ref[...], preferred_element_type=F32)
    u = jnp.dot(h, wu_ref[...], preferred_element_type=F32)
    o_ref[...] = (_silu(g) * u).astype(BF16)


def _norm_swiglu(x, nw, wg, wu, tm, tn):
    m, d = x.shape
    n = wg.shape[1]
    return pl.pallas_call(
        _norm_swiglu_kernel,
        grid=(m // tm, n // tn),
        in_specs=[pl.BlockSpec((tm, d), lambda i, j: (i, 0)),
                  pl.BlockSpec((1, d), lambda i, j: (0, 0)),
                  pl.BlockSpec((d, tn), lambda i, j: (0, j)),
                  pl.BlockSpec((d, tn), lambda i, j: (0, j))],
        out_specs=pl.BlockSpec((tm, tn), lambda i, j: (i, j)),
        out_shape=jax.ShapeDtypeStruct((m, n), BF16),
        scratch_shapes=[pltpu.VMEM((tm, d), BF16)],
        compiler_params=pltpu.CompilerParams(
            dimension_semantics=("parallel", "arbitrary"), vmem_limit_bytes=VMEM_LIMIT),
        name="norm_swiglu",
    )(x, nw, wg, wu)


def _matmul_res_kernel(a_ref, w_ref, r_ref, o_ref):
    o_ref[...] = r_ref[...] + jnp.dot(a_ref[...], w_ref[...], preferred_element_type=F32)


def _matmul_res(a, w, res, tm, tn):
    m, k = a.shape
    n = w.shape[1]
    return pl.pallas_call(
        _matmul_res_kernel,
        grid=(n // tn, m // tm),
        in_specs=[pl.BlockSpec((tm, k), lambda j, i: (i, 0)),
                  pl.BlockSpec((k, tn), lambda j, i: (0, j)),
                  pl.BlockSpec((tm, tn), lambda j, i: (i, j))],
        out_specs=pl.BlockSpec((tm, tn), lambda j, i: (i, j)),
        out_shape=jax.ShapeDtypeStruct((m, n), F32),
        compiler_params=pltpu.CompilerParams(
            dimension_semantics=("parallel", "arbitrary"), vmem_limit_bytes=VMEM_LIMIT),
        name="matmul_residual",
    )(a, w, res)


def _matmul2_res_kernel(a_ref, b_ref, wa_ref, wb_ref, r_ref, o_ref):
    acc = jnp.dot(a_ref[...], wa_ref[...], preferred_element_type=F32)
    acc = acc + jnp.dot(b_ref[...], wb_ref[...], preferred_element_type=F32)
    o_ref[...] = r_ref[...] + acc


def _matmul2_res(a, b, w, res, tm, tn):
    m, k = a.shape
    n = w.shape[1]
    return pl.pallas_call(
        _matmul2_res_kernel,
        grid=(n // tn, m // tm),
        in_specs=[pl.BlockSpec((tm, k), lambda j, i: (i, 0)),
                  pl.BlockSpec((tm, k), lambda j, i: (i, 0)),
                  pl.BlockSpec((k, tn), lambda j, i: (0, j)),
                  pl.BlockSpec((k, tn), lambda j, i: (1, j)),
                  pl.BlockSpec((tm, tn), lambda j, i: (i, j))],
        out_specs=pl.BlockSpec((tm, tn), lambda j, i: (i, j)),
        out_shape=jax.ShapeDtypeStruct((m, n), F32),
        compiler_params=pltpu.CompilerParams(
            dimension_semantics=("parallel", "arbitrary"), vmem_limit_bytes=VMEM_LIMIT),
        name="out_proj_residual",
    )(a, b, w, w, res)


def _rmsnorm_kernel(x_ref, nw_ref, o_ref):
    x = x_ref[...]
    ms = jnp.mean(x * x, axis=-1, keepdims=True)
    o_ref[...] = x * lax.rsqrt(ms + NORM_EPS) * nw_ref[...]


def _rmsnorm(x, nw, tm):
    m, d = x.shape
    return pl.pallas_call(
        _rmsnorm_kernel,
        grid=(m // tm,),
        in_specs=[pl.BlockSpec((tm, d), lambda i: (i, 0)),
                  pl.BlockSpec((1, d), lambda i: (0, 0))],
        out_specs=pl.BlockSpec((tm, d), lambda i: (i, 0)),
        out_shape=jax.ShapeDtypeStruct((m, d), F32),
        compiler_params=pltpu.CompilerParams(dimension_semantics=("parallel",)),
        name="final_rmsnorm",
    )(x, nw)


def _rows_back(x, tail, n_back):
    c = x.shape[0]
    ext = jnp.concatenate([tail, x], axis=0)
    return [pltpu.roll(ext, s, 0)[SUBLANES:SUBLANES + c, :] for s in range(1, n_back + 1)]


def _rwkv_chunk_units(units, ones_bd, tri, m0):
    c = CHUNK
    n2 = 2 * c
    ss = [_mm(u[4] * u[4], ones_bd) for u in units]
    log_w = [_mm_exact_lhs(tri, u[3]) for u in units]
    bon = [_mm(u[0] * u[1] * u[6], ones_bd) for u in units]

    def bd(x):
        return jnp.concatenate([jnp.where(m0, x, 0.0), jnp.where(m0, 0.0, x)], axis=0).astype(BF16)

    lhs, rhs, bk, v_bd, w_last = [], [], [], [], []
    for (r, k, v, lw, kraw, eta, _, _), s2, lg in zip(units, ss, log_w):
        kk = kraw * lax.rsqrt(s2 + RWKV_L2_EPS)
        beta = kk * eta
        lw_last = lg[c - 1:c, :]
        w_inv = jnp.exp(-lg)
        w_end = jnp.exp(lw_last - lg)
        lhs.append(jnp.concatenate([bd(-kk * jnp.exp(lg - lw)), bd(r * jnp.exp(lg))], axis=0))
        rhs.append(jnp.concatenate([bd(beta * w_inv), bd(k * w_inv)], axis=0))
        bk.append(jnp.concatenate([bd(beta * w_end), bd(k * w_end)], axis=0))
        v_bd.append(bd(v))
        w_last.append(jnp.exp(lw_last))

    sc = [_mm_nt(a, b) for a, b in zip(lhs, rhs)]
    xr = [_mm_nt(a, u[7]) for a, u in zip(lhs, units)]
    i, j = _iota2((n2, n2))
    strict = i > j
    incl = i >= j
    a_ab = [jnp.where(strict, x[:n2, :n2], 0.0) for x in sc]
    av = [_mm(jnp.where(strict, x[:n2, n2:], 0.0), vb) for x, vb in zip(sc, v_bd)]
    a_r = [jnp.where(jnp.concatenate([incl, incl], axis=1), x[n2:, :], 0.0).astype(BF16) for x in sc]
    t_inv = _unit_lower_inverse(a_ab)
    u_mat = [_mm(t, x[:n2] + a) for t, x, a in zip(t_inv, xr, av)]
    uv = [jnp.concatenate([x.astype(BF16), vb], axis=0) for x, vb in zip(u_mat, v_bd)]
    y_bd = [x[n2:] + _mm(a, z) for x, a, z in zip(xr, a_r, uv)]
    s_new = [u[7] * wl + _mm_tn(z, b) for u, wl, z, b in zip(units, w_last, uv, bk)]
    y = [x[:c] + x[c:] for x in y_bd]
    return y, bon, s_new


def _rwkv_mix_kernel(has_vres, *refs):
    if has_vres:
        (rkv_ref, wa_ref, gd_ref, ab_ref, vf_ref, mu_rkv_ref, mu_sm_ref, w0_ref, wup_ref, a0_ref,
         aup_ref, gup_ref, kk_ref, ka_ref, rk_ref, lnw_ref, lnb_ref, vup_ref, v0_ref,
         o_ref, s_ref, tail_rkv, tail_sm) = refs
        vout_ref = None
    else:
        (rkv_ref, wa_ref, gd_ref, ab_ref, mu_rkv_ref, mu_sm_ref, w0_ref, wup_ref, a0_ref,
         aup_ref, gup_ref, kk_ref, ka_ref, rk_ref, lnw_ref, lnb_ref,
         o_ref, vout_ref, s_ref, tail_rkv, tail_sm) = refs
    bb, tt = rkv_ref.shape[0], rkv_ref.shape[1]
    n_pairs = RWKV_DIM // LANES

    @pl.when(pl.program_id(1) == 0)
    def _():
        s_ref[...] = jnp.zeros(s_ref.shape, F32)
        tail_rkv[...] = jnp.zeros(tail_rkv.shape, F32)
        tail_sm[...] = jnp.zeros(tail_sm.shape, F32)

    li, lj = _iota2((LANES, LANES))
    ones_bd = ((li // RWKV_HEAD) == (lj // RWKV_HEAD)).astype(BF16)
    ti, tj = _iota2((CHUNK, CHUNK))
    tri = (ti >= tj).astype(BF16)
    m0 = lax.broadcasted_iota(jnp.int32, (CHUNK, LANES), 1) < RWKV_HEAD
    inv_n = 1.0 / RWKV_HEAD
    ids = [(bi, slice(p * LANES, (p + 1) * LANES)) for bi in range(bb) for p in range(n_pairs)]

    def body(ci, carry):
        rows = pl.ds(pl.multiple_of(ci * CHUNK, CHUNK), CHUNK)
        sh, shs = [], []
        for bi in range(bb):
            p = rkv_ref[bi, rows, :]
            (prev,) = _rows_back(p, tail_rkv[bi], 1)
            tail_rkv[bi] = p[CHUNK - SUBLANES:, :]
            sh.append(p + (prev - p) * mu_rkv_ref[...])
            ps = jnp.concatenate([wa_ref[bi, rows, :], gd_ref[bi, rows, :], ab_ref[bi, rows, :]], axis=1)
            (prev_s,) = _rows_back(ps, tail_sm[bi], 1)
            tail_sm[bi] = ps[CHUNK - SUBLANES:, :]
            shs.append(ps + (prev_s - ps) * mu_sm_ref[...])
        w_lo = [_mm(jnp.tanh(x[:, SM_WA:SM_WA + 128]), wup_ref[...]) for x in shs]
        a_lo = [_mm(x[:, SM_WA:SM_WA + 128], aup_ref[...]) for x in shs]
        g_all = [_mm(_sigmoid(x[:, SM_GD:SM_GD + 256]), gup_ref[...]) for x in shs]
        if has_vres:
            v_lo = [_mm(x[:, SM_AB:SM_AB + 128], vup_ref[...]) for x in shs]
        r_all, k_all, v_all, lw_all, kr_all, eta_all = [], [], [], [], [], []
        for bi in range(bb):
            w = -_softplus(-(w0_ref[...] + w_lo[bi])) - 0.5
            lw_all.append(-jnp.exp(w))
            eta = _sigmoid(a0_ref[...] + a_lo[bi])
            k = sh[bi][:, RWKV_DIM:2 * RWKV_DIM]
            v = sh[bi][:, 2 * RWKV_DIM:3 * RWKV_DIM]
            if has_vres:
                v = v + (vf_ref[bi, rows, :] - v) * _sigmoid(v0_ref[...] + v_lo[bi])
            else:
                vout_ref[bi, rows, :] = v
            r_all.append(sh[bi][:, 0:RWKV_DIM])
            v_all.append(v)
            kr_all.append(k * kk_ref[...])
            k_all.append(k * (1.0 + (eta - 1.0) * ka_ref[...]))
            eta_all.append(eta)
        units = [tuple(a[bi][:, cs] for a in (r_all, k_all, v_all, lw_all, kr_all, eta_all))
                 + (rk_ref[:, cs], s_ref[n]) for n, (bi, cs) in enumerate(ids)]
        y, bon, s_new = _rwkv_chunk_units(units, ones_bd, tri, m0)
        mean = [_mm(x, ones_bd) * inv_n for x in y]
        d = [x - mu for x, mu in zip(y, mean)]
        var = [_mm(x * x, ones_bd) * inv_n for x in d]
        for n, (bi, cs) in enumerate(ids):
            yn = d[n] * lax.rsqrt(var[n] + RWKV_GN_EPS) * lnw_ref[:, cs] + lnb_ref[:, cs]
            o_ref[bi, rows, cs] = ((yn + bon[n] * units[n][2]) * g_all[bi][:, cs]).astype(BF16)
            s_ref[n] = s_new[n]
        return carry

    lax.fori_loop(0, tt // CHUNK, body, 0)


def _rwkv_mix(proj, v_first, lp, batch, seq, tt, bb):
    nt = seq // tt
    has_vres = v_first is not None
    full = lambda b, t: (0, 0)
    wide = pl.BlockSpec((bb, tt, RWKV_DIM), lambda b, t: (b, t, 0))
    vec = pl.BlockSpec((1, RWKV_DIM), full)
    lora = pl.BlockSpec((128, RWKV_DIM), full)
    col_blk = lambda col, w: pl.BlockSpec((bb, tt, w), lambda b, t: (b, t, col // w))
    in_specs = [col_blk(COL_RKV, 3 * RWKV_DIM), col_blk(COL_WA, 128), col_blk(COL_GD, 256),
                col_blk(COL_AB, 128)]
    args = [proj, proj, proj, proj]
    if has_vres:
        in_specs.append(wide)
        args.append(v_first)
    in_specs += [pl.BlockSpec((1, 3 * RWKV_DIM), full), pl.BlockSpec((1, SM_W), full),
                 vec, lora, vec, lora, pl.BlockSpec((256, RWKV_DIM), full), vec, vec, vec, vec, vec]
    args += [lp["mu_rkv"], lp["mu_sm"], lp["w0"], lp["w_up"], lp["a0"], lp["a_up"], lp["g_up"],
             lp["k_k"], lp["k_a"], lp["r_k"], lp["ln_w"], lp["ln_b"]]
    y_shape = jax.ShapeDtypeStruct((batch, seq, RWKV_DIM), BF16)
    if has_vres:
        in_specs += [lora, vec]
        args += [lp["v_up"], lp["v0"]]
        out_specs, out_shape = wide, y_shape
    else:
        out_specs = [wide, wide]
        out_shape = [y_shape, jax.ShapeDtypeStruct((batch, seq, RWKV_DIM), F32)]
    return pl.pallas_call(
        functools.partial(_rwkv_mix_kernel, has_vres),
        grid=(batch // bb, nt),
        in_specs=in_specs,
        out_specs=out_specs,
        out_shape=out_shape,
        scratch_shapes=[pltpu.VMEM((bb * RWKV_DIM // LANES, LANES, LANES), F32),
                        pltpu.VMEM((bb, SUBLANES, 3 * RWKV_DIM), F32),
                        pltpu.VMEM((bb, SUBLANES, SM_W), F32)],
        compiler_params=pltpu.CompilerParams(
            dimension_semantics=("parallel", "arbitrary"), vmem_limit_bytes=VMEM_LIMIT),
        name="rwkv_mix",
    )(*args)


def _gdn_chunk_prep(x, tail, ab, cw, a_log, dt_bias, tri):
    x1, x2, x3 = _rows_back(x, tail, GDN_CONV - 1)
    y = _silu(x * cw[3:4, :] + x1 * cw[2:3, :] + x2 * cw[1:2, :] + x3 * cw[0:1, :])
    q, k = [], []
    for h in range(GDN_QK_HEADS):
        qh = y[:, h * GDN_HEAD:(h + 1) * GDN_HEAD]
        kh = y[:, GDN_QK_DIM + h * GDN_HEAD:GDN_QK_DIM + (h + 1) * GDN_HEAD]
        q.append(qh * (lax.rsqrt(jnp.sum(qh * qh, axis=-1, keepdims=True) + GDN_L2_EPS)
                       * (GDN_HEAD ** -0.5)))
        k.append(kh * lax.rsqrt(jnp.sum(kh * kh, axis=-1, keepdims=True) + GDN_L2_EPS))
    g = -jnp.exp(a_log) * _softplus(ab + dt_bias)
    beta = _sigmoid(ab)
    g_cum = _mm_exact_lhs(tri, g)
    bc = lambda a, j: jnp.broadcast_to(a[:, j:j + 1], (a.shape[0], GDN_HEAD))
    g_b = jnp.concatenate([bc(g_cum, j) for j in range(GDN_V_HEADS)], axis=1)
    beta_b = jnp.concatenate([bc(beta, GDN_V_HEADS + j) for j in range(GDN_V_HEADS)], axis=1)
    return q, k, y[:, 2 * GDN_QK_DIM:], g_b, beta_b


def _gdn_chunk_units(units):
    c = CHUNK
    n2 = 2 * c
    st = lambda x: jnp.concatenate([x[:, :GDN_HEAD], x[:, GDN_HEAD:]], axis=0)
    i, j = _iota2((n2, n2))
    incl = ((i // c) == (j // c)) & (i >= j)
    strict = i > j

    g_c = [st(u[3]) for u in units]
    b_c = [st(u[4]) for u in units]
    k2 = [jnp.concatenate([u[1], u[1]], axis=0) for u in units]
    q2 = [jnp.concatenate([u[0], u[0]], axis=0) for u in units]
    kb = [a * b for a, b in zip(k2, b_c)]
    kq = [_mm_nt(jnp.concatenate([a, b], axis=0), kk) for a, b, kk in zip(kb, q2, k2)]
    gamma = [jnp.exp(jnp.where(incl, g - g.T, -jnp.inf)) for g in g_c]
    l_neg = [jnp.where(strict, -(x[:n2] * gm), 0.0) for x, gm in zip(kq, gamma)]
    a_qk = [(x[n2:] * gm).astype(BF16) for x, gm in zip(kq, gamma)]
    t_inv = _unit_lower_inverse(l_neg)
    e_g = [jnp.exp(g) for g in g_c]
    uw = [_mm(t, jnp.concatenate([st(u[2]) * b, kbi * eg], axis=1))
          for t, u, b, kbi, eg in zip(t_inv, units, b_c, kb, e_g)]
    qg = [qq * eg for qq, eg in zip(q2, e_g)]
    hd = lambda e: slice(e * GDN_HEAD, (e + 1) * GDN_HEAD)
    rw = lambda e: slice(e * c, (e + 1) * c)
    wq = [[_mm(jnp.concatenate([x[rw(e), GDN_HEAD:], qe[rw(e)]], axis=0), u[5][:, hd(e)])
           for x, qe, u in zip(uw, qg, units)] for e in range(2)]
    v_new = [jnp.concatenate([x[rw(0), :GDN_HEAD] - w0[:c], x[rw(1), :GDN_HEAD] - w1[:c]], axis=0)
             for x, w0, w1 in zip(uw, wq[0], wq[1])]
    o_s = [jnp.concatenate([w0[c:], w1[c:]], axis=0) + _mm(a, vn)
           for w0, w1, a, vn in zip(wq[0], wq[1], a_qk, v_new)]
    k_g, decay = [], []
    for g, u in zip(g_c, units):
        g_last = (g[c - 1:c, :], g[n2 - 1:n2, :])
        k_g.append([u[1] * jnp.exp(g_last[e] - g[rw(e)]) for e in range(2)])
        decay.append((jnp.exp(g_last[0]), jnp.exp(g_last[1])))
    ds = [[_mm_tn(kg[e], vn[rw(e)]) for kg, vn in zip(k_g, v_new)] for e in range(2)]
    s_new = [jnp.concatenate([u[5][:, hd(0)] * dc[0] + d0, u[5][:, hd(1)] * dc[1] + d1], axis=1)
             for u, dc, d0, d1 in zip(units, decay, ds[0], ds[1])]
    outs = [jnp.concatenate([o[:c], o[c:]], axis=1) for o in o_s]
    return outs, s_new


def _gdn_mix_kernel(qk_ref, v_ref, z_ref, ab_ref, cw_ref, alog_ref, dtb_ref, nw_ref,
                    o_ref, s_ref, tail_ref):
    bb, tt = qk_ref.shape[0], qk_ref.shape[1]
    pw = 2 * GDN_HEAD

    @pl.when(pl.program_id(1) == 0)
    def _():
        s_ref[...] = jnp.zeros(s_ref.shape, F32)
        tail_ref[...] = jnp.zeros(tail_ref.shape, F32)

    ids = [(bi, h) for bi in range(bb) for h in range(GDN_QK_HEADS)]
    ti, tj = _iota2((CHUNK, CHUNK))
    tri = (ti >= tj).astype(BF16)

    def body(ci, carry):
        rows = pl.ds(pl.multiple_of(ci * CHUNK, CHUNK), CHUNK)
        rowp = []
        for bi in range(bb):
            x = jnp.concatenate([qk_ref[bi, rows, :], v_ref[bi, rows, :]], axis=1)
            rowp.append(_gdn_chunk_prep(x, tail_ref[bi], ab_ref[bi, rows, :],
                                        cw_ref[...], alog_ref[...], dtb_ref[...], tri))
            tail_ref[bi] = x[CHUNK - SUBLANES:, :]
        units = []
        for n, (bi, h) in enumerate(ids):
            q, k, v, g_b, beta_b = rowp[bi]
            cv = slice(h * pw, (h + 1) * pw)
            units.append((q[h], k[h], v[:, cv], g_b[:, cv], beta_b[:, cv], s_ref[n]))
        outs, s_new = _gdn_chunk_units(units)
        for n, (bi, h) in enumerate(ids):
            s_ref[n] = s_new[n]
            for e in range(2):
                cols = slice(h * pw + e * GDN_HEAD, h * pw + (e + 1) * GDN_HEAD)
                oh = outs[n][:, e * GDN_HEAD:(e + 1) * GDN_HEAD]
                oh = oh * lax.rsqrt(jnp.mean(oh * oh, axis=-1, keepdims=True) + GDN_NORM_EPS) * nw_ref[...]
                o_ref[bi, rows, cols] = (oh * _silu(z_ref[bi, rows, cols])).astype(BF16)
        return carry

    lax.fori_loop(0, tt // CHUNK, body, 0)


def _gdn_mix(proj, lp, batch, seq, tt, bb):
    nt = seq // tt
    full = lambda b, t: (0, 0)
    qk_w = 2 * GDN_QK_DIM
    wide = lambda col: pl.BlockSpec((bb, tt, GDN_V_DIM), lambda b, t: (b, t, col // GDN_V_DIM))
    return pl.pallas_call(
        _gdn_mix_kernel,
        grid=(batch // bb, nt),
        in_specs=[wide(COL_GQK), wide(COL_GV), wide(COL_GZ),
                  pl.BlockSpec((bb, tt, LANES), lambda b, t: (b, t, COL_AB // LANES)),
                  pl.BlockSpec((GDN_CONV, qk_w + GDN_V_DIM), full),
                  pl.BlockSpec((1, LANES), full),
                  pl.BlockSpec((1, LANES), full),
                  pl.BlockSpec((1, GDN_HEAD), full)],
        out_specs=wide(0),
        out_shape=jax.ShapeDtypeStruct((batch, seq, GDN_V_DIM), BF16),
        scratch_shapes=[pltpu.VMEM((bb * GDN_QK_HEADS, GDN_HEAD, 2 * GDN_HEAD), F32),
                        pltpu.VMEM((bb, SUBLANES, qk_w + GDN_V_DIM), F32)],
        compiler_params=pltpu.CompilerParams(
            dimension_semantics=("parallel", "arbitrary"), vmem_limit_bytes=VMEM_LIMIT),
        name="gdn_mix",
    )(proj, proj, proj, proj, lp["conv_w"], lp["a_log"], lp["dt_bias"], lp["norm_w"])


def _pad_cols(a, n):
    return jnp.pad(a, [(0, 0)] * (a.ndim - 1) + [(0, n - a.shape[-1])])


def _proj_layout(main, vd):
    n_small = RWKV_DECAY_LORA + RWKV_AAA_LORA + RWKV_GATE_LORA
    rkv = main[..., :3 * RWKV_DIM]
    small = main[..., 3 * RWKV_DIM:3 * RWKV_DIM + n_small]
    gdn = main[..., 3 * RWKV_DIM + n_small:]
    zeros = lambda n: jnp.zeros(main.shape[:-1] + (n,), main.dtype)
    ab_pad = COL_WA - (COL_GQK + gdn.shape[-1] + vd.shape[-1])
    return jnp.concatenate([rkv, gdn, vd, zeros(ab_pad), small, zeros(N_PROJ - COL_WA - n_small)], axis=-1)


def kernel(x, attn_norm_w, w_in, rwkv_mu, rwkv_w0, rwkv_w_up, rwkv_a0, rwkv_a_up, rwkv_g_up,
           rwkv_k_k, rwkv_k_a, rwkv_r_k, rwkv_ln_w, rwkv_ln_b, vres_down, vres_mu, vres_up, vres_v0,
           gdn_conv_w, gdn_A_log, gdn_dt_bias, gdn_norm_w, w_out, ffn_norm_w, ffn_w_gate, ffn_w_up,
           ffn_w_down, final_norm_w):
    batch, seq, d = x.shape
    depth = w_in.shape[0]
    m = batch * seq

    gdn_pad = jnp.zeros((depth, w_in.shape[2] - rwkv_mu.shape[1]), F32)
    vmu = jnp.concatenate([jnp.zeros((1, RWKV_MV_LORA), F32), vres_mu], axis=0)
    mu_all = _proj_layout(jnp.concatenate([rwkv_mu, gdn_pad], axis=1), vmu)
    row_place = lambda a, before, total: jnp.pad(
        a, ((0, 0), (before, total - before - a.shape[1]), (0, 0))).astype(BF16)
    w_up = row_place(rwkv_w_up, 0, 128)
    a_up = row_place(rwkv_a_up, RWKV_DECAY_LORA, 128)
    g_up = row_place(rwkv_g_up, 0, 256)
    v_up = row_place(vres_up, AB_VD, 128)
    a_log = _pad_cols(gdn_A_log, LANES)
    dt_bias = _pad_cols(gdn_dt_bias, LANES)
    r_k = rwkv_r_k.reshape(depth, RWKV_DIM)

    tm = min(1024, m)
    row = lambda a, l: a[l][None, :]

    xf = x.reshape(m, d)
    v_first = None
    for l in range(depth):
        vd_w = (vres_down[l - 1].astype(BF16) if l > 0 else jnp.zeros((d, RWKV_MV_LORA), BF16))
        w_proj = _proj_layout(w_in[l].astype(BF16), vd_w)
        proj = _norm_matmul(xf, row(attn_norm_w, l), w_proj, tm, 1664)
        mu_l = mu_all[l]
        lp = dict(mu_rkv=mu_l[None, COL_RKV:COL_RKV + 3 * RWKV_DIM],
                  mu_sm=jnp.concatenate([mu_l[COL_WA:COL_WA + 128], mu_l[COL_GD:COL_GD + 256],
                                         mu_l[COL_AB:COL_AB + 128]])[None, :],
                  w0=row(rwkv_w0, l), w_up=w_up[l], a0=row(rwkv_a0, l), a_up=a_up[l], g_up=g_up[l],
                  k_k=row(rwkv_k_k, l), k_a=row(rwkv_k_a, l), r_k=row(r_k, l),
                  ln_w=row(rwkv_ln_w, l), ln_b=row(rwkv_ln_b, l),
                  conv_w=gdn_conv_w[l], a_log=row(a_log, l), dt_bias=row(dt_bias, l),
                  norm_w=row(gdn_norm_w, l))
        if l > 0:
            lp["v_up"] = v_up[l - 1]
            lp["v0"] = row(vres_v0, l - 1)
        as3 = lambda a: a.reshape(batch, seq, a.shape[-1])
        bb_a = 2 if batch % 2 == 0 else 1
        if l == 0:
            y_a, v_first = _rwkv_mix(as3(proj), None, lp, batch, seq, min(128, seq), bb_a)
        else:
            y_a = _rwkv_mix(as3(proj), v_first, lp, batch, seq, min(128, seq), bb_a)
        y_a = y_a.reshape(m, RWKV_DIM)
        bb_b = 4 if batch % 4 == 0 else bb_a
        y_b = _gdn_mix(as3(proj), lp, batch, seq, min(128, seq), bb_b).reshape(m, GDN_V_DIM)
        xf = _matmul2_res(y_a, y_b, w_out[l].astype(BF16), xf, tm, 1024)
        hmid = _norm_swiglu(xf, row(ffn_norm_w, l), ffn_w_gate[l].astype(BF16),
                            ffn_w_up[l].astype(BF16), tm, 512)
        xf = _matmul_res(hmid, ffn_w_down[l].astype(BF16), xf, min(512, m), 1024)
    out = _rmsnorm(xf, final_norm_w[None, :], min(512, m))
    return out.reshape(batch, seq, d)
```

```python
import functools

import jax
import jax.numpy as jnp
from jax import lax
from jax.experimental import pallas as pl
from jax.experimental.pallas import tpu as pltpu

F32 = jnp.float32
BF16 = jnp.bfloat16

D_MODEL = 2048
RWKV_HEAD = 64
RWKV_DIM = 1024
RWKV_DECAY_LORA = 64
RWKV_AAA_LORA = 64
RWKV_MV_LORA = 32
RWKV_GATE_LORA = 160
RWKV_GN_EPS = RWKV_HEAD * 1e-5
RWKV_L2_EPS = 1e-12
GDN_HEAD = 128
GDN_V_DIM = 1024
GDN_V_HEADS = 8
GDN_QK_HEADS = 4
GDN_QK_DIM = 512
GDN_CONV = 4
GDN_L2_EPS = 1e-6
GDN_NORM_EPS = 1e-6
D_FF = 5632
NORM_EPS = 1e-5

LANES = 128
SUBLANES = 8
CHUNK = 64
PAIR = 2 * CHUNK
VMEM_LIMIT = 56 * 1024 * 1024

COL_RKV = 0
COL_GQK = 3 * RWKV_DIM
COL_GV = COL_GQK + 2 * GDN_QK_DIM
COL_GZ = COL_GV + GDN_V_DIM
COL_AB = COL_GZ + GDN_V_DIM
COL_WA = COL_AB + 128
COL_GD = COL_WA + 128
N_PROJ = COL_GD + 256
AB_VD = 16
SM_WA, SM_GD, SM_AB = 0, 128, 384
SM_W = 512


def _mm(a, b):
    return jnp.dot(a.astype(BF16), b.astype(BF16), preferred_element_type=F32)


def _mm_nt(a, b):
    return lax.dot_general(a.astype(BF16), b.astype(BF16), (((1,), (1,)), ((), ())),
                           preferred_element_type=F32)


def _mm_tn(a, b):
    return lax.dot_general(a.astype(BF16), b.astype(BF16), (((0,), (0,)), ((), ())),
                           preferred_element_type=F32)


def _split3(x):
    hi = x.astype(BF16)
    r1 = x - hi.astype(F32)
    mid = r1.astype(BF16)
    lo = (r1 - mid.astype(F32)).astype(BF16)
    return hi, mid, lo


def _mm_exact_lhs(a01, x):
    n = x.shape[1]
    hi, mid, lo = _split3(x)
    y = jnp.dot(a01, jnp.concatenate([hi, mid, lo], axis=1), preferred_element_type=F32)
    return y[:, :n] + y[:, n:2 * n] + y[:, 2 * n:]


def _mm_exact_rhs(x, b01):
    m = x.shape[0]
    hi, mid, lo = _split3(x)
    y = jnp.dot(jnp.concatenate([hi, mid, lo], axis=0), b01, preferred_element_type=F32)
    return y[:m] + y[m:2 * m] + y[2 * m:]


def _sigmoid(x):
    return 1.0 / (1.0 + jnp.exp(-x))


def _softplus(x):
    return jnp.maximum(x, 0.0) + jnp.log(1.0 + jnp.exp(-jnp.abs(x)))


def _silu(x):
    return x * _sigmoid(x)


def _iota2(shape):
    return (lax.broadcasted_iota(jnp.int32, shape, 0), lax.broadcasted_iota(jnp.int32, shape, 1))


def _unit_lower_inverse(n_mats):
    sz = n_mats[0].shape[0]
    i, j = _iota2((sz, sz))
    eye = (i == j).astype(F32)
    blk8 = (i // 8) == (j // 8)
    d1 = [jnp.where(blk8, n, 0.0) for n in n_mats]
    d2 = [_mm(x, x) for x in d1]
    d4 = [_mm(x, x) for x in d2]
    p = [eye + x for x in d1]
    p = [x + _mm(x, y) for x, y in zip(p, d2)]
    p = [x + _mm(x, y) for x, y in zip(p, d4)]
    for s in (8, 16, 32):
        off = ((i // (2 * s)) == (j // (2 * s))) & ((i // s) == (j // s) + 1)
        q = [_mm(jnp.where(off, n, 0.0), x) for n, x in zip(n_mats, p)]
        p = [x + _mm(x, y) for x, y in zip(p, q)]
    return p


def _norm_matmul_kernel(x_ref, nw_ref, w_ref, o_ref, h_ref):
    @pl.when(pl.program_id(1) == 0)
    def _():
        x = x_ref[...]
        ms = jnp.mean(x * x, axis=-1, keepdims=True)
        h_ref[...] = (x * lax.rsqrt(ms + NORM_EPS) * nw_ref[...]).astype(BF16)

    o_ref[...] = jnp.dot(h_ref[...], w_ref[...], preferred_element_type=F32)


def _norm_matmul(x, nw, w, layer, tm, tn):
    m, d = x.shape
    n = w.shape[2]
    return pl.pallas_call(
        _norm_matmul_kernel,
        grid=(m // tm, n // tn),
        in_specs=[pl.BlockSpec((tm, d), lambda i, j: (i, 0)),
                  pl.BlockSpec((1, d), lambda i, j: (0, 0)),
                  pl.BlockSpec((None, d, tn), lambda i, j: (layer, 0, j))],
        out_specs=pl.BlockSpec((tm, tn), lambda i, j: (i, j)),
        out_shape=jax.ShapeDtypeStruct((m, n), F32),
        scratch_shapes=[pltpu.VMEM((tm, d), BF16)],
        compiler_params=pltpu.CompilerParams(
            dimension_semantics=("parallel", "arbitrary"), vmem_limit_bytes=VMEM_LIMIT),
        name="norm_in_proj",
    )(x, nw, w)


def _norm_swiglu_kernel(x_ref, nw_ref, wg_ref, wu_ref, o_ref, h_ref):
    @pl.when(pl.program_id(1) == 0)
    def _():
        x = x_ref[...]
        ms = jnp.mean(x * x, axis=-1, keepdims=True)
        h_ref[...] = (x * lax.rsqrt(ms + NORM_EPS) * nw_ref[...]).astype(BF16)

    h = h_ref[...]
    g = jnp.dot(h, wg_ref[...], preferred_element_type=F32)
    u = jnp.dot(h, wu_ref[...], preferred_element_type=F32)
    o_ref[...] = (_silu(g) * u).astype(BF16)


def _norm_swiglu(x, nw, wg, wu, layer, tm, tn):
    m, d = x.shape
    n = wg.shape[2]
    return pl.pallas_call(
        _norm_swiglu_kernel,
        grid=(m // tm, n // tn),
        in_specs=[pl.BlockSpec((tm, d), lambda i, j: (i, 0)),
                  pl.BlockSpec((1, d), lambda i, j: (0, 0)),
                  pl.BlockSpec((None, d, tn), lambda i, j: (layer, 0, j)),
                  pl.BlockSpec((None, d, tn), lambda i, j: (layer, 0, j))],
        out_specs=pl.BlockSpec((tm, tn), lambda i, j: (i, j)),
        out_shape=jax.ShapeDtypeStruct((m, n), BF16),
        scratch_shapes=[pltpu.VMEM((tm, d), BF16)],
        compiler_params=pltpu.CompilerParams(
            dimension_semantics=("parallel", "arbitrary"), vmem_limit_bytes=VMEM_LIMIT),
        name="norm_swiglu",
    )(x, nw, wg, wu)


def _matmul_res_kernel(a_ref, w_ref, r_ref, o_ref):
    o_ref[...] = r_ref[...] + jnp.dot(a_ref[...], w_ref[...], preferred_element_type=F32)


def _matmul_res(a, w, layer, res, tm, tn):
    m, k = a.shape
    n = w.shape[2]
    return pl.pallas_call(
        _matmul_res_kernel,
        grid=(n // tn, m // tm),
        in_specs=[pl.BlockSpec((tm, k), lambda j, i: (i, 0)),
                  pl.BlockSpec((None, k, tn), lambda j, i: (layer, 0, j)),
                  pl.BlockSpec((tm, tn), lambda j, i: (i, j))],
        out_specs=pl.BlockSpec((tm, tn), lambda j, i: (i, j)),
        out_shape=jax.ShapeDtypeStruct((m, n), F32),
        compiler_params=pltpu.CompilerParams(
            dimension_semantics=("parallel", "arbitrary"), vmem_limit_bytes=VMEM_LIMIT),
        name="matmul_residual",
    )(a, w, res)


def _matmul2_res_kernel(a_ref, b_ref, wa_ref, wb_ref, r_ref, o_ref):
    acc = jnp.dot(a_ref[...], wa_ref[...], preferred_element_type=F32)
    acc = acc + jnp.dot(b_ref[...], wb_ref[...], preferred_element_type=F32)
    o_ref[...] = r_ref[...] + acc


def _matmul2_res(a, b, w, layer, res, tm, tn):
    m, k = a.shape
    n = w.shape[2]
    return pl.pallas_call(
        _matmul2_res_kernel,
        grid=(n // tn, m // tm),
        in_specs=[pl.BlockSpec((tm, k), lambda j, i: (i, 0)),
                  pl.BlockSpec((tm, k), lambda j, i: (i, 0)),
                  pl.BlockSpec((None, k, tn), lambda j, i: (layer, 0, j)),
                  pl.BlockSpec((None, k, tn), lambda j, i: (layer, 1, j)),
                  pl.BlockSpec((tm, tn), lambda j, i: (i, j))],
        out_specs=pl.BlockSpec((tm, tn), lambda j, i: (i, j)),
        out_shape=jax.ShapeDtypeStruct((m, n), F32),
        compiler_params=pltpu.CompilerParams(
            dimension_semantics=("parallel", "arbitrary"), vmem_limit_bytes=VMEM_LIMIT),
        name="out_proj_residual",
    )(a, b, w, w, res)


def _rmsnorm_kernel(x_ref, nw_ref, o_ref):
    x = x_ref[...]
    ms = jnp.mean(x * x, axis=-1, keepdims=True)
    o_ref[...] = x * lax.rsqrt(ms + NORM_EPS) * nw_ref[...]


def _rmsnorm(x, nw, tm):
    m, d = x.shape
    return pl.pallas_call(
        _rmsnorm_kernel,
        grid=(m // tm,),
        in_specs=[pl.BlockSpec((tm, d), lambda i: (i, 0)),
                  pl.BlockSpec((1, d), lambda i: (0, 0))],
        out_specs=pl.BlockSpec((tm, d), lambda i: (i, 0)),
        out_shape=jax.ShapeDtypeStruct((m, d), F32),
        compiler_params=pltpu.CompilerParams(dimension_semantics=("parallel",)),
        name="final_rmsnorm",
    )(x, nw)


def _rows_back(x, tail, n_back):
    c = x.shape[0]
    ext = jnp.concatenate([tail, x], axis=0)
    return [pltpu.roll(ext, s, 0)[SUBLANES:SUBLANES + c, :] for s in range(1, n_back + 1)]


def _rwkv_chunk_units(units, ones_bd, tri, m0):
    c = CHUNK
    n2 = 2 * c
    ss = [_mm(u[4] * u[4], ones_bd) for u in units]
    log_w = [_mm_exact_lhs(tri, u[3]) for u in units]
    bon = [_mm(u[0] * u[1] * u[6], ones_bd) for u in units]

    def bd(x):
        return jnp.concatenate([jnp.where(m0, x, 0.0), jnp.where(m0, 0.0, x)], axis=0).astype(BF16)

    lhs, rhs, bk, v_bd, w_last = [], [], [], [], []
    for (r, k, v, lw, kraw, eta, _, _), s2, lg in zip(units, ss, log_w):
        kk = kraw * lax.rsqrt(s2 + RWKV_L2_EPS)
        beta = kk * eta
        lw_last = lg[c - 1:c, :]
        w_inv = jnp.exp(-lg)
        w_end = jnp.exp(lw_last - lg)
        lhs.append(jnp.concatenate([bd(-kk * jnp.exp(lg - lw)), bd(r * jnp.exp(lg))], axis=0))
        rhs.append(jnp.concatenate([bd(beta * w_inv), bd(k * w_inv)], axis=0))
        bk.append(jnp.concatenate([bd(beta * w_end), bd(k * w_end)], axis=0))
        v_bd.append(bd(v))
        w_last.append(jnp.exp(lw_last))

    sc = [_mm_nt(a, b) for a, b in zip(lhs, rhs)]
    xr = [_mm_nt(a, u[7]) for a, u in zip(lhs, units)]
    i, j = _iota2((n2, n2))
    strict = i > j
    incl = i >= j
    a_ab = [jnp.where(strict, x[:n2, :n2], 0.0) for x in sc]
    av = [_mm(jnp.where(strict, x[:n2, n2:], 0.0), vb) for x, vb in zip(sc, v_bd)]
    a_r = [jnp.where(jnp.concatenate([incl, incl], axis=1), x[n2:, :], 0.0).astype(BF16) for x in sc]
    t_inv = _unit_lower_inverse(a_ab)
    u_mat = [_mm(t, x[:n2] + a) for t, x, a in zip(t_inv, xr, av)]
    uv = [jnp.concatenate([x.astype(BF16), vb], axis=0) for x, vb in zip(u_mat, v_bd)]
    y_bd = [x[n2:] + _mm(a, z) for x, a, z in zip(xr, a_r, uv)]
    s_new = [u[7] * wl + _mm_tn(z, b) for u, wl, z, b in zip(units, w_last, uv, bk)]
    y = [x[:c] + x[c:] for x in y_bd]
    return y, bon, s_new


def _rwkv_mix_kernel(has_vres, *refs):
    if has_vres:
        (rkv_ref, wa_ref, gd_ref, ab_ref, vf_ref, mu_rkv_ref, mu_sm_ref, w0_ref, wup_ref, a0_ref,
         aup_ref, gup_ref, kk_ref, ka_ref, rk_ref, lnw_ref, lnb_ref, vup_ref, v0_ref,
         o_ref, s_ref, tail_rkv, tail_sm) = refs
        vout_ref = None
    else:
        (rkv_ref, wa_ref, gd_ref, ab_ref, mu_rkv_ref, mu_sm_ref, w0_ref, wup_ref, a0_ref,
         aup_ref, gup_ref, kk_ref, ka_ref, rk_ref, lnw_ref, lnb_ref,
         o_ref, vout_ref, s_ref, tail_rkv, tail_sm) = refs
    bb, tt = rkv_ref.shape[0], rkv_ref.shape[1]
    n_pairs = RWKV_DIM // LANES

    @pl.when(pl.program_id(1) == 0)
    def _():
        s_ref[...] = jnp.zeros(s_ref.shape, F32)
        tail_rkv[...] = jnp.zeros(tail_rkv.shape, F32)
        tail_sm[...] = jnp.zeros(tail_sm.shape, F32)

    li, lj = _iota2((LANES, LANES))
    ones_bd = ((li // RWKV_HEAD) == (lj // RWKV_HEAD)).astype(BF16)
    ti, tj = _iota2((CHUNK, CHUNK))
    tri = (ti >= tj).astype(BF16)
    m0 = lax.broadcasted_iota(jnp.int32, (CHUNK, LANES), 1) < RWKV_HEAD
    inv_n = 1.0 / RWKV_HEAD
    ids = [(bi, slice(p * LANES, (p + 1) * LANES)) for bi in range(bb) for p in range(n_pairs)]

    def body(ci, carry):
        rows = pl.ds(pl.multiple_of(ci * CHUNK, CHUNK), CHUNK)
        sh, shs = [], []
        for bi in range(bb):
            p = rkv_ref[bi, rows, :]
            (prev,) = _rows_back(p, tail_rkv[bi], 1)
            tail_rkv[bi] = p[CHUNK - SUBLANES:, :]
            sh.append(p + (prev - p) * mu_rkv_ref[...])
            ps = jnp.concatenate([wa_ref[bi, rows, :], gd_ref[bi, rows, :], ab_ref[bi, rows, :]], axis=1)
            (prev_s,) = _rows_back(ps, tail_sm[bi], 1)
            tail_sm[bi] = ps[CHUNK - SUBLANES:, :]
            shs.append(ps + (prev_s - ps) * mu_sm_ref[...])
        w_lo = [_mm(jnp.tanh(x[:, SM_WA:SM_WA + 128]), wup_ref[...]) for x in shs]
        a_lo = [_mm(x[:, SM_WA:SM_WA + 128], aup_ref[...]) for x in shs]
        g_all = [_mm(_sigmoid(x[:, SM_GD:SM_GD + 256]), gup_ref[...]) for x in shs]
        if has_vres:
            v_lo = [_mm(x[:, SM_AB:SM_AB + 128], vup_ref[...]) for x in shs]
        r_all, k_all, v_all, lw_all, kr_all, eta_all = [], [], [], [], [], []
        for bi in range(bb):
            w = -_softplus(-(w0_ref[...] + w_lo[bi])) - 0.5
            lw_all.append(-jnp.exp(w))
            eta = _sigmoid(a0_ref[...] + a_lo[bi])
            k = sh[bi][:, RWKV_DIM:2 * RWKV_DIM]
            v = sh[bi][:, 2 * RWKV_DIM:3 * RWKV_DIM]
            if has_vres:
                v = v + (vf_ref[bi, rows, :] - v) * _sigmoid(v0_ref[...] + v_lo[bi])
            else:
                vout_ref[bi, rows, :] = v
            r_all.append(sh[bi][:, 0:RWKV_DIM])
            v_all.append(v)
            kr_all.append(k * kk_ref[...])
            k_all.append(k * (1.0 + (eta - 1.0) * ka_ref[...]))
            eta_all.append(eta)
        units = [tuple(a[bi][:, cs] for a in (r_all, k_all, v_all, lw_all, kr_all, eta_all))
                 + (rk_ref[:, cs], s_ref[n]) for n, (bi, cs) in enumerate(ids)]
        y, bon, s_new = _rwkv_chunk_units(units, ones_bd, tri, m0)
        mean = [_mm(x, ones_bd) * inv_n for x in y]
        d = [x - mu for x, mu in zip(y, mean)]
        var = [_mm(x * x, ones_bd) * inv_n for x in d]
        for n, (bi, cs) in enumerate(ids):
            yn = d[n] * lax.rsqrt(var[n] + RWKV_GN_EPS) * lnw_ref[:, cs] + lnb_ref[:, cs]
            o_ref[bi, rows, cs] = ((yn + bon[n] * units[n][2]) * g_all[bi][:, cs]).astype(BF16)
            s_ref[n] = s_new[n]
        return carry

    lax.fori_loop(0, tt // CHUNK, body, 0)


def _rwkv_mix(proj, v_first, lp, batch, seq, tt, bb):
    nt = seq // tt
    has_vres = v_first is not None
    full = lambda b, t: (0, 0)
    wide = pl.BlockSpec((bb, tt, RWKV_DIM), lambda b, t: (b, t, 0))
    vec = pl.BlockSpec((1, RWKV_DIM), full)
    lora = pl.BlockSpec((128, RWKV_DIM), full)
    col_blk = lambda col, w: pl.BlockSpec((bb, tt, w), lambda b, t: (b, t, col // w))
    in_specs = [col_blk(COL_RKV, 3 * RWKV_DIM), col_blk(COL_WA, 128), col_blk(COL_GD, 256),
                col_blk(COL_AB, 128)]
    args = [proj, proj, proj, proj]
    if has_vres:
        in_specs.append(wide)
        args.append(v_first)
    in_specs += [pl.BlockSpec((1, 3 * RWKV_DIM), full), pl.BlockSpec((1, SM_W), full),
                 vec, lora, vec, lora, pl.BlockSpec((256, RWKV_DIM), full), vec, vec, vec, vec, vec]
    args += [lp["mu_rkv"], lp["mu_sm"], lp["w0"], lp["w_up"], lp["a0"], lp["a_up"], lp["g_up"],
             lp["k_k"], lp["k_a"], lp["r_k"], lp["ln_w"], lp["ln_b"]]
    y_shape = jax.ShapeDtypeStruct((batch, seq, RWKV_DIM), BF16)
    if has_vres:
        in_specs += [lora, vec]
        args += [lp["v_up"], lp["v0"]]
        out_specs, out_shape = wide, y_shape
    else:
        out_specs = [wide, wide]
        out_shape = [y_shape, jax.ShapeDtypeStruct((batch, seq, RWKV_DIM), F32)]
    return pl.pallas_call(
        functools.partial(_rwkv_mix_kernel, has_vres),
        grid=(batch // bb, nt),
        in_specs=in_specs,
        out_specs=out_specs,
        out_shape=out_shape,
        scratch_shapes=[pltpu.VMEM((bb * RWKV_DIM // LANES, LANES, LANES), F32),
                        pltpu.VMEM((bb, SUBLANES, 3 * RWKV_DIM), F32),
                        pltpu.VMEM((bb, SUBLANES, SM_W), F32)],
        compiler_params=pltpu.CompilerParams(
            dimension_semantics=("parallel", "arbitrary"), vmem_limit_bytes=VMEM_LIMIT),
        name="rwkv_mix",
    )(*args)


def _gdn_chunk_prep(x, tail, ab, cw, a_log, dt_bias, tri):
    x1, x2, x3 = _rows_back(x, tail, GDN_CONV - 1)
    y = _silu(x * cw[3:4, :] + x1 * cw[2:3, :] + x2 * cw[1:2, :] + x3 * cw[0:1, :])
    q, k = [], []
    for h in range(GDN_QK_HEADS):
        qh = y[:, h * GDN_HEAD:(h + 1) * GDN_HEAD]
        kh = y[:, GDN_QK_DIM + h * GDN_HEAD:GDN_QK_DIM + (h + 1) * GDN_HEAD]
        q.append(qh * (lax.rsqrt(jnp.sum(qh * qh, axis=-1, keepdims=True) + GDN_L2_EPS)
                       * (GDN_HEAD ** -0.5)))
        k.append(kh * lax.rsqrt(jnp.sum(kh * kh, axis=-1, keepdims=True) + GDN_L2_EPS))
    g = -jnp.exp(a_log) * _softplus(ab + dt_bias)
    beta = _sigmoid(ab)
    g_cum = _mm_exact_lhs(tri, g)
    bc = lambda a, j: jnp.broadcast_to(a[:, j:j + 1], (a.shape[0], GDN_HEAD))
    g_b = jnp.concatenate([bc(g_cum, j) for j in range(GDN_V_HEADS)], axis=1)
    beta_b = jnp.concatenate([bc(beta, GDN_V_HEADS + j) for j in range(GDN_V_HEADS)], axis=1)
    return q, k, y[:, 2 * GDN_QK_DIM:], g_b, beta_b


def _gdn_chunk_units(units):
    c = CHUNK
    n2 = 2 * c
    st = lambda x: jnp.concatenate([x[:, :GDN_HEAD], x[:, GDN_HEAD:]], axis=0)
    i, j = _iota2((n2, n2))
    incl = ((i // c) == (j // c)) & (i >= j)
    strict = i > j

    g_c = [st(u[3]) for u in units]
    b_c = [st(u[4]) for u in units]
    k2 = [jnp.concatenate([u[1], u[1]], axis=0) for u in units]
    q2 = [jnp.concatenate([u[0], u[0]], axis=0) for u in units]
    kb = [a * b for a, b in zip(k2, b_c)]
    kq = [_mm_nt(jnp.concatenate([a, b], axis=0), kk) for a, b, kk in zip(kb, q2, k2)]
    gamma = [jnp.exp(jnp.where(incl, g - g.T, -jnp.inf)) for g in g_c]
    l_neg = [jnp.where(strict, -(x[:n2] * gm), 0.0) for x, gm in zip(kq, gamma)]
    a_qk = [(x[n2:] * gm).astype(BF16) for x, gm in zip(kq, gamma)]
    t_inv = _unit_lower_inverse(l_neg)
    e_g = [jnp.exp(g) for g in g_c]
    uw = [_mm(t, jnp.concatenate([st(u[2]) * b, kbi * eg], axis=1))
          for t, u, b, kbi, eg in zip(t_inv, units, b_c, kb, e_g)]
    qg = [qq * eg for qq, eg in zip(q2, e_g)]
    hd = lambda e: slice(e * GDN_HEAD, (e + 1) * GDN_HEAD)
    rw = lambda e: slice(e * c, (e + 1) * c)
    wq = [[_mm(jnp.concatenate([x[rw(e), GDN_HEAD:], qe[rw(e)]], axis=0), u[5][:, hd(e)])
           for x, qe, u in zip(uw, qg, units)] for e in range(2)]
    v_new = [jnp.concatenate([x[rw(0), :GDN_HEAD] - w0[:c], x[rw(1), :GDN_HEAD] - w1[:c]], axis=0)
             for x, w0, w1 in zip(uw, wq[0], wq[1])]
    o_s = [jnp.concatenate([w0[c:], w1[c:]], axis=0) + _mm(a, vn)
           for w0, w1, a, vn in zip(wq[0], wq[1], a_qk, v_new)]
    k_g, decay = [], []
    for g, u in zip(g_c, units):
        g_last = (g[c - 1:c, :], g[n2 - 1:n2, :])
        k_g.append([u[1] * jnp.exp(g_last[e] - g[rw(e)]) for e in range(2)])
        decay.append((jnp.exp(g_last[0]), jnp.exp(g_last[1])))
    ds = [[_mm_tn(kg[e], vn[rw(e)]) for kg, vn in zip(k_g, v_new)] for e in range(2)]
    s_new = [jnp.concatenate([u[5][:, hd(0)] * dc[0] + d0, u[5][:, hd(1)] * dc[1] + d1], axis=1)
             for u, dc, d0, d1 in zip(units, decay, ds[0], ds[1])]
    outs = [jnp.concatenate([o[:c], o[c:]], axis=1) for o in o_s]
    return outs, s_new


def _gdn_mix_kernel(qk_ref, v_ref, z_ref, ab_ref, cw_ref, alog_ref, dtb_ref, nw_ref,
                    o_ref, s_ref, tail_ref):
    bb, tt = qk_ref.shape[0], qk_ref.shape[1]
    pw = 2 * GDN_HEAD

    @pl.when(pl.program_id(1) == 0)
    def _():
        s_ref[...] = jnp.zeros(s_ref.shape, F32)
        tail_ref[...] = jnp.zeros(tail_ref.shape, F32)

    ids = [(bi, h) for bi in range(bb) for h in range(GDN_QK_HEADS)]
    ti, tj = _iota2((CHUNK, CHUNK))
    tri = (ti >= tj).astype(BF16)

    def body(ci, carry):
        rows = pl.ds(pl.multiple_of(ci * CHUNK, CHUNK), CHUNK)
        rowp = []
        for bi in range(bb):
            x = jnp.concatenate([qk_ref[bi, rows, :], v_ref[bi, rows, :]], axis=1)
            rowp.append(_gdn_chunk_prep(x, tail_ref[bi], ab_ref[bi, rows, :],
                                        cw_ref[...], alog_ref[...], dtb_ref[...], tri))
            tail_ref[bi] = x[CHUNK - SUBLANES:, :]
        units = []
        for n, (bi, h) in enumerate(ids):
            q, k, v, g_b, beta_b = rowp[bi]
            cv = slice(h * pw, (h + 1) * pw)
            units.append((q[h], k[h], v[:, cv], g_b[:, cv], beta_b[:, cv], s_ref[n]))
        outs, s_new = _gdn_chunk_units(units)
        for n, (bi, h) in enumerate(ids):
            s_ref[n] = s_new[n]
            for e in range(2):
                cols = slice(h * pw + e * GDN_HEAD, h * pw + (e + 1) * GDN_HEAD)
                oh = outs[n][:, e * GDN_HEAD:(e + 1) * GDN_HEAD]
                oh = oh * lax.rsqrt(jnp.mean(oh * oh, axis=-1, keepdims=True) + GDN_NORM_EPS) * nw_ref[...]
                o_ref[bi, rows, cols] = (oh * _silu(z_ref[bi, rows, cols])).astype(BF16)
        return carry

    lax.fori_loop(0, tt // CHUNK, body, 0)


def _gdn_mix(proj, lp, batch, seq, tt, bb):
    nt = seq // tt
    full = lambda b, t: (0, 0)
    qk_w = 2 * GDN_QK_DIM
    wide = lambda col: pl.BlockSpec((bb, tt, GDN_V_DIM), lambda b, t: (b, t, col // GDN_V_DIM))
    return pl.pallas_call(
        _gdn_mix_kernel,
        grid=(batch // bb, nt),
        in_specs=[wide(COL_GQK), wide(COL_GV), wide(COL_GZ),
                  pl.BlockSpec((bb, tt, LANES), lambda b, t: (b, t, COL_AB // LANES)),
                  pl.BlockSpec((GDN_CONV, qk_w + GDN_V_DIM), full),
                  pl.BlockSpec((1, LANES), full),
                  pl.BlockSpec((1, LANES), full),
                  pl.BlockSpec((1, GDN_HEAD), full)],
        out_specs=wide(0),
        out_shape=jax.ShapeDtypeStruct((batch, seq, GDN_V_DIM), BF16),
        scratch_shapes=[pltpu.VMEM((bb * GDN_QK_HEADS, GDN_HEAD, 2 * GDN_HEAD), F32),
                        pltpu.VMEM((bb, SUBLANES, qk_w + GDN_V_DIM), F32)],
        compiler_params=pltpu.CompilerParams(
            dimension_semantics=("parallel", "arbitrary"), vmem_limit_bytes=VMEM_LIMIT),
        name="gdn_mix",
    )(proj, proj, proj, proj, lp["conv_w"], lp["a_log"], lp["dt_bias"], lp["norm_w"])


def _pad_cols(a, n):
    return jnp.pad(a, [(0, 0)] * (a.ndim - 1) + [(0, n - a.shape[-1])])


def _proj_layout(main, vd):
    n_small = RWKV_DECAY_LORA + RWKV_AAA_LORA + RWKV_GATE_LORA
    rkv = main[..., :3 * RWKV_DIM]
    small = main[..., 3 * RWKV_DIM:3 * RWKV_DIM + n_small]
    gdn = main[..., 3 * RWKV_DIM + n_small:]
    zeros = lambda n: jnp.zeros(main.shape[:-1] + (n,), main.dtype)
    ab_pad = COL_WA - (COL_GQK + gdn.shape[-1] + vd.shape[-1])
    return jnp.concatenate([rkv, gdn, vd, zeros(ab_pad), small, zeros(N_PROJ - COL_WA - n_small)], axis=-1)


def kernel(x, attn_norm_w, w_in, rwkv_mu, rwkv_w0, rwkv_w_up, rwkv_a0, rwkv_a_up, rwkv_g_up,
           rwkv_k_k, rwkv_k_a, rwkv_r_k, rwkv_ln_w, rwkv_ln_b, vres_down, vres_mu, vres_up, vres_v0,
           gdn_conv_w, gdn_A_log, gdn_dt_bias, gdn_norm_w, w_out, ffn_norm_w, ffn_w_gate, ffn_w_up,
           ffn_w_down, final_norm_w):
    batch, seq, d = x.shape
    depth = w_in.shape[0]
    m = batch * seq

    gdn_pad = jnp.zeros((depth, w_in.shape[2] - rwkv_mu.shape[1]), F32)
    vmu = jnp.concatenate([jnp.zeros((1, RWKV_MV_LORA), F32), vres_mu], axis=0)
    mu_all = _proj_layout(jnp.concatenate([rwkv_mu, gdn_pad], axis=1), vmu)
    row_place = lambda a, before, total: jnp.pad(
        a, ((0, 0), (before, total - before - a.shape[1]), (0, 0))).astype(BF16)
    w_up = row_place(rwkv_w_up, 0, 128)
    a_up = row_place(rwkv_a_up, RWKV_DECAY_LORA, 128)
    g_up = row_place(rwkv_g_up, 0, 256)
    v_up = row_place(vres_up, AB_VD, 128)
    a_log = _pad_cols(gdn_A_log, LANES)
    dt_bias = _pad_cols(gdn_dt_bias, LANES)
    r_k = rwkv_r_k.reshape(depth, RWKV_DIM)
    vd_w = jnp.concatenate([jnp.zeros((1, d, RWKV_MV_LORA), BF16), vres_down.astype(BF16)], axis=0)
    w_proj = _proj_layout(w_in.astype(BF16), vd_w)
    w_out_b = w_out.astype(BF16)
    wg_b = ffn_w_gate.astype(BF16)
    wu_b = ffn_w_up.astype(BF16)
    wd_b = ffn_w_down.astype(BF16)

    tm = min(1024, m)
    row = lambda a, l: a[l][None, :]

    xf = x.reshape(m, d)
    v_first = None
    for l in range(depth):
        proj = _norm_matmul(xf, row(attn_norm_w, l), w_proj, l, tm, 1664)
        mu_l = mu_all[l]
        lp = dict(mu_rkv=mu_l[None, COL_RKV:COL_RKV + 3 * RWKV_DIM],
                  mu_sm=jnp.concatenate([mu_l[COL_WA:COL_WA + 128], mu_l[COL_GD:COL_GD + 256],
                                         mu_l[COL_AB:COL_AB + 128]])[None, :],
                  w0=row(rwkv_w0, l), w_up=w_up[l], a0=row(rwkv_a0, l), a_up=a_up[l], g_up=g_up[l],
                  k_k=row(rwkv_k_k, l), k_a=row(rwkv_k_a, l), r_k=row(r_k, l),
                  ln_w=row(rwkv_ln_w, l), ln_b=row(rwkv_ln_b, l),
                  conv_w=gdn_conv_w[l], a_log=row(a_log, l), dt_bias=row(dt_bias, l),
                  norm_w=row(gdn_norm_w, l))
        if l > 0:
            lp["v_up"] = v_up[l - 1]
            lp["v0"] = row(vres_v0, l - 1)
        as3 = lambda a: a.reshape(batch, seq, a.shape[-1])
        bb_a = 2 if batch % 2 == 0 else 1
        if l == 0:
            y_a, v_first = _rwkv_mix(as3(proj), None, lp, batch, seq, min(128, seq), bb_a)
        else:
            y_a = _rwkv_mix(as3(proj), v_first, lp, batch, seq, min(128, seq), bb_a)
        y_a = y_a.reshape(m, RWKV_DIM)
        bb_b = 4 if batch % 4 == 0 else bb_a
        y_b = _gdn_mix(as3(proj), lp, batch, seq, min(128, seq), bb_b).reshape(m, GDN_V_DIM)
        xf = _matmul2_res(y_a, y_b, w_out_b, l, xf, tm, 1024)
        hmid = _norm_swiglu(xf, row(ffn_norm_w, l), wg_b, wu_b, l, tm, 512)
        xf = _matmul_res(hmid, wd_b, l, xf, min(512, m), 1024)
    out = _rmsnorm(xf, final_norm_w[None, :], min(512, m))
    return out.reshape(batch, seq, d)
```

```python
import functools

import jax
import jax.numpy as jnp
from jax import lax
from jax.experimental import pallas as pl
from jax.experimental.pallas import tpu as pltpu

F32 = jnp.float32
BF16 = jnp.bfloat16

D_MODEL = 2048
RWKV_HEAD = 64
RWKV_DIM = 1024
RWKV_DECAY_LORA = 64
RWKV_AAA_LORA = 64
RWKV_MV_LORA = 32
RWKV_GATE_LORA = 160
RWKV_GN_EPS = RWKV_HEAD * 1e-5
RWKV_L2_EPS = 1e-12
GDN_HEAD = 128
GDN_V_DIM = 1024
GDN_V_HEADS = 8
GDN_QK_HEADS = 4
GDN_QK_DIM = 512
GDN_CONV = 4
GDN_L2_EPS = 1e-6
GDN_NORM_EPS = 1e-6
D_FF = 5632
NORM_EPS = 1e-5

LANES = 128
SUBLANES = 8
CHUNK = 64
PAIR = 2 * CHUNK
VMEM_LIMIT = 56 * 1024 * 1024

COL_RKV = 0
COL_GQK = 3 * RWKV_DIM
COL_GV = COL_GQK + 2 * GDN_QK_DIM
COL_GZ = COL_GV + GDN_V_DIM
COL_AB = COL_GZ + GDN_V_DIM
COL_WA = COL_AB + 128
COL_GD = COL_WA + 128
N_PROJ = COL_GD + 256
AB_VD = 16
SM_WA, SM_GD, SM_AB = 0, 128, 384
SM_W = 512


def _mm(a, b):
    return jnp.dot(a.astype(BF16), b.astype(BF16), preferred_element_type=F32)


def _mm_nt(a, b):
    return lax.dot_general(a.astype(BF16), b.astype(BF16), (((1,), (1,)), ((), ())),
                           preferred_element_type=F32)


def _mm_tn(a, b):
    return lax.dot_general(a.astype(BF16), b.astype(BF16), (((0,), (0,)), ((), ())),
                           preferred_element_type=F32)


def _split3(x):
    hi = x.astype(BF16)
    r1 = x - hi.astype(F32)
    mid = r1.astype(BF16)
    lo = (r1 - mid.astype(F32)).astype(BF16)
    return hi, mid, lo


def _mm_exact_lhs(a01, x):
    n = x.shape[1]
    hi, mid, lo = _split3(x)
    y = jnp.dot(a01, jnp.concatenate([hi, mid, lo], axis=1), preferred_element_type=F32)
    return y[:, :n] + y[:, n:2 * n] + y[:, 2 * n:]


def _sigmoid(x):
    return 1.0 / (1.0 + jnp.exp(-x))


def _softplus(x):
    return jnp.maximum(x, 0.0) + jnp.log(1.0 + jnp.exp(-jnp.abs(x)))


def _silu(x):
    return x * _sigmoid(x)


def _iota2(shape):
    return (lax.broadcasted_iota(jnp.int32, shape, 0), lax.broadcasted_iota(jnp.int32, shape, 1))


def _unit_lower_inverse(n_mats):
    sz = n_mats[0].shape[0]
    i, j = _iota2((sz, sz))
    eye = (i == j).astype(F32)
    blk8 = (i // 8) == (j // 8)
    d1 = [jnp.where(blk8, n, 0.0) for n in n_mats]
    d2 = [_mm(x, x) for x in d1]
    d4 = [_mm(x, x) for x in d2]
    p = [eye + x for x in d1]
    p = [x + _mm(x, y) for x, y in zip(p, d2)]
    p = [x + _mm(x, y) for x, y in zip(p, d4)]
    for s in (8, 16, 32):
        off = ((i // (2 * s)) == (j // (2 * s))) & ((i // s) == (j // s) + 1)
        q = [_mm(jnp.where(off, n, 0.0), x) for n, x in zip(n_mats, p)]
        p = [x + _mm(x, y) for x, y in zip(p, q)]
    return p


def _norm_matmul_kernel(x_ref, nw_ref, w_ref, o_ref, h_ref):
    @pl.when(pl.program_id(1) == 0)
    def _():
        x = x_ref[...]
        ms = jnp.mean(x * x, axis=-1, keepdims=True)
        h_ref[...] = (x * lax.rsqrt(ms + NORM_EPS) * nw_ref[...]).astype(BF16)

    o_ref[...] = jnp.dot(h_ref[...], w_ref[...], preferred_element_type=F32)


def _norm_matmul(x, nw, w, layer, tm, tn):
    m, d = x.shape
    n = w.shape[2]
    return pl.pallas_call(
        _norm_matmul_kernel,
        grid=(m // tm, n // tn),
        in_specs=[pl.BlockSpec((tm, d), lambda i, j: (i, 0)),
                  pl.BlockSpec((1, d), lambda i, j: (0, 0)),
                  pl.BlockSpec((None, d, tn), lambda i, j: (layer, 0, j))],
        out_specs=pl.BlockSpec((tm, tn), lambda i, j: (i, j)),
        out_shape=jax.ShapeDtypeStruct((m, n), F32),
        scratch_shapes=[pltpu.VMEM((tm, d), BF16)],
        compiler_params=pltpu.CompilerParams(
            dimension_semantics=("parallel", "arbitrary"), vmem_limit_bytes=VMEM_LIMIT),
        name="norm_in_proj",
    )(x, nw, w)


def _norm_swiglu_kernel(x_ref, nw_ref, wg_ref, wu_ref, o_ref, h_ref):
    @pl.when(pl.program_id(1) == 0)
    def _():
        x = x_ref[...]
        ms = jnp.mean(x * x, axis=-1, keepdims=True)
        h_ref[...] = (x * lax.rsqrt(ms + NORM_EPS) * nw_ref[...]).astype(BF16)

    h = h_ref[...]
    g = jnp.dot(h, wg_ref[...], preferred_element_type=F32)
    u = jnp.dot(h, wu_ref[...], preferred_element_type=F32)
    o_ref[...] = (_silu(g) * u).astype(BF16)


def _norm_swiglu(x, nw, wg, wu, layer, tm, tn):
    m, d = x.shape
    n = wg.shape[2]
    return pl.pallas_call(
        _norm_swiglu_kernel,
        grid=(m // tm, n // tn),
        in_specs=[pl.BlockSpec((tm, d), lambda i, j: (i, 0)),
                  pl.BlockSpec((1, d), lambda i, j: (0, 0)),
                  pl.BlockSpec((None, d, tn), lambda i, j: (layer, 0, j)),
                  pl.BlockSpec((None, d, tn), lambda i, j: (layer, 0, j))],
        out_specs=pl.BlockSpec((tm, tn), lambda i, j: (i, j)),
        out_shape=jax.ShapeDtypeStruct((m, n), BF16),
        scratch_shapes=[pltpu.VMEM((tm, d), BF16)],
        compiler_params=pltpu.CompilerParams(
            dimension_semantics=("parallel", "arbitrary"), vmem_limit_bytes=VMEM_LIMIT),
        name="norm_swiglu",
    )(x, nw, wg, wu)


def _matmul_res_kernel(a_ref, w_ref, r_ref, o_ref):
    o_ref[...] = r_ref[...] + jnp.dot(a_ref[...], w_ref[...], preferred_element_type=F32)


def _matmul_res(a, w, layer, res, tm, tn):
    m, k = a.shape
    n = w.shape[2]
    return pl.pallas_call(
        _matmul_res_kernel,
        grid=(n // tn, m // tm),
        in_specs=[pl.BlockSpec((tm, k), lambda j, i: (i, 0)),
                  pl.BlockSpec((None, k, tn), lambda j, i: (layer, 0, j)),
                  pl.BlockSpec((tm, tn), lambda j, i: (i, j))],
        out_specs=pl.BlockSpec((tm, tn), lambda j, i: (i, j)),
        out_shape=jax.ShapeDtypeStruct((m, n), F32),
        compiler_params=pltpu.CompilerParams(
            dimension_semantics=("parallel", "arbitrary"), vmem_limit_bytes=VMEM_LIMIT),
        name="matmul_residual",
    )(a, w, res)


def _matmul_res_norm_kernel(a_ref, w_ref, r_ref, nw_ref, o_ref):
    x = r_ref[...] + jnp.dot(a_ref[...], w_ref[...], preferred_element_type=F32)
    ms = jnp.mean(x * x, axis=-1, keepdims=True)
    o_ref[...] = x * lax.rsqrt(ms + NORM_EPS) * nw_ref[...]


def _matmul_res_norm(a, w, layer, res, nw, tm):
    m, k = a.shape
    n = w.shape[2]
    return pl.pallas_call(
        _matmul_res_norm_kernel,
        grid=(m // tm,),
        in_specs=[pl.BlockSpec((tm, k), lambda i: (i, 0)),
                  pl.BlockSpec((None, k, n), lambda i: (layer, 0, 0), pipeline_mode=pl.Buffered(1)),
                  pl.BlockSpec((tm, n), lambda i: (i, 0)),
                  pl.BlockSpec((1, n), lambda i: (0, 0))],
        out_specs=pl.BlockSpec((tm, n), lambda i: (i, 0)),
        out_shape=jax.ShapeDtypeStruct((m, n), F32),
        compiler_params=pltpu.CompilerParams(
            dimension_semantics=("arbitrary",), vmem_limit_bytes=VMEM_LIMIT),
        name="matmul_residual_final_norm",
    )(a, w, res, nw)


def _matmul2_res_kernel(a_ref, b_ref, wa_ref, wb_ref, r_ref, o_ref):
    acc = jnp.dot(a_ref[...], wa_ref[...], preferred_element_type=F32)
    acc = acc + jnp.dot(b_ref[...], wb_ref[...], preferred_element_type=F32)
    o_ref[...] = r_ref[...] + acc


def _matmul2_res(a, b, w, layer, res, tm, tn):
    m, k = a.shape
    n = w.shape[2]
    return pl.pallas_call(
        _matmul2_res_kernel,
        grid=(n // tn, m // tm),
        in_specs=[pl.BlockSpec((tm, k), lambda j, i: (i, 0)),
                  pl.BlockSpec((tm, k), lambda j, i: (i, 0)),
                  pl.BlockSpec((None, k, tn), lambda j, i: (layer, 0, j)),
                  pl.BlockSpec((None, k, tn), lambda j, i: (layer, 1, j)),
                  pl.BlockSpec((tm, tn), lambda j, i: (i, j))],
        out_specs=pl.BlockSpec((tm, tn), lambda j, i: (i, j)),
        out_shape=jax.ShapeDtypeStruct((m, n), F32),
        compiler_params=pltpu.CompilerParams(
            dimension_semantics=("parallel", "arbitrary"), vmem_limit_bytes=VMEM_LIMIT),
        name="out_proj_residual",
    )(a, b, w, w, res)


def _rows_back(x, tail, n_back):
    c = x.shape[0]
    ext = jnp.concatenate([tail, x], axis=0)
    return [pltpu.roll(ext, s, 0)[SUBLANES:SUBLANES + c, :] for s in range(1, n_back + 1)]


def _rwkv_chunk_units(units, ones_bd, tri, m0):
    c = CHUNK
    n2 = 2 * c
    ss = [_mm(u[4] * u[4], ones_bd) for u in units]
    log_w = [_mm_exact_lhs(tri, u[3]) for u in units]
    bon = [_mm(u[0] * u[1] * u[6], ones_bd) for u in units]

    def bd(x):
        return jnp.concatenate([jnp.where(m0, x, 0.0), jnp.where(m0, 0.0, x)], axis=0).astype(BF16)

    lhs, rhs, bk, v_bd, w_last = [], [], [], [], []
    for (r, k, v, lw, kraw, eta, _, _), s2, lg in zip(units, ss, log_w):
        kk = kraw * lax.rsqrt(s2 + RWKV_L2_EPS)
        beta = kk * eta
        lw_last = lg[c - 1:c, :]
        w_inv = jnp.exp(-lg)
        w_end = jnp.exp(lw_last - lg)
        lhs.append(jnp.concatenate([bd(-kk * jnp.exp(lg - lw)), bd(r * jnp.exp(lg))], axis=0))
        rhs.append(jnp.concatenate([bd(beta * w_inv), bd(k * w_inv)], axis=0))
        bk.append(jnp.concatenate([bd(beta * w_end), bd(k * w_end)], axis=0))
        v_bd.append(bd(v))
        w_last.append(jnp.exp(lw_last))

    sc = [_mm_nt(a, b) for a, b in zip(lhs, rhs)]
    xr = [_mm_nt(a, u[7]) for a, u in zip(lhs, units)]
    i, j = _iota2((n2, n2))
    strict = i > j
    incl = i >= j
    a_ab = [jnp.where(strict, x[:n2, :n2], 0.0) for x in sc]
    av = [_mm(jnp.where(strict, x[:n2, n2:], 0.0), vb) for x, vb in zip(sc, v_bd)]
    a_r = [jnp.where(jnp.concatenate([incl, incl], axis=1), x[n2:, :], 0.0).astype(BF16) for x in sc]
    t_inv = _unit_lower_inverse(a_ab)
    u_mat = [_mm(t, x[:n2] + a) for t, x, a in zip(t_inv, xr, av)]
    uv = [jnp.concatenate([x.astype(BF16), vb], axis=0) for x, vb in zip(u_mat, v_bd)]
    y_bd = [x[n2:] + _mm(a, z) for x, a, z in zip(xr, a_r, uv)]
    s_new = [u[7] * wl + _mm_tn(z, b) for u, wl, z, b in zip(units, w_last, uv, bk)]
    y = [x[:c] + x[c:] for x in y_bd]
    return y, bon, s_new


def _rwkv_mix_kernel(has_vres, *refs):
    if has_vres:
        (rkv_ref, wa_ref, gd_ref, ab_ref, vf_ref, mu_rkv_ref, mu_sm_ref, w0_ref, wup_ref, a0_ref,
         aup_ref, gup_ref, kk_ref, ka_ref, rk_ref, lnw_ref, lnb_ref, vup_ref, v0_ref,
         o_ref, s_ref, tail_rkv, tail_sm) = refs
        vout_ref = None
    else:
        (rkv_ref, wa_ref, gd_ref, ab_ref, mu_rkv_ref, mu_sm_ref, w0_ref, wup_ref, a0_ref,
         aup_ref, gup_ref, kk_ref, ka_ref, rk_ref, lnw_ref, lnb_ref,
         o_ref, vout_ref, s_ref, tail_rkv, tail_sm) = refs
    bb, tt = rkv_ref.shape[0], rkv_ref.shape[1]
    n_pairs = RWKV_DIM // LANES

    @pl.when(pl.program_id(1) == 0)
    def _():
        s_ref[...] = jnp.zeros(s_ref.shape, F32)
        tail_rkv[...] = jnp.zeros(tail_rkv.shape, F32)
        tail_sm[...] = jnp.zeros(tail_sm.shape, F32)

    li, lj = _iota2((LANES, LANES))
    ones_bd = ((li // RWKV_HEAD) == (lj // RWKV_HEAD)).astype(BF16)
    ti, tj = _iota2((CHUNK, CHUNK))
    tri = (ti >= tj).astype(BF16)
    m0 = lax.broadcasted_iota(jnp.int32, (CHUNK, LANES), 1) < RWKV_HEAD
    inv_n = 1.0 / RWKV_HEAD
    ids = [(bi, slice(p * LANES, (p + 1) * LANES)) for bi in range(bb) for p in range(n_pairs)]

    def body(ci, carry):
        rows = pl.ds(pl.multiple_of(ci * CHUNK, CHUNK), CHUNK)
        sh, shs = [], []
        for bi in range(bb):
            p = rkv_ref[bi, rows, :]
            (prev,) = _rows_back(p, tail_rkv[bi], 1)
            tail_rkv[bi] = p[CHUNK - SUBLANES:, :]
            sh.append(p + (prev - p) * mu_rkv_ref[...])
            ps = jnp.concatenate([wa_ref[bi, rows, :], gd_ref[bi, rows, :], ab_ref[bi, rows, :]], axis=1)
            (prev_s,) = _rows_back(ps, tail_sm[bi], 1)
            tail_sm[bi] = ps[CHUNK - SUBLANES:, :]
            shs.append(ps + (prev_s - ps) * mu_sm_ref[...])
        w_lo = [_mm(jnp.tanh(x[:, SM_WA:SM_WA + 128]), wup_ref[...]) for x in shs]
        a_lo = [_mm(x[:, SM_WA:SM_WA + 128], aup_ref[...]) for x in shs]
        g_all = [_mm(_sigmoid(x[:, SM_GD:SM_GD + 256]), gup_ref[...]) for x in shs]
        if has_vres:
            v_lo = [_mm(x[:, SM_AB:SM_AB + 128], vup_ref[...]) for x in shs]
        r_all, k_all, v_all, lw_all, kr_all, eta_all = [], [], [], [], [], []
        for bi in range(bb):
            w = -_softplus(-(w0_ref[...] + w_lo[bi])) - 0.5
            lw_all.append(-jnp.exp(w))
            eta = _sigmoid(a0_ref[...] + a_lo[bi])
            k = sh[bi][:, RWKV_DIM:2 * RWKV_DIM]
            v = sh[bi][:, 2 * RWKV_DIM:3 * RWKV_DIM]
            if has_vres:
                v = v + (vf_ref[bi, rows, :] - v) * _sigmoid(v0_ref[...] + v_lo[bi])
            else:
                vout_ref[bi, rows, :] = v
            r_all.append(sh[bi][:, 0:RWKV_DIM])
            v_all.append(v)
            kr_all.append(k * kk_ref[...])
            k_all.append(k * (1.0 + (eta - 1.0) * ka_ref[...]))
            eta_all.append(eta)
        units = [tuple(a[bi][:, cs] for a in (r_all, k_all, v_all, lw_all, kr_all, eta_all))
                 + (rk_ref[:, cs], s_ref[n]) for n, (bi, cs) in enumerate(ids)]
        y, bon, s_new = _rwkv_chunk_units(units, ones_bd, tri, m0)
        mean = [_mm(x, ones_bd) * inv_n for x in y]
        d = [x - mu for x, mu in zip(y, mean)]
        var = [_mm(x * x, ones_bd) * inv_n for x in d]
        for n, (bi, cs) in enumerate(ids):
            yn = d[n] * lax.rsqrt(var[n] + RWKV_GN_EPS) * lnw_ref[:, cs] + lnb_ref[:, cs]
            o_ref[bi, rows, cs] = ((yn + bon[n] * units[n][2]) * g_all[bi][:, cs]).astype(BF16)
            s_ref[n] = s_new[n]
        return carry

    lax.fori_loop(0, tt // CHUNK, body, 0)


def _rwkv_mix(proj, v_first, lp, batch, seq, tt, bb):
    nt = seq // tt
    has_vres = v_first is not None
    full = lambda b, t: (0, 0)
    wide = pl.BlockSpec((bb, tt, RWKV_DIM), lambda b, t: (b, t, 0))
    vec = pl.BlockSpec((1, RWKV_DIM), full)
    lora = pl.BlockSpec((128, RWKV_DIM), full)
    col_blk = lambda col, w: pl.BlockSpec((bb, tt, w), lambda b, t: (b, t, col // w))
    in_specs = [col_blk(COL_RKV, 3 * RWKV_DIM), col_blk(COL_WA, 128), col_blk(COL_GD, 256),
                col_blk(COL_AB, 128)]
    args = [proj, proj, proj, proj]
    if has_vres:
        in_specs.append(wide)
        args.append(v_first)
    in_specs += [pl.BlockSpec((1, 3 * RWKV_DIM), full), pl.BlockSpec((1, SM_W), full),
                 vec, lora, vec, lora, pl.BlockSpec((256, RWKV_DIM), full), vec, vec, vec, vec, vec]
    args += [lp["mu_rkv"], lp["mu_sm"], lp["w0"], lp["w_up"], lp["a0"], lp["a_up"], lp["g_up"],
             lp["k_k"], lp["k_a"], lp["r_k"], lp["ln_w"], lp["ln_b"]]
    y_shape = jax.ShapeDtypeStruct((batch, seq, RWKV_DIM), BF16)
    if has_vres:
        in_specs += [lora, vec]
        args += [lp["v_up"], lp["v0"]]
        out_specs, out_shape = wide, y_shape
    else:
        out_specs = [wide, wide]
        out_shape = [y_shape, jax.ShapeDtypeStruct((batch, seq, RWKV_DIM), F32)]
    return pl.pallas_call(
        functools.partial(_rwkv_mix_kernel, has_vres),
        grid=(batch // bb, nt),
        in_specs=in_specs,
        out_specs=out_specs,
        out_shape=out_shape,
        scratch_shapes=[pltpu.VMEM((bb * RWKV_DIM // LANES, LANES, LANES), F32),
                        pltpu.VMEM((bb, SUBLANES, 3 * RWKV_DIM), F32),
                        pltpu.VMEM((bb, SUBLANES, SM_W), F32)],
        compiler_params=pltpu.CompilerParams(
            dimension_semantics=("parallel", "arbitrary"), vmem_limit_bytes=VMEM_LIMIT),
        name="rwkv_mix",
    )(*args)


def _gdn_chunk_prep(x, tail, ab, cw, a_log, dt_bias, tri):
    x1, x2, x3 = _rows_back(x, tail, GDN_CONV - 1)
    y = _silu(x * cw[3:4, :] + x1 * cw[2:3, :] + x2 * cw[1:2, :] + x3 * cw[0:1, :])
    q, k = [], []
    for h in range(GDN_QK_HEADS):
        qh = y[:, h * GDN_HEAD:(h + 1) * GDN_HEAD]
        kh = y[:, GDN_QK_DIM + h * GDN_HEAD:GDN_QK_DIM + (h + 1) * GDN_HEAD]
        q.append(qh * (lax.rsqrt(jnp.sum(qh * qh, axis=-1, keepdims=True) + GDN_L2_EPS)
                       * (GDN_HEAD ** -0.5)))
        k.append(kh * lax.rsqrt(jnp.sum(kh * kh, axis=-1, keepdims=True) + GDN_L2_EPS))
    g = -jnp.exp(a_log) * _softplus(ab + dt_bias)
    beta = _sigmoid(ab)
    g_cum = _mm_exact_lhs(tri, g)
    bc = lambda a, j: jnp.broadcast_to(a[:, j:j + 1], (a.shape[0], GDN_HEAD))
    g_b = jnp.concatenate([bc(g_cum, j) for j in range(GDN_V_HEADS)], axis=1)
    beta_b = jnp.concatenate([bc(beta, GDN_V_HEADS + j) for j in range(GDN_V_HEADS)], axis=1)
    return q, k, y[:, 2 * GDN_QK_DIM:], g_b, beta_b


def _gdn_chunk_units(units):
    c = CHUNK
    n2 = 2 * c
    st = lambda x: jnp.concatenate([x[:, :GDN_HEAD], x[:, GDN_HEAD:]], axis=0)
    i, j = _iota2((n2, n2))
    incl = ((i // c) == (j // c)) & (i >= j)
    strict = i > j

    g_c = [st(u[3]) for u in units]
    b_c = [st(u[4]) for u in units]
    k2 = [jnp.concatenate([u[1], u[1]], axis=0) for u in units]
    q2 = [jnp.concatenate([u[0], u[0]], axis=0) for u in units]
    kb = [a * b for a, b in zip(k2, b_c)]
    kq = [_mm_nt(jnp.concatenate([a, b], axis=0), kk) for a, b, kk in zip(kb, q2, k2)]
    gamma = [jnp.exp(jnp.where(incl, g - g.T, -jnp.inf)) for g in g_c]
    l_neg = [jnp.where(strict, -(x[:n2] * gm), 0.0) for x, gm in zip(kq, gamma)]
    a_qk = [(x[n2:] * gm).astype(BF16) for x, gm in zip(kq, gamma)]
    t_inv = _unit_lower_inverse(l_neg)
    e_g = [jnp.exp(g) for g in g_c]
    uw = [_mm(t, jnp.concatenate([st(u[2]) * b, kbi * eg], axis=1))
          for t, u, b, kbi, eg in zip(t_inv, units, b_c, kb, e_g)]
    qg = [qq * eg for qq, eg in zip(q2, e_g)]
    hd = lambda e: slice(e * GDN_HEAD, (e + 1) * GDN_HEAD)
    rw = lambda e: slice(e * c, (e + 1) * c)
    wq = [[_mm(jnp.concatenate([x[rw(e), GDN_HEAD:], qe[rw(e)]], axis=0), u[5][:, hd(e)])
           for x, qe, u in zip(uw, qg, units)] for e in range(2)]
    v_new = [jnp.concatenate([x[rw(0), :GDN_HEAD] - w0[:c], x[rw(1), :GDN_HEAD] - w1[:c]], axis=0)
             for x, w0, w1 in zip(uw, wq[0], wq[1])]
    o_s = [jnp.concatenate([w0[c:], w1[c:]], axis=0) + _mm(a, vn)
           for w0, w1, a, vn in zip(wq[0], wq[1], a_qk, v_new)]
    k_g, decay = [], []
    for g, u in zip(g_c, units):
        g_last = (g[c - 1:c, :], g[n2 - 1:n2, :])
        k_g.append([u[1] * jnp.exp(g_last[e] - g[rw(e)]) for e in range(2)])
        decay.append((jnp.exp(g_last[0]), jnp.exp(g_last[1])))
    ds = [[_mm_tn(kg[e], vn[rw(e)]) for kg, vn in zip(k_g, v_new)] for e in range(2)]
    s_new = [jnp.concatenate([u[5][:, hd(0)] * dc[0] + d0, u[5][:, hd(1)] * dc[1] + d1], axis=1)
             for u, dc, d0, d1 in zip(units, decay, ds[0], ds[1])]
    outs = [jnp.concatenate([o[:c], o[c:]], axis=1) for o in o_s]
    return outs, s_new


def _gdn_mix_kernel(qk_ref, v_ref, z_ref, ab_ref, cw_ref, alog_ref, dtb_ref, nw_ref,
                    o_ref, s_ref, tail_ref):
    bb, tt = qk_ref.shape[0], qk_ref.shape[1]
    pw = 2 * GDN_HEAD

    @pl.when(pl.program_id(1) == 0)
    def _():
        s_ref[...] = jnp.zeros(s_ref.shape, F32)
        tail_ref[...] = jnp.zeros(tail_ref.shape, F32)

    ids = [(bi, h) for bi in range(bb) for h in range(GDN_QK_HEADS)]
    ti, tj = _iota2((CHUNK, CHUNK))
    tri = (ti >= tj).astype(BF16)

    def body(ci, carry):
        rows = pl.ds(pl.multiple_of(ci * CHUNK, CHUNK), CHUNK)
        rowp = []
        for bi in range(bb):
            x = jnp.concatenate([qk_ref[bi, rows, :], v_ref[bi, rows, :]], axis=1)
            rowp.append(_gdn_chunk_prep(x, tail_ref[bi], ab_ref[bi, rows, :],
                                        cw_ref[...], alog_ref[...], dtb_ref[...], tri))
            tail_ref[bi] = x[CHUNK - SUBLANES:, :]
        units = []
        for n, (bi, h) in enumerate(ids):
            q, k, v, g_b, beta_b = rowp[bi]
            cv = slice(h * pw, (h + 1) * pw)
            units.append((q[h], k[h], v[:, cv], g_b[:, cv], beta_b[:, cv], s_ref[n]))
        outs, s_new = _gdn_chunk_units(units)
        for n, (bi, h) in enumerate(ids):
            s_ref[n] = s_new[n]
            for e in range(2):
                cols = slice(h * pw + e * GDN_HEAD, h * pw + (e + 1) * GDN_HEAD)
                oh = outs[n][:, e * GDN_HEAD:(e + 1) * GDN_HEAD]
                oh = oh * lax.rsqrt(jnp.mean(oh * oh, axis=-1, keepdims=True) + GDN_NORM_EPS) * nw_ref[...]
                o_ref[bi, rows, cols] = (oh * _silu(z_ref[bi, rows, cols])).astype(BF16)
        return carry

    lax.fori_loop(0, tt // CHUNK, body, 0)


def _gdn_mix(proj, lp, batch, seq, tt, bb):
    nt = seq // tt
    full = lambda b, t: (0, 0)
    qk_w = 2 * GDN_QK_DIM
    wide = lambda col: pl.BlockSpec((bb, tt, GDN_V_DIM), lambda b, t: (b, t, col // GDN_V_DIM))
    return pl.pallas_call(
        _gdn_mix_kernel,
        grid=(batch // bb, nt),
        in_specs=[wide(COL_GQK), wide(COL_GV), wide(COL_GZ),
                  pl.BlockSpec((bb, tt, LANES), lambda b, t: (b, t, COL_AB // LANES)),
                  pl.BlockSpec((GDN_CONV, qk_w + GDN_V_DIM), full),
                  pl.BlockSpec((1, LANES), full),
                  pl.BlockSpec((1, LANES), full),
                  pl.BlockSpec((1, GDN_HEAD), full)],
        out_specs=wide(0),
        out_shape=jax.ShapeDtypeStruct((batch, seq, GDN_V_DIM), BF16),
        scratch_shapes=[pltpu.VMEM((bb * GDN_QK_HEADS, GDN_HEAD, 2 * GDN_HEAD), F32),
                        pltpu.VMEM((bb, SUBLANES, qk_w + GDN_V_DIM), F32)],
        compiler_params=pltpu.CompilerParams(
            dimension_semantics=("parallel", "arbitrary"), vmem_limit_bytes=VMEM_LIMIT),
        name="gdn_mix",
    )(proj, proj, proj, proj, lp["conv_w"], lp["a_log"], lp["dt_bias"], lp["norm_w"])


def _pad_cols(a, n):
    return jnp.pad(a, [(0, 0)] * (a.ndim - 1) + [(0, n - a.shape[-1])])


def _proj_layout(main, vd):
    n_small = RWKV_DECAY_LORA + RWKV_AAA_LORA + RWKV_GATE_LORA
    rkv = main[..., :3 * RWKV_DIM]
    small = main[..., 3 * RWKV_DIM:3 * RWKV_DIM + n_small]
    gdn = main[..., 3 * RWKV_DIM + n_small:]
    zeros = lambda n: jnp.zeros(main.shape[:-1] + (n,), main.dtype)
    ab_pad = COL_WA - (COL_GQK + gdn.shape[-1] + vd.shape[-1])
    return jnp.concatenate([rkv, gdn, vd, zeros(ab_pad), small, zeros(N_PROJ - COL_WA - n_small)], axis=-1)


def kernel(x, attn_norm_w, w_in, rwkv_mu, rwkv_w0, rwkv_w_up, rwkv_a0, rwkv_a_up, rwkv_g_up,
           rwkv_k_k, rwkv_k_a, rwkv_r_k, rwkv_ln_w, rwkv_ln_b, vres_down, vres_mu, vres_up, vres_v0,
           gdn_conv_w, gdn_A_log, gdn_dt_bias, gdn_norm_w, w_out, ffn_norm_w, ffn_w_gate, ffn_w_up,
           ffn_w_down, final_norm_w):
    batch, seq, d = x.shape
    depth = w_in.shape[0]
    m = batch * seq

    gdn_pad = jnp.zeros((depth, w_in.shape[2] - rwkv_mu.shape[1]), F32)
    vmu = jnp.concatenate([jnp.zeros((1, RWKV_MV_LORA), F32), vres_mu], axis=0)
    mu_all = _proj_layout(jnp.concatenate([rwkv_mu, gdn_pad], axis=1), vmu)
    row_place = lambda a, before, total: jnp.pad(
        a, ((0, 0), (before, total - before - a.shape[1]), (0, 0))).astype(BF16)
    w_up = row_place(rwkv_w_up, 0, 128)
    a_up = row_place(rwkv_a_up, RWKV_DECAY_LORA, 128)
    g_up = row_place(rwkv_g_up, 0, 256)
    v_up = row_place(vres_up, AB_VD, 128)
    a_log = _pad_cols(gdn_A_log, LANES)
    dt_bias = _pad_cols(gdn_dt_bias, LANES)
    r_k = rwkv_r_k.reshape(depth, RWKV_DIM)
    vd_w = jnp.concatenate([jnp.zeros((1, d, RWKV_MV_LORA), BF16), vres_down.astype(BF16)], axis=0)
    w_proj = _proj_layout(w_in.astype(BF16), vd_w)
    w_out_b = w_out.astype(BF16)
    wg_b = ffn_w_gate.astype(BF16)
    wu_b = ffn_w_up.astype(BF16)
    wd_b = ffn_w_down.astype(BF16)

    tm = min(1024, m)
    row = lambda a, l: a[l][None, :]

    xf = x.reshape(m, d)
    v_first = None
    for l in range(depth):
        proj = _norm_matmul(xf, row(attn_norm_w, l), w_proj, l, tm, 1664)
        mu_l = mu_all[l]
        lp = dict(mu_rkv=mu_l[None, COL_RKV:COL_RKV + 3 * RWKV_DIM],
                  mu_sm=jnp.concatenate([mu_l[COL_WA:COL_WA + 128], mu_l[COL_GD:COL_GD + 256],
                                         mu_l[COL_AB:COL_AB + 128]])[None, :],
                  w0=row(rwkv_w0, l), w_up=w_up[l], a0=row(rwkv_a0, l), a_up=a_up[l], g_up=g_up[l],
                  k_k=row(rwkv_k_k, l), k_a=row(rwkv_k_a, l), r_k=row(r_k, l),
                  ln_w=row(rwkv_ln_w, l), ln_b=row(rwkv_ln_b, l),
                  conv_w=gdn_conv_w[l], a_log=row(a_log, l), dt_bias=row(dt_bias, l),
                  norm_w=row(gdn_norm_w, l))
        if l > 0:
            lp["v_up"] = v_up[l - 1]
            lp["v0"] = row(vres_v0, l - 1)
        as3 = lambda a: a.reshape(batch, seq, a.shape[-1])
        bb_a = 2 if batch % 2 == 0 else 1
        if l == 0:
            y_a, v_first = _rwkv_mix(as3(proj), None, lp, batch, seq, min(128, seq), bb_a)
        else:
            y_a = _rwkv_mix(as3(proj), v_first, lp, batch, seq, min(128, seq), bb_a)
        y_a = y_a.reshape(m, RWKV_DIM)
        bb_b = 4 if batch % 4 == 0 else bb_a
        y_b = _gdn_mix(as3(proj), lp, batch, seq, min(128, seq), bb_b).reshape(m, GDN_V_DIM)
        xf = _matmul2_res(y_a, y_b, w_out_b, l, xf, tm, 1024)
        hmid = _norm_swiglu(xf, row(ffn_norm_w, l), wg_b, wu_b, l, tm, 512)
        if l + 1 < depth:
            xf = _matmul_res(hmid, wd_b, l, xf, min(512, m), 1024)
        else:
            xf = _matmul_res_norm(hmid, wd_b, l, xf, final_norm_w[None, :], min(256, m))
    return xf.reshape(batch, seq, d)
```

```python
import functools

import jax
import jax.numpy as jnp
from jax import lax
from jax.experimental import pallas as pl
from jax.experimental.pallas import tpu as pltpu

F32 = jnp.float32
BF16 = jnp.bfloat16

D_MODEL = 2048
RWKV_HEAD = 64
RWKV_DIM = 1024
RWKV_DECAY_LORA = 64
RWKV_AAA_LORA = 64
RWKV_MV_LORA = 32
RWKV_GATE_LORA = 160
RWKV_GN_EPS = RWKV_HEAD * 1e-5
RWKV_L2_EPS = 1e-12
GDN_HEAD = 128
GDN_V_DIM = 1024
GDN_V_HEADS = 8
GDN_QK_HEADS = 4
GDN_QK_DIM = 512
GDN_CONV = 4
GDN_L2_EPS = 1e-6
GDN_NORM_EPS = 1e-6
D_FF = 5632
NORM_EPS = 1e-5

LANES = 128
SUBLANES = 8
CHUNK = 64
PAIR = 2 * CHUNK
VMEM_LIMIT = 56 * 1024 * 1024

COL_RKV = 0
COL_GQK = 3 * RWKV_DIM
COL_GV = COL_GQK + 2 * GDN_QK_DIM
COL_GZ = COL_GV + GDN_V_DIM
COL_AB = COL_GZ + GDN_V_DIM
COL_WA = COL_AB + 128
COL_GD = COL_WA + 128
N_PROJ = COL_GD + 256
AB_VD = 16
SM_WA, SM_GD, SM_AB = 0, 128, 384
SM_W = 512


def _mm(a, b):
    return jnp.dot(a.astype(BF16), b.astype(BF16), preferred_element_type=F32)


def _mm_nt(a, b):
    return lax.dot_general(a.astype(BF16), b.astype(BF16), (((1,), (1,)), ((), ())),
                           preferred_element_type=F32)


def _mm_tn(a, b):
    return lax.dot_general(a.astype(BF16), b.astype(BF16), (((0,), (0,)), ((), ())),
                           preferred_element_type=F32)


def _split3(x):
    hi = x.astype(BF16)
    r1 = x - hi.astype(F32)
    mid = r1.astype(BF16)
    lo = (r1 - mid.astype(F32)).astype(BF16)
    return hi, mid, lo


def _mm_exact_lhs(a01, x):
    n = x.shape[1]
    hi, mid, lo = _split3(x)
    y = jnp.dot(a01, jnp.concatenate([hi, mid, lo], axis=1), preferred_element_type=F32)
    return y[:, :n] + y[:, n:2 * n] + y[:, 2 * n:]


def _sigmoid(x):
    return 1.0 / (1.0 + jnp.exp(-x))


def _softplus(x):
    return jnp.maximum(x, 0.0) + jnp.log(1.0 + jnp.exp(-jnp.abs(x)))


def _silu(x):
    return x * _sigmoid(x)


def _iota2(shape):
    return (lax.broadcasted_iota(jnp.int32, shape, 0), lax.broadcasted_iota(jnp.int32, shape, 1))


def _unit_lower_inverse(n_mats):
    sz = n_mats[0].shape[0]
    i, j = _iota2((sz, sz))
    eye = (i == j).astype(F32)
    blk8 = (i // 8) == (j // 8)
    d1 = [jnp.where(blk8, n, 0.0) for n in n_mats]
    d2 = [_mm(x, x) for x in d1]
    d4 = [_mm(x, x) for x in d2]
    p = [eye + x for x in d1]
    p = [x + _mm(x, y) for x, y in zip(p, d2)]
    p = [x + _mm(x, y) for x, y in zip(p, d4)]
    for s in (8, 16, 32):
        off = ((i // (2 * s)) == (j // (2 * s))) & ((i // s) == (j // s) + 1)
        q = [_mm(jnp.where(off, n, 0.0), x) for n, x in zip(n_mats, p)]
        p = [x + _mm(x, y) for x, y in zip(p, q)]
    return p


def _norm_matmul_kernel(x_ref, nw_ref, w_ref, o_ref, h_ref):
    @pl.when(pl.program_id(1) == 0)
    def _():
        x = x_ref[...]
        ms = jnp.mean(x * x, axis=-1, keepdims=True)
        h_ref[...] = (x * lax.rsqrt(ms + NORM_EPS) * nw_ref[...]).astype(BF16)

    o_ref[...] = jnp.dot(h_ref[...], w_ref[...], preferred_element_type=F32)


def _norm_matmul(x, nw, w, layer, tm, tn):
    m, d = x.shape
    n = w.shape[2]
    return pl.pallas_call(
        _norm_matmul_kernel,
        grid=(m // tm, n // tn),
        in_specs=[pl.BlockSpec((tm, d), lambda i, j: (i, 0)),
                  pl.BlockSpec((1, d), lambda i, j: (0, 0)),
                  pl.BlockSpec((None, d, tn), lambda i, j: (layer, 0, j))],
        out_specs=pl.BlockSpec((tm, tn), lambda i, j: (i, j)),
        out_shape=jax.ShapeDtypeStruct((m, n), F32),
        scratch_shapes=[pltpu.VMEM((tm, d), BF16)],
        compiler_params=pltpu.CompilerParams(
            dimension_semantics=("parallel", "arbitrary"), vmem_limit_bytes=VMEM_LIMIT),
        name="norm_in_proj",
    )(x, nw, w)


def _norm_swiglu_kernel(x_ref, nw_ref, wg_ref, wu_ref, o_ref, h_ref):
    @pl.when(pl.program_id(1) == 0)
    def _():
        x = x_ref[...]
        ms = jnp.mean(x * x, axis=-1, keepdims=True)
        h_ref[...] = (x * lax.rsqrt(ms + NORM_EPS) * nw_ref[...]).astype(BF16)

    h = h_ref[...]
    g = jnp.dot(h, wg_ref[...], preferred_element_type=F32)
    u = jnp.dot(h, wu_ref[...], preferred_element_type=F32)
    o_ref[...] = (_silu(g) * u).astype(BF16)


def _norm_swiglu(x, nw, wg, wu, layer, tm, tn):
    m, d = x.shape
    n = wg.shape[2]
    return pl.pallas_call(
        _norm_swiglu_kernel,
        grid=(m // tm, n // tn),
        in_specs=[pl.BlockSpec((tm, d), lambda i, j: (i, 0)),
                  pl.BlockSpec((1, d), lambda i, j: (0, 0)),
                  pl.BlockSpec((None, d, tn), lambda i, j: (layer, 0, j)),
                  pl.BlockSpec((None, d, tn), lambda i, j: (layer, 0, j))],
        out_specs=pl.BlockSpec((tm, tn), lambda i, j: (i, j)),
        out_shape=jax.ShapeDtypeStruct((m, n), BF16),
        scratch_shapes=[pltpu.VMEM((tm, d), BF16)],
        compiler_params=pltpu.CompilerParams(
            dimension_semantics=("parallel", "arbitrary"), vmem_limit_bytes=VMEM_LIMIT),
        name="norm_swiglu",
    )(x, nw, wg, wu)


def _matmul_res_kernel(a_ref, w_ref, r_ref, o_ref):
    o_ref[...] = r_ref[...] + jnp.dot(a_ref[...], w_ref[...], preferred_element_type=F32)


def _matmul_res(a, w, layer, res, tm, tn):
    m, k = a.shape
    n = w.shape[2]
    return pl.pallas_call(
        _matmul_res_kernel,
        grid=(n // tn, m // tm),
        in_specs=[pl.BlockSpec((tm, k), lambda j, i: (i, 0)),
                  pl.BlockSpec((None, k, tn), lambda j, i: (layer, 0, j)),
                  pl.BlockSpec((tm, tn), lambda j, i: (i, j))],
        out_specs=pl.BlockSpec((tm, tn), lambda j, i: (i, j)),
        out_shape=jax.ShapeDtypeStruct((m, n), F32),
        compiler_params=pltpu.CompilerParams(
            dimension_semantics=("parallel", "arbitrary"), vmem_limit_bytes=VMEM_LIMIT),
        name="matmul_residual",
    )(a, w, res)


def _matmul_res_norm_kernel(a_ref, w_ref, r_ref, nw_ref, o_ref):
    x = r_ref[...] + jnp.dot(a_ref[...], w_ref[...], preferred_element_type=F32)
    ms = jnp.mean(x * x, axis=-1, keepdims=True)
    o_ref[...] = x * lax.rsqrt(ms + NORM_EPS) * nw_ref[...]


def _matmul_res_norm(a, w, layer, res, nw, tm):
    m, k = a.shape
    n = w.shape[2]
    return pl.pallas_call(
        _matmul_res_norm_kernel,
        grid=(m // tm,),
        in_specs=[pl.BlockSpec((tm, k), lambda i: (i, 0)),
                  pl.BlockSpec((None, k, n), lambda i: (layer, 0, 0), pipeline_mode=pl.Buffered(1)),
                  pl.BlockSpec((tm, n), lambda i: (i, 0)),
                  pl.BlockSpec((1, n), lambda i: (0, 0))],
        out_specs=pl.BlockSpec((tm, n), lambda i: (i, 0)),
        out_shape=jax.ShapeDtypeStruct((m, n), F32),
        compiler_params=pltpu.CompilerParams(
            dimension_semantics=("arbitrary",), vmem_limit_bytes=VMEM_LIMIT),
        name="matmul_residual_final_norm",
    )(a, w, res, nw)


def _matmul2_res_kernel(a_ref, b_ref, wa_ref, wb_ref, r_ref, o_ref):
    acc = jnp.dot(a_ref[...], wa_ref[...], preferred_element_type=F32)
    acc = acc + jnp.dot(b_ref[...], wb_ref[...], preferred_element_type=F32)
    o_ref[...] = r_ref[...] + acc


def _matmul2_res(a, b, w, layer, res, tm, tn):
    m, k = a.shape
    n = w.shape[2]
    return pl.pallas_call(
        _matmul2_res_kernel,
        grid=(n // tn, m // tm),
        in_specs=[pl.BlockSpec((tm, k), lambda j, i: (i, 0)),
                  pl.BlockSpec((tm, k), lambda j, i: (i, 0)),
                  pl.BlockSpec((None, k, tn), lambda j, i: (layer, 0, j)),
                  pl.BlockSpec((None, k, tn), lambda j, i: (layer, 1, j)),
                  pl.BlockSpec((tm, tn), lambda j, i: (i, j))],
        out_specs=pl.BlockSpec((tm, tn), lambda j, i: (i, j)),
        out_shape=jax.ShapeDtypeStruct((m, n), F32),
        compiler_params=pltpu.CompilerParams(
            dimension_semantics=("parallel", "arbitrary"), vmem_limit_bytes=VMEM_LIMIT),
        name="out_proj_residual",
    )(a, b, w, w, res)


def _rows_back(x, tail, n_back):
    c = x.shape[0]
    ext = jnp.concatenate([tail, x], axis=0)
    return [pltpu.roll(ext, s, 0)[SUBLANES:SUBLANES + c, :] for s in range(1, n_back + 1)]


def _rwkv_chunk_units(units, ones_bd, tri, m0):
    c = CHUNK
    n2 = 2 * c
    ss = [_mm(u[4] * u[4], ones_bd) for u in units]
    log_w = [_mm_exact_lhs(tri, u[3]) for u in units]
    bon = [_mm(u[0] * u[1] * u[6], ones_bd) for u in units]

    def bd(x):
        return jnp.concatenate([jnp.where(m0, x, 0.0), jnp.where(m0, 0.0, x)], axis=0).astype(BF16)

    lhs, rhs, bk, v_bd, w_last = [], [], [], [], []
    for (r, k, v, lw, kraw, eta, _, _), s2, lg in zip(units, ss, log_w):
        kk = kraw * lax.rsqrt(s2 + RWKV_L2_EPS)
        beta = kk * eta
        lw_last = lg[c - 1:c, :]
        w_inv = jnp.exp(-lg)
        w_end = jnp.exp(lw_last - lg)
        lhs.append(jnp.concatenate([bd(-kk * jnp.exp(lg - lw)), bd(r * jnp.exp(lg))], axis=0))
        rhs.append(jnp.concatenate([bd(beta * w_inv), bd(k * w_inv)], axis=0))
        bk.append(jnp.concatenate([bd(beta * w_end), bd(k * w_end)], axis=0))
        v_bd.append(bd(v))
        w_last.append(jnp.exp(lw_last))

    sc = [_mm_nt(a, b) for a, b in zip(lhs, rhs)]
    xr = [_mm_nt(a, u[7]) for a, u in zip(lhs, units)]
    i, j = _iota2((n2, n2))
    strict = i > j
    incl = i >= j
    a_ab = [jnp.where(strict, x[:n2, :n2], 0.0) for x in sc]
    av = [_mm(jnp.where(strict, x[:n2, n2:], 0.0), vb) for x, vb in zip(sc, v_bd)]
    a_r = [jnp.where(jnp.concatenate([incl, incl], axis=1), x[n2:, :], 0.0).astype(BF16) for x in sc]
    t_inv = _unit_lower_inverse(a_ab)
    u_mat = [_mm(t, x[:n2] + a) for t, x, a in zip(t_inv, xr, av)]
    uv = [jnp.concatenate([x.astype(BF16), vb], axis=0) for x, vb in zip(u_mat, v_bd)]
    y_bd = [x[n2:] + _mm(a, z) for x, a, z in zip(xr, a_r, uv)]
    s_new = [u[7] * wl + _mm_tn(z, b) for u, wl, z, b in zip(units, w_last, uv, bk)]
    y = [x[:c] + x[c:] for x in y_bd]
    return y, bon, s_new


def _rwkv_mix_kernel(has_vres, *refs):
    if has_vres:
        (rkv_ref, wa_ref, gd_ref, ab_ref, vf_ref, mu_rkv_ref, mu_sm_ref, w0_ref, wup_ref, a0_ref,
         aup_ref, gup_ref, kk_ref, ka_ref, rk_ref, lnw_ref, lnb_ref, vup_ref, v0_ref,
         o_ref, s_ref, tail_rkv, tail_sm) = refs
        vout_ref = None
    else:
        (rkv_ref, wa_ref, gd_ref, ab_ref, mu_rkv_ref, mu_sm_ref, w0_ref, wup_ref, a0_ref,
         aup_ref, gup_ref, kk_ref, ka_ref, rk_ref, lnw_ref, lnb_ref,
         o_ref, vout_ref, s_ref, tail_rkv, tail_sm) = refs
    bb, tt = rkv_ref.shape[0], rkv_ref.shape[1]
    n_pairs = RWKV_DIM // LANES

    @pl.when(pl.program_id(1) == 0)
    def _():
        s_ref[...] = jnp.zeros(s_ref.shape, F32)
        tail_rkv[...] = jnp.zeros(tail_rkv.shape, F32)
        tail_sm[...] = jnp.zeros(tail_sm.shape, F32)

    li, lj = _iota2((LANES, LANES))
    ones_bd = ((li // RWKV_HEAD) == (lj // RWKV_HEAD)).astype(BF16)
    ti, tj = _iota2((CHUNK, CHUNK))
    tri = (ti >= tj).astype(BF16)
    m0 = lax.broadcasted_iota(jnp.int32, (CHUNK, LANES), 1) < RWKV_HEAD
    inv_n = 1.0 / RWKV_HEAD
    ids = [(bi, slice(p * LANES, (p + 1) * LANES)) for bi in range(bb) for p in range(n_pairs)]

    def body(ci, carry):
        rows = pl.ds(pl.multiple_of(ci * CHUNK, CHUNK), CHUNK)
        sh, shs = [], []
        for bi in range(bb):
            p = rkv_ref[bi, rows, :]
            (prev,) = _rows_back(p, tail_rkv[bi], 1)
            tail_rkv[bi] = p[CHUNK - SUBLANES:, :]
            sh.append(p + (prev - p) * mu_rkv_ref[...])
            ps = jnp.concatenate([wa_ref[bi, rows, :], gd_ref[bi, rows, :], ab_ref[bi, rows, :]], axis=1)
            (prev_s,) = _rows_back(ps, tail_sm[bi], 1)
            tail_sm[bi] = ps[CHUNK - SUBLANES:, :]
            shs.append(ps + (prev_s - ps) * mu_sm_ref[...])
        w_lo = [_mm(jnp.tanh(x[:, SM_WA:SM_WA + 128]), wup_ref[...]) for x in shs]
        a_lo = [_mm(x[:, SM_WA:SM_WA + 128], aup_ref[...]) for x in shs]
        g_all = [_mm(_sigmoid(x[:, SM_GD:SM_GD + 256]), gup_ref[...]) for x in shs]
        if has_vres:
            v_lo = [_mm(x[:, SM_AB:SM_AB + 128], vup_ref[...]) for x in shs]
        r_all, k_all, v_all, lw_all, kr_all, eta_all = [], [], [], [], [], []
        for bi in range(bb):
            w = -_softplus(-(w0_ref[...] + w_lo[bi])) - 0.5
            lw_all.append(-jnp.exp(w))
            eta = _sigmoid(a0_ref[...] + a_lo[bi])
            k = sh[bi][:, RWKV_DIM:2 * RWKV_DIM]
            v = sh[bi][:, 2 * RWKV_DIM:3 * RWKV_DIM]
            if has_vres:
                v = v + (vf_ref[bi, rows, :] - v) * _sigmoid(v0_ref[...] + v_lo[bi])
            else:
                vout_ref[bi, rows, :] = v
            r_all.append(sh[bi][:, 0:RWKV_DIM])
            v_all.append(v)
            kr_all.append(k * kk_ref[...])
            k_all.append(k * (1.0 + (eta - 1.0) * ka_ref[...]))
            eta_all.append(eta)
        units = [tuple(a[bi][:, cs] for a in (r_all, k_all, v_all, lw_all, kr_all, eta_all))
                 + (rk_ref[:, cs], s_ref[n]) for n, (bi, cs) in enumerate(ids)]
        y, bon, s_new = _rwkv_chunk_units(units, ones_bd, tri, m0)
        mean = [_mm(x, ones_bd) * inv_n for x in y]
        d = [x - mu for x, mu in zip(y, mean)]
        var = [_mm(x * x, ones_bd) * inv_n for x in d]
        for n, (bi, cs) in enumerate(ids):
            yn = d[n] * lax.rsqrt(var[n] + RWKV_GN_EPS) * lnw_ref[:, cs] + lnb_ref[:, cs]
            o_ref[bi, rows, cs] = ((yn + bon[n] * units[n][2]) * g_all[bi][:, cs]).astype(BF16)
            s_ref[n] = s_new[n]
        return carry

    lax.fori_loop(0, tt // CHUNK, body, 0, unroll=True)


def _rwkv_mix(proj, v_first, lp, batch, seq, tt, bb):
    nt = seq // tt
    has_vres = v_first is not None
    full = lambda b, t: (0, 0)
    wide = pl.BlockSpec((bb, tt, RWKV_DIM), lambda b, t: (b, t, 0))
    vec = pl.BlockSpec((1, RWKV_DIM), full)
    lora = pl.BlockSpec((128, RWKV_DIM), full)
    col_blk = lambda col, w: pl.BlockSpec((bb, tt, w), lambda b, t: (b, t, col // w))
    in_specs = [col_blk(COL_RKV, 3 * RWKV_DIM), col_blk(COL_WA, 128), col_blk(COL_GD, 256),
                col_blk(COL_AB, 128)]
    args = [proj, proj, proj, proj]
    if has_vres:
        in_specs.append(wide)
        args.append(v_first)
    in_specs += [pl.BlockSpec((1, 3 * RWKV_DIM), full), pl.BlockSpec((1, SM_W), full),
                 vec, lora, vec, lora, pl.BlockSpec((256, RWKV_DIM), full), vec, vec, vec, vec, vec]
    args += [lp["mu_rkv"], lp["mu_sm"], lp["w0"], lp["w_up"], lp["a0"], lp["a_up"], lp["g_up"],
             lp["k_k"], lp["k_a"], lp["r_k"], lp["ln_w"], lp["ln_b"]]
    y_shape = jax.ShapeDtypeStruct((batch, seq, RWKV_DIM), BF16)
    if has_vres:
        in_specs += [lora, vec]
        args += [lp["v_up"], lp["v0"]]
        out_specs, out_shape = wide, y_shape
    else:
        out_specs = [wide, wide]
        out_shape = [y_shape, jax.ShapeDtypeStruct((batch, seq, RWKV_DIM), F32)]
    return pl.pallas_call(
        functools.partial(_rwkv_mix_kernel, has_vres),
        grid=(batch // bb, nt),
        in_specs=in_specs,
        out_specs=out_specs,
        out_shape=out_shape,
        scratch_shapes=[pltpu.VMEM((bb * RWKV_DIM // LANES, LANES, LANES), F32),
                        pltpu.VMEM((bb, SUBLANES, 3 * RWKV_DIM), F32),
                        pltpu.VMEM((bb, SUBLANES, SM_W), F32)],
        compiler_params=pltpu.CompilerParams(
            dimension_semantics=("parallel", "arbitrary"), vmem_limit_bytes=VMEM_LIMIT),
        name="rwkv_mix",
    )(*args)


def _gdn_chunk_prep(x, tail, ab, cw, a_log, dt_bias, tri):
    x1, x2, x3 = _rows_back(x, tail, GDN_CONV - 1)
    y = _silu(x * cw[3:4, :] + x1 * cw[2:3, :] + x2 * cw[1:2, :] + x3 * cw[0:1, :])
    q, k = [], []
    for h in range(GDN_QK_HEADS):
        qh = y[:, h * GDN_HEAD:(h + 1) * GDN_HEAD]
        kh = y[:, GDN_QK_DIM + h * GDN_HEAD:GDN_QK_DIM + (h + 1) * GDN_HEAD]
        q.append(qh * (lax.rsqrt(jnp.sum(qh * qh, axis=-1, keepdims=True) + GDN_L2_EPS)
                       * (GDN_HEAD ** -0.5)))
        k.append(kh * lax.rsqrt(jnp.sum(kh * kh, axis=-1, keepdims=True) + GDN_L2_EPS))
    g = -jnp.exp(a_log) * _softplus(ab + dt_bias)
    beta = _sigmoid(ab)
    g_cum = _mm_exact_lhs(tri, g)
    bc = lambda a, j: jnp.broadcast_to(a[:, j:j + 1], (a.shape[0], GDN_HEAD))
    g_b = jnp.concatenate([bc(g_cum, j) for j in range(GDN_V_HEADS)], axis=1)
    beta_b = jnp.concatenate([bc(beta, GDN_V_HEADS + j) for j in range(GDN_V_HEADS)], axis=1)
    return q, k, y[:, 2 * GDN_QK_DIM:], g_b, beta_b


def _gdn_chunk_units(units):
    c = CHUNK
    n2 = 2 * c
    st = lambda x: jnp.concatenate([x[:, :GDN_HEAD], x[:, GDN_HEAD:]], axis=0)
    i, j = _iota2((n2, n2))
    incl = ((i // c) == (j // c)) & (i >= j)
    strict = i > j

    g_c = [st(u[3]) for u in units]
    b_c = [st(u[4]) for u in units]
    k2 = [jnp.concatenate([u[1], u[1]], axis=0) for u in units]
    q2 = [jnp.concatenate([u[0], u[0]], axis=0) for u in units]
    kb = [a * b for a, b in zip(k2, b_c)]
    kq = [_mm_nt(jnp.concatenate([a, b], axis=0), kk) for a, b, kk in zip(kb, q2, k2)]
    gamma = [jnp.exp(jnp.where(incl, g - g.T, -jnp.inf)) for g in g_c]
    l_neg = [jnp.where(strict, -(x[:n2] * gm), 0.0) for x, gm in zip(kq, gamma)]
    a_qk = [(x[n2:] * gm).astype(BF16) for x, gm in zip(kq, gamma)]
    t_inv = _unit_lower_inverse(l_neg)
    e_g = [jnp.exp(g) for g in g_c]
    uw = [_mm(t, jnp.concatenate([st(u[2]) * b, kbi * eg], axis=1))
          for t, u, b, kbi, eg in zip(t_inv, units, b_c, kb, e_g)]
    qg = [qq * eg for qq, eg in zip(q2, e_g)]
    hd = lambda e: slice(e * GDN_HEAD, (e + 1) * GDN_HEAD)
    rw = lambda e: slice(e * c, (e + 1) * c)
    wq = [[_mm(jnp.concatenate([x[rw(e), GDN_HEAD:], qe[rw(e)]], axis=0), u[5][:, hd(e)])
           for x, qe, u in zip(uw, qg, units)] for e in range(2)]
    v_new = [jnp.concatenate([x[rw(0), :GDN_HEAD] - w0[:c], x[rw(1), :GDN_HEAD] - w1[:c]], axis=0)
             for x, w0, w1 in zip(uw, wq[0], wq[1])]
    o_s = [jnp.concatenate([w0[c:], w1[c:]], axis=0) + _mm(a, vn)
           for w0, w1, a, vn in zip(wq[0], wq[1], a_qk, v_new)]
    k_g, decay = [], []
    for g, u in zip(g_c, units):
        g_last = (g[c - 1:c, :], g[n2 - 1:n2, :])
        k_g.append([u[1] * jnp.exp(g_last[e] - g[rw(e)]) for e in range(2)])
        decay.append((jnp.exp(g_last[0]), jnp.exp(g_last[1])))
    ds = [[_mm_tn(kg[e], vn[rw(e)]) for kg, vn in zip(k_g, v_new)] for e in range(2)]
    s_new = [jnp.concatenate([u[5][:, hd(0)] * dc[0] + d0, u[5][:, hd(1)] * dc[1] + d1], axis=1)
             for u, dc, d0, d1 in zip(units, decay, ds[0], ds[1])]
    outs = [jnp.concatenate([o[:c], o[c:]], axis=1) for o in o_s]
    return outs, s_new


def _gdn_mix_kernel(qk_ref, v_ref, z_ref, ab_ref, cw_ref, alog_ref, dtb_ref, nw_ref,
                    o_ref, s_ref, tail_ref):
    bb, tt = qk_ref.shape[0], qk_ref.shape[1]
    pw = 2 * GDN_HEAD

    @pl.when(pl.program_id(1) == 0)
    def _():
        s_ref[...] = jnp.zeros(s_ref.shape, F32)
        tail_ref[...] = jnp.zeros(tail_ref.shape, F32)

    ids = [(bi, h) for bi in range(bb) for h in range(GDN_QK_HEADS)]
    ti, tj = _iota2((CHUNK, CHUNK))
    tri = (ti >= tj).astype(BF16)

    def body(ci, carry):
        rows = pl.ds(pl.multiple_of(ci * CHUNK, CHUNK), CHUNK)
        rowp = []
        for bi in range(bb):
            x = jnp.concatenate([qk_ref[bi, rows, :], v_ref[bi, rows, :]], axis=1)
            rowp.append(_gdn_chunk_prep(x, tail_ref[bi], ab_ref[bi, rows, :],
                                        cw_ref[...], alog_ref[...], dtb_ref[...], tri))
            tail_ref[bi] = x[CHUNK - SUBLANES:, :]
        units = []
        for n, (bi, h) in enumerate(ids):
            q, k, v, g_b, beta_b = rowp[bi]
            cv = slice(h * pw, (h + 1) * pw)
            units.append((q[h], k[h], v[:, cv], g_b[:, cv], beta_b[:, cv], s_ref[n]))
        outs, s_new = _gdn_chunk_units(units)
        for n, (bi, h) in enumerate(ids):
            s_ref[n] = s_new[n]
            for e in range(2):
                cols = slice(h * pw + e * GDN_HEAD, h * pw + (e + 1) * GDN_HEAD)
                oh = outs[n][:, e * GDN_HEAD:(e + 1) * GDN_HEAD]
                oh = oh * lax.rsqrt(jnp.mean(oh * oh, axis=-1, keepdims=True) + GDN_NORM_EPS) * nw_ref[...]
                o_ref[bi, rows, cols] = (oh * _silu(z_ref[bi, rows, cols])).astype(BF16)
        return carry

    lax.fori_loop(0, tt // CHUNK, body, 0, unroll=True)


def _gdn_mix(proj, lp, batch, seq, tt, bb):
    nt = seq // tt
    full = lambda b, t: (0, 0)
    qk_w = 2 * GDN_QK_DIM
    wide = lambda col: pl.BlockSpec((bb, tt, GDN_V_DIM), lambda b, t: (b, t, col // GDN_V_DIM))
    return pl.pallas_call(
        _gdn_mix_kernel,
        grid=(batch // bb, nt),
        in_specs=[wide(COL_GQK), wide(COL_GV), wide(COL_GZ),
                  pl.BlockSpec((bb, tt, LANES), lambda b, t: (b, t, COL_AB // LANES)),
                  pl.BlockSpec((GDN_CONV, qk_w + GDN_V_DIM), full),
                  pl.BlockSpec((1, LANES), full),
                  pl.BlockSpec((1, LANES), full),
                  pl.BlockSpec((1, GDN_HEAD), full)],
        out_specs=wide(0),
        out_shape=jax.ShapeDtypeStruct((batch, seq, GDN_V_DIM), BF16),
        scratch_shapes=[pltpu.VMEM((bb * GDN_QK_HEADS, GDN_HEAD, 2 * GDN_HEAD), F32),
                        pltpu.VMEM((bb, SUBLANES, qk_w + GDN_V_DIM), F32)],
        compiler_params=pltpu.CompilerParams(
            dimension_semantics=("parallel", "arbitrary"), vmem_limit_bytes=VMEM_LIMIT),
        name="gdn_mix",
    )(proj, proj, proj, proj, lp["conv_w"], lp["a_log"], lp["dt_bias"], lp["norm_w"])


def _pad_cols(a, n):
    return jnp.pad(a, [(0, 0)] * (a.ndim - 1) + [(0, n - a.shape[-1])])


def _proj_layout(main, vd):
    n_small = RWKV_DECAY_LORA + RWKV_AAA_LORA + RWKV_GATE_LORA
    rkv = main[..., :3 * RWKV_DIM]
    small = main[..., 3 * RWKV_DIM:3 * RWKV_DIM + n_small]
    gdn = main[..., 3 * RWKV_DIM + n_small:]
    zeros = lambda n: jnp.zeros(main.shape[:-1] + (n,), main.dtype)
    ab_pad = COL_WA - (COL_GQK + gdn.shape[-1] + vd.shape[-1])
    return jnp.concatenate([rkv, gdn, vd, zeros(ab_pad), small, zeros(N_PROJ - COL_WA - n_small)], axis=-1)


def kernel(x, attn_norm_w, w_in, rwkv_mu, rwkv_w0, rwkv_w_up, rwkv_a0, rwkv_a_up, rwkv_g_up,
           rwkv_k_k, rwkv_k_a, rwkv_r_k, rwkv_ln_w, rwkv_ln_b, vres_down, vres_mu, vres_up, vres_v0,
           gdn_conv_w, gdn_A_log, gdn_dt_bias, gdn_norm_w, w_out, ffn_norm_w, ffn_w_gate, ffn_w_up,
           ffn_w_down, final_norm_w):
    batch, seq, d = x.shape
    depth = w_in.shape[0]
    m = batch * seq

    gdn_pad = jnp.zeros((depth, w_in.shape[2] - rwkv_mu.shape[1]), F32)
    vmu = jnp.concatenate([jnp.zeros((1, RWKV_MV_LORA), F32), vres_mu], axis=0)
    mu_all = _proj_layout(jnp.concatenate([rwkv_mu, gdn_pad], axis=1), vmu)
    row_place = lambda a, before, total: jnp.pad(
        a, ((0, 0), (before, total - before - a.shape[1]), (0, 0))).astype(BF16)
    w_up = row_place(rwkv_w_up, 0, 128)
    a_up = row_place(rwkv_a_up, RWKV_DECAY_LORA, 128)
    g_up = row_place(rwkv_g_up, 0, 256)
    v_up = row_place(vres_up, AB_VD, 128)
    a_log = _pad_cols(gdn_A_log, LANES)
    dt_bias = _pad_cols(gdn_dt_bias, LANES)
    r_k = rwkv_r_k.reshape(depth, RWKV_DIM)
    vd_w = jnp.concatenate([jnp.zeros((1, d, RWKV_MV_LORA), BF16), vres_down.astype(BF16)], axis=0)
    w_proj = _proj_layout(w_in.astype(BF16), vd_w)
    w_out_b = w_out.astype(BF16)
    wg_b = ffn_w_gate.astype(BF16)
    wu_b = ffn_w_up.astype(BF16)
    wd_b = ffn_w_down.astype(BF16)

    tm = min(1024, m)
    row = lambda a, l: a[l][None, :]

    xf = x.reshape(m, d)
    v_first = None
    for l in range(depth):
        proj = _norm_matmul(xf, row(attn_norm_w, l), w_proj, l, tm, 1664)
        mu_l = mu_all[l]
        lp = dict(mu_rkv=mu_l[None, COL_RKV:COL_RKV + 3 * RWKV_DIM],
                  mu_sm=jnp.concatenate([mu_l[COL_WA:COL_WA + 128], mu_l[COL_GD:COL_GD + 256],
                                         mu_l[COL_AB:COL_AB + 128]])[None, :],
                  w0=row(rwkv_w0, l), w_up=w_up[l], a0=row(rwkv_a0, l), a_up=a_up[l], g_up=g_up[l],
                  k_k=row(rwkv_k_k, l), k_a=row(rwkv_k_a, l), r_k=row(r_k, l),
                  ln_w=row(rwkv_ln_w, l), ln_b=row(rwkv_ln_b, l),
                  conv_w=gdn_conv_w[l], a_log=row(a_log, l), dt_bias=row(dt_bias, l),
                  norm_w=row(gdn_norm_w, l))
        if l > 0:
            lp["v_up"] = v_up[l - 1]
            lp["v0"] = row(vres_v0, l - 1)
        as3 = lambda a: a.reshape(batch, seq, a.shape[-1])
        bb_a = 2 if batch % 2 == 0 else 1
        if l == 0:
            y_a, v_first = _rwkv_mix(as3(proj), None, lp, batch, seq, min(128, seq), bb_a)
        else:
            y_a = _rwkv_mix(as3(proj), v_first, lp, batch, seq, min(128, seq), bb_a)
        y_a = y_a.reshape(m, RWKV_DIM)
        bb_b = 4 if batch % 4 == 0 else bb_a
        y_b = _gdn_mix(as3(proj), lp, batch, seq, min(128, seq), bb_b).reshape(m, GDN_V_DIM)
        xf = _matmul2_res(y_a, y_b, w_out_b, l, xf, tm, 1024)
        hmid = _norm_swiglu(xf, row(ffn_norm_w, l), wg_b, wu_b, l, tm, 512)
        if l + 1 < depth:
            xf = _matmul_res(hmid, wd_b, l, xf, min(512, m), 1024)
        else:
            xf = _matmul_res_norm(hmid, wd_b, l, xf, final_norm_w[None, :], min(256, m))
    return xf.reshape(batch, seq, d)
```

```python
import functools

import jax
import jax.numpy as jnp
from jax import lax
from jax.experimental import pallas as pl
from jax.experimental.pallas import tpu as pltpu

F32 = jnp.float32
BF16 = jnp.bfloat16

D_MODEL = 2048
RWKV_HEAD = 64
RWKV_DIM = 1024
RWKV_DECAY_LORA = 64
RWKV_AAA_LORA = 64
RWKV_MV_LORA = 32
RWKV_GATE_LORA = 160
RWKV_GN_EPS = RWKV_HEAD * 1e-5
RWKV_L2_EPS = 1e-12
GDN_HEAD = 128
GDN_V_DIM = 1024
GDN_V_HEADS = 8
GDN_QK_HEADS = 4
GDN_QK_DIM = 512
GDN_CONV = 4
GDN_L2_EPS = 1e-6
GDN_NORM_EPS = 1e-6
D_FF = 5632
NORM_EPS = 1e-5

LANES = 128
SUBLANES = 8
CHUNK = 64
PAIR = 2 * CHUNK
VMEM_LIMIT = 56 * 1024 * 1024

COL_RKV = 0
COL_GQK = 3 * RWKV_DIM
COL_GV = COL_GQK + 2 * GDN_QK_DIM
COL_GZ = COL_GV + GDN_V_DIM
COL_AB = COL_GZ + GDN_V_DIM
COL_WA = COL_AB + 128
COL_GD = COL_WA + 128
N_PROJ = COL_GD + 256
AB_VD = 16
SM_WA, SM_GD, SM_AB = 0, 128, 384
SM_W = 512


def _mm(a, b):
    return jnp.dot(a.astype(BF16), b.astype(BF16), preferred_element_type=F32)


def _mm_nt(a, b):
    return lax.dot_general(a.astype(BF16), b.astype(BF16), (((1,), (1,)), ((), ())),
                           preferred_element_type=F32)


def _mm_tn(a, b):
    return lax.dot_general(a.astype(BF16), b.astype(BF16), (((0,), (0,)), ((), ())),
                           preferred_element_type=F32)


def _split3(x):
    hi = x.astype(BF16)
    r1 = x - hi.astype(F32)
    mid = r1.astype(BF16)
    lo = (r1 - mid.astype(F32)).astype(BF16)
    return hi, mid, lo


def _mm_exact_lhs(a01, x):
    n = x.shape[1]
    hi, mid, lo = _split3(x)
    y = jnp.dot(a01, jnp.concatenate([hi, mid, lo], axis=1), preferred_element_type=F32)
    return y[:, :n] + y[:, n:2 * n] + y[:, 2 * n:]


def _sigmoid(x):
    return 1.0 / (1.0 + jnp.exp(-x))


def _softplus(x):
    return jnp.maximum(x, 0.0) + jnp.log(1.0 + jnp.exp(-jnp.abs(x)))


def _silu(x):
    return x * _sigmoid(x)


def _iota2(shape):
    return (lax.broadcasted_iota(jnp.int32, shape, 0), lax.broadcasted_iota(jnp.int32, shape, 1))


def _unit_lower_inverse(n_mats):
    sz = n_mats[0].shape[0]
    i, j = _iota2((sz, sz))
    eye = (i == j).astype(F32)
    blk8 = (i // 8) == (j // 8)
    d1 = [jnp.where(blk8, n, 0.0) for n in n_mats]
    d2 = [_mm(x, x) for x in d1]
    d4 = [_mm(x, x) for x in d2]
    p = [eye + x for x in d1]
    p = [x + _mm(x, y) for x, y in zip(p, d2)]
    p = [x + _mm(x, y) for x, y in zip(p, d4)]
    for s in (8, 16, 32):
        off = ((i // (2 * s)) == (j // (2 * s))) & ((i // s) == (j // s) + 1)
        q = [_mm(jnp.where(off, n, 0.0), x) for n, x in zip(n_mats, p)]
        p = [x + _mm(x, y) for x, y in zip(p, q)]
    return p


def _norm_matmul_kernel(x_ref, nw_ref, w_ref, o_ref, h_ref):
    @pl.when(pl.program_id(1) == 0)
    def _():
        x = x_ref[...]
        ms = jnp.mean(x * x, axis=-1, keepdims=True)
        h_ref[...] = (x * lax.rsqrt(ms + NORM_EPS) * nw_ref[...]).astype(BF16)

    o_ref[...] = jnp.dot(h_ref[...], w_ref[...], preferred_element_type=F32)


def _norm_matmul(x, nw, w, layer, tm, tn):
    m, d = x.shape
    n = w.shape[2]
    return pl.pallas_call(
        _norm_matmul_kernel,
        grid=(m // tm, n // tn),
        in_specs=[pl.BlockSpec((tm, d), lambda i, j: (i, 0)),
                  pl.BlockSpec((1, d), lambda i, j: (0, 0)),
                  pl.BlockSpec((None, d, tn), lambda i, j: (layer, 0, j))],
        out_specs=pl.BlockSpec((tm, tn), lambda i, j: (i, j)),
        out_shape=jax.ShapeDtypeStruct((m, n), F32),
        scratch_shapes=[pltpu.VMEM((tm, d), BF16)],
        compiler_params=pltpu.CompilerParams(
            dimension_semantics=("parallel", "arbitrary"), vmem_limit_bytes=VMEM_LIMIT),
        name="norm_in_proj",
    )(x, nw, w)


def _norm_swiglu_kernel(x_ref, nw_ref, wg_ref, wu_ref, o_ref, h_ref):
    @pl.when(pl.program_id(1) == 0)
    def _():
        x = x_ref[...]
        ms = jnp.mean(x * x, axis=-1, keepdims=True)
        h_ref[...] = (x * lax.rsqrt(ms + NORM_EPS) * nw_ref[...]).astype(BF16)

    h = h_ref[...]
    g = jnp.dot(h, wg_ref[...], preferred_element_type=F32)
    u = jnp.dot(h, wu_ref[...], preferred_element_type=F32)
    o_ref[...] = (_silu(g) * u).astype(BF16)


def _norm_swiglu(x, nw, wg, wu, layer, tm, tn):
    m, d = x.shape
    n = wg.shape[2]
    return pl.pallas_call(
        _norm_swiglu_kernel,
        grid=(m // tm, n // tn),
        in_specs=[pl.BlockSpec((tm, d), lambda i, j: (i, 0)),
                  pl.BlockSpec((1, d), lambda i, j: (0, 0)),
                  pl.BlockSpec((None, d, tn), lambda i, j: (layer, 0, j)),
                  pl.BlockSpec((None, d, tn), lambda i, j: (layer, 0, j))],
        out_specs=pl.BlockSpec((tm, tn), lambda i, j: (i, j)),
        out_shape=jax.ShapeDtypeStruct((m, n), BF16),
        scratch_shapes=[pltpu.VMEM((tm, d), BF16)],
        compiler_params=pltpu.CompilerParams(
            dimension_semantics=("parallel", "arbitrary"), vmem_limit_bytes=VMEM_LIMIT),
        name="norm_swiglu",
    )(x, nw, wg, wu)


def _matmul_res_kernel(a_ref, w_ref, r_ref, o_ref):
    o_ref[...] = r_ref[...] + jnp.dot(a_ref[...], w_ref[...], preferred_element_type=F32)


def _matmul_res(a, w, layer, res, tm, tn):
    m, k = a.shape
    n = w.shape[2]
    return pl.pallas_call(
        _matmul_res_kernel,
        grid=(n // tn, m // tm),
        in_specs=[pl.BlockSpec((tm, k), lambda j, i: (i, 0)),
                  pl.BlockSpec((None, k, tn), lambda j, i: (layer, 0, j)),
                  pl.BlockSpec((tm, tn), lambda j, i: (i, j))],
        out_specs=pl.BlockSpec((tm, tn), lambda j, i: (i, j)),
        out_shape=jax.ShapeDtypeStruct((m, n), F32),
        compiler_params=pltpu.CompilerParams(
            dimension_semantics=("parallel", "arbitrary"), vmem_limit_bytes=VMEM_LIMIT),
        name="matmul_residual",
    )(a, w, res)


def _matmul_res_norm_kernel(a_ref, w_ref, r_ref, nw_ref, o_ref):
    x = r_ref[...] + jnp.dot(a_ref[...], w_ref[...], preferred_element_type=F32)
    ms = jnp.mean(x * x, axis=-1, keepdims=True)
    o_ref[...] = x * lax.rsqrt(ms + NORM_EPS) * nw_ref[...]


def _matmul_res_norm(a, w, layer, res, nw, tm):
    m, k = a.shape
    n = w.shape[2]
    return pl.pallas_call(
        _matmul_res_norm_kernel,
        grid=(m // tm,),
        in_specs=[pl.BlockSpec((tm, k), lambda i: (i, 0)),
                  pl.BlockSpec((None, k, n), lambda i: (layer, 0, 0), pipeline_mode=pl.Buffered(1)),
                  pl.BlockSpec((tm, n), lambda i: (i, 0)),
                  pl.BlockSpec((1, n), lambda i: (0, 0))],
        out_specs=pl.BlockSpec((tm, n), lambda i: (i, 0)),
        out_shape=jax.ShapeDtypeStruct((m, n), F32),
        compiler_params=pltpu.CompilerParams(
            dimension_semantics=("arbitrary",), vmem_limit_bytes=VMEM_LIMIT),
        name="matmul_residual_final_norm",
    )(a, w, res, nw)


def _matmul2_res_kernel(a_ref, b_ref, wa_ref, wb_ref, r_ref, o_ref):
    acc = jnp.dot(a_ref[...], wa_ref[...], preferred_element_type=F32)
    acc = acc + jnp.dot(b_ref[...], wb_ref[...], preferred_element_type=F32)
    o_ref[...] = r_ref[...] + acc


def _matmul2_res(a, b, w, layer, res, tm, tn):
    m, k = a.shape
    n = w.shape[2]
    return pl.pallas_call(
        _matmul2_res_kernel,
        grid=(n // tn, m // tm),
        in_specs=[pl.BlockSpec((tm, k), lambda j, i: (i, 0)),
                  pl.BlockSpec((tm, k), lambda j, i: (i, 0)),
                  pl.BlockSpec((None, k, tn), lambda j, i: (layer, 0, j)),
                  pl.BlockSpec((None, k, tn), lambda j, i: (layer, 1, j)),
                  pl.BlockSpec((tm, tn), lambda j, i: (i, j))],
        out_specs=pl.BlockSpec((tm, tn), lambda j, i: (i, j)),
        out_shape=jax.ShapeDtypeStruct((m, n), F32),
        compiler_params=pltpu.CompilerParams(
            dimension_semantics=("parallel", "arbitrary"), vmem_limit_bytes=VMEM_LIMIT),
        name="out_proj_residual",
    )(a, b, w, w, res)


def _rows_back(x, tail, n_back):
    c = x.shape[0]
    ext = jnp.concatenate([tail, x], axis=0)
    return [pltpu.roll(ext, s, 0)[SUBLANES:SUBLANES + c, :] for s in range(1, n_back + 1)]


def _rwkv_chunk_units(units, ones_bd, tri, m0):
    c = CHUNK
    n2 = 2 * c
    ss = [_mm(u[4] * u[4], ones_bd) for u in units]
    log_w = [_mm_exact_lhs(tri, u[3]) for u in units]
    bon = [_mm(u[0] * u[1] * u[6], ones_bd) for u in units]

    def bd(x):
        return jnp.concatenate([jnp.where(m0, x, 0.0), jnp.where(m0, 0.0, x)], axis=0).astype(BF16)

    lhs, rhs, bk, v_bd, w_last = [], [], [], [], []
    for (r, k, v, lw, kraw, eta, _, _), s2, lg in zip(units, ss, log_w):
        kk = kraw * lax.rsqrt(s2 + RWKV_L2_EPS)
        beta = kk * eta
        lw_last = lg[c - 1:c, :]
        w_inv = jnp.exp(-lg)
        w_end = jnp.exp(lw_last - lg)
        lhs.append(jnp.concatenate([bd(-kk * jnp.exp(lg - lw)), bd(r * jnp.exp(lg))], axis=0))
        rhs.append(jnp.concatenate([bd(beta * w_inv), bd(k * w_inv)], axis=0))
        bk.append(jnp.concatenate([bd(beta * w_end), bd(k * w_end)], axis=0))
        v_bd.append(bd(v))
        w_last.append(jnp.exp(lw_last))

    sc = [_mm_nt(a, b) for a, b in zip(lhs, rhs)]
    xr = [_mm_nt(a, u[7]) for a, u in zip(lhs, units)]
    i, j = _iota2((n2, n2))
    strict = i > j
    incl = i >= j
    a_ab = [jnp.where(strict, x[:n2, :n2], 0.0) for x in sc]
    av = [_mm(jnp.where(strict, x[:n2, n2:], 0.0), vb) for x, vb in zip(sc, v_bd)]
    a_r = [jnp.where(jnp.concatenate([incl, incl], axis=1), x[n2:, :], 0.0).astype(BF16) for x in sc]
    t_inv = _unit_lower_inverse(a_ab)
    u_mat = [_mm(t, x[:n2] + a) for t, x, a in zip(t_inv, xr, av)]
    uv = [jnp.concatenate([x.astype(BF16), vb], axis=0) for x, vb in zip(u_mat, v_bd)]
    y_bd = [x[n2:] + _mm(a, z) for x, a, z in zip(xr, a_r, uv)]
    s_new = [u[7] * wl + _mm_tn(z, b) for u, wl, z, b in zip(units, w_last, uv, bk)]
    y = [x[:c] + x[c:] for x in y_bd]
    return y, bon, s_new


def _rwkv_mix_kernel(has_vres, *refs):
    if has_vres:
        (rkv_ref, wa_ref, gd_ref, ab_ref, vf_ref, mu_rkv_ref, mu_sm_ref, w0_ref, wup_ref, a0_ref,
         aup_ref, gup_ref, kk_ref, ka_ref, rk_ref, lnw_ref, lnb_ref, vup_ref, v0_ref,
         o_ref, s_ref, tail_rkv, tail_sm) = refs
        vout_ref = None
    else:
        (rkv_ref, wa_ref, gd_ref, ab_ref, mu_rkv_ref, mu_sm_ref, w0_ref, wup_ref, a0_ref,
         aup_ref, gup_ref, kk_ref, ka_ref, rk_ref, lnw_ref, lnb_ref,
         o_ref, vout_ref, s_ref, tail_rkv, tail_sm) = refs
    bb, tt = rkv_ref.shape[0], rkv_ref.shape[1]
    n_pairs = RWKV_DIM // LANES

    @pl.when(pl.program_id(1) == 0)
    def _():
        s_ref[...] = jnp.zeros(s_ref.shape, F32)
        tail_rkv[...] = jnp.zeros(tail_rkv.shape, F32)
        tail_sm[...] = jnp.zeros(tail_sm.shape, F32)

    li, lj = _iota2((LANES, LANES))
    ones_bd = ((li // RWKV_HEAD) == (lj // RWKV_HEAD)).astype(BF16)
    ti, tj = _iota2((CHUNK, CHUNK))
    tri = (ti >= tj).astype(BF16)
    m0 = lax.broadcasted_iota(jnp.int32, (CHUNK, LANES), 1) < RWKV_HEAD
    inv_n = 1.0 / RWKV_HEAD
    ids = [(bi, slice(p * LANES, (p + 1) * LANES)) for bi in range(bb) for p in range(n_pairs)]

    def body(ci, carry):
        rows = pl.ds(pl.multiple_of(ci * CHUNK, CHUNK), CHUNK)
        sh, shs = [], []
        for bi in range(bb):
            p = rkv_ref[bi, rows, :]
            (prev,) = _rows_back(p, tail_rkv[bi], 1)
            tail_rkv[bi] = p[CHUNK - SUBLANES:, :]
            sh.append(p + (prev - p) * mu_rkv_ref[...])
            ps = jnp.concatenate([wa_ref[bi, rows, :], gd_ref[bi, rows, :], ab_ref[bi, rows, :]], axis=1)
            (prev_s,) = _rows_back(ps, tail_sm[bi], 1)
            tail_sm[bi] = ps[CHUNK - SUBLANES:, :]
            shs.append(ps + (prev_s - ps) * mu_sm_ref[...])
        w_lo = [_mm(jnp.tanh(x[:, SM_WA:SM_WA + 128]), wup_ref[...]) for x in shs]
        a_lo = [_mm(x[:, SM_WA:SM_WA + 128], aup_ref[...]) for x in shs]
        g_all = [_mm(_sigmoid(x[:, SM_GD:SM_GD + 256]), gup_ref[...]) for x in shs]
        if has_vres:
            v_lo = [_mm(x[:, SM_AB:SM_AB + 128], vup_ref[...]) for x in shs]
        r_all, k_all, v_all, lw_all, kr_all, eta_all = [], [], [], [], [], []
        for bi in range(bb):
            w = -_softplus(-(w0_ref[...] + w_lo[bi])) - 0.5
            lw_all.append(-jnp.exp(w))
            eta = _sigmoid(a0_ref[...] + a_lo[bi])
            k = sh[bi][:, RWKV_DIM:2 * RWKV_DIM]
            v = sh[bi][:, 2 * RWKV_DIM:3 * RWKV_DIM]
            if has_vres:
                v = v + (vf_ref[bi, rows, :] - v) * _sigmoid(v0_ref[...] + v_lo[bi])
            else:
                vout_ref[bi, rows, :] = v
            r_all.append(sh[bi][:, 0:RWKV_DIM])
            v_all.append(v)
            kr_all.append(k * kk_ref[...])
            k_all.append(k * (1.0 + (eta - 1.0) * ka_ref[...]))
            eta_all.append(eta)
        units = [tuple(a[bi][:, cs] for a in (r_all, k_all, v_all, lw_all, kr_all, eta_all))
                 + (rk_ref[:, cs], s_ref[n]) for n, (bi, cs) in enumerate(ids)]
        y, bon, s_new = _rwkv_chunk_units(units, ones_bd, tri, m0)
        mean = [_mm(x, ones_bd) * inv_n for x in y]
        d = [x - mu for x, mu in zip(y, mean)]
        var = [_mm(x * x, ones_bd) * inv_n for x in d]
        for n, (bi, cs) in enumerate(ids):
            yn = d[n] * lax.rsqrt(var[n] + RWKV_GN_EPS) * lnw_ref[:, cs] + lnb_ref[:, cs]
            o_ref[bi, rows, cs] = ((yn + bon[n] * units[n][2]) * g_all[bi][:, cs]).astype(BF16)
            s_ref[n] = s_new[n]
        return carry

    lax.fori_loop(0, tt // CHUNK, body, 0, unroll=True)


def _rwkv_mix(proj, v_first, lp, batch, seq, tt, bb):
    nt = seq // tt
    has_vres = v_first is not None
    full = lambda b, t: (0, 0)
    wide = pl.BlockSpec((bb, tt, RWKV_DIM), lambda b, t: (b, t, 0))
    vec = pl.BlockSpec((1, RWKV_DIM), full)
    lora = pl.BlockSpec((128, RWKV_DIM), full)
    col_blk = lambda col, w: pl.BlockSpec((bb, tt, w), lambda b, t: (b, t, col // w))
    in_specs = [col_blk(COL_RKV, 3 * RWKV_DIM), col_blk(COL_WA, 128), col_blk(COL_GD, 256),
                col_blk(COL_AB, 128)]
    args = [proj, proj, proj, proj]
    if has_vres:
        in_specs.append(wide)
        args.append(v_first)
    in_specs += [pl.BlockSpec((1, 3 * RWKV_DIM), full), pl.BlockSpec((1, SM_W), full),
                 vec, lora, vec, lora, pl.BlockSpec((256, RWKV_DIM), full), vec, vec, vec, vec, vec]
    args += [lp["mu_rkv"], lp["mu_sm"], lp["w0"], lp["w_up"], lp["a0"], lp["a_up"], lp["g_up"],
             lp["k_k"], lp["k_a"], lp["r_k"], lp["ln_w"], lp["ln_b"]]
    y_shape = jax.ShapeDtypeStruct((batch, seq, RWKV_DIM), BF16)
    if has_vres:
        in_specs += [lora, vec]
        args += [lp["v_up"], lp["v0"]]
        out_specs, out_shape = wide, y_shape
    else:
        out_specs = [wide, wide]
        out_shape = [y_shape, jax.ShapeDtypeStruct((batch, seq, RWKV_DIM), F32)]
    return pl.pallas_call(
        functools.partial(_rwkv_mix_kernel, has_vres),
        grid=(batch // bb, nt),
        in_specs=in_specs,
        out_specs=out_specs,
        out_shape=out_shape,
        scratch_shapes=[pltpu.VMEM((bb * RWKV_DIM // LANES, LANES, LANES), F32),
                        pltpu.VMEM((bb, SUBLANES, 3 * RWKV_DIM), F32),
                        pltpu.VMEM((bb, SUBLANES, SM_W), F32)],
        compiler_params=pltpu.CompilerParams(
            dimension_semantics=("parallel", "arbitrary"), vmem_limit_bytes=VMEM_LIMIT),
        name="rwkv_mix",
    )(*args)


def _gdn_chunk_prep(x, tail, ab, cw, a_log, dt_bias, tri):
    x1, x2, x3 = _rows_back(x, tail, GDN_CONV - 1)
    y = _silu(x * cw[3:4, :] + x1 * cw[2:3, :] + x2 * cw[1:2, :] + x3 * cw[0:1, :])
    q, k = [], []
    for h in range(GDN_QK_HEADS):
        qh = y[:, h * GDN_HEAD:(h + 1) * GDN_HEAD]
        kh = y[:, GDN_QK_DIM + h * GDN_HEAD:GDN_QK_DIM + (h + 1) * GDN_HEAD]
        q.append(qh * (lax.rsqrt(jnp.sum(qh * qh, axis=-1, keepdims=True) + GDN_L2_EPS)
                       * (GDN_HEAD ** -0.5)))
        k.append(kh * lax.rsqrt(jnp.sum(kh * kh, axis=-1, keepdims=True) + GDN_L2_EPS))
    g = -jnp.exp(a_log) * _softplus(ab + dt_bias)
    beta = _sigmoid(ab)
    g_cum = _mm_exact_lhs(tri, g)
    bc = lambda a, j: jnp.broadcast_to(a[:, j:j + 1], (a.shape[0], GDN_HEAD))
    g_b = jnp.concatenate([bc(g_cum, j) for j in range(GDN_V_HEADS)], axis=1)
    beta_b = jnp.concatenate([bc(beta, GDN_V_HEADS + j) for j in range(GDN_V_HEADS)], axis=1)
    return q, k, y[:, 2 * GDN_QK_DIM:], g_b, beta_b


def _gdn_chunk_units(units):
    c = CHUNK
    n2 = 2 * c
    st = lambda x: jnp.concatenate([x[:, :GDN_HEAD], x[:, GDN_HEAD:]], axis=0)
    i, j = _iota2((n2, n2))
    incl = ((i // c) == (j // c)) & (i >= j)
    strict = i > j

    g_c = [st(u[3]) for u in units]
    b_c = [st(u[4]) for u in units]
    k2 = [jnp.concatenate([u[1], u[1]], axis=0) for u in units]
    q2 = [jnp.concatenate([u[0], u[0]], axis=0) for u in units]
    kb = [a * b for a, b in zip(k2, b_c)]
    kq = [_mm_nt(jnp.concatenate([a, b], axis=0), kk) for a, b, kk in zip(kb, q2, k2)]
    gamma = [jnp.exp(jnp.where(incl, g - g.T, -jnp.inf)) for g in g_c]
    l_neg = [jnp.where(strict, -(x[:n2] * gm), 0.0) for x, gm in zip(kq, gamma)]
    a_qk = [(x[n2:] * gm).astype(BF16) for x, gm in zip(kq, gamma)]
    t_inv = _unit_lower_inverse(l_neg)
    e_g = [jnp.exp(g) for g in g_c]
    uw = [_mm(t, jnp.concatenate([st(u[2]) * b, kbi * eg], axis=1))
          for t, u, b, kbi, eg in zip(t_inv, units, b_c, kb, e_g)]
    qg = [qq * eg for qq, eg in zip(q2, e_g)]
    hd = lambda e: slice(e * GDN_HEAD, (e + 1) * GDN_HEAD)
    rw = lambda e: slice(e * c, (e + 1) * c)
    wq = [[_mm(jnp.concatenate([x[rw(e), GDN_HEAD:], qe[rw(e)]], axis=0), u[5][:, hd(e)])
           for x, qe, u in zip(uw, qg, units)] for e in range(2)]
    v_new = [jnp.concatenate([x[rw(0), :GDN_HEAD] - w0[:c], x[rw(1), :GDN_HEAD] - w1[:c]], axis=0)
             for x, w0, w1 in zip(uw, wq[0], wq[1])]
    o_s = [jnp.concatenate([w0[c:], w1[c:]], axis=0) + _mm(a, vn)
           for w0, w1, a, vn in zip(wq[0], wq[1], a_qk, v_new)]
    k_g, decay = [], []
    for g, u in zip(g_c, units):
        g_last = (g[c - 1:c, :], g[n2 - 1:n2, :])
        k_g.append([u[1] * jnp.exp(g_last[e] - g[rw(e)]) for e in range(2)])
        decay.append((jnp.exp(g_last[0]), jnp.exp(g_last[1])))
    ds = [[_mm_tn(kg[e], vn[rw(e)]) for kg, vn in zip(k_g, v_new)] for e in range(2)]
    s_new = [jnp.concatenate([u[5][:, hd(0)] * dc[0] + d0, u[5][:, hd(1)] * dc[1] + d1], axis=1)
             for u, dc, d0, d1 in zip(units, decay, ds[0], ds[1])]
    outs = [jnp.concatenate([o[:c], o[c:]], axis=1) for o in o_s]
    return outs, s_new


def _gdn_mix_kernel(qk_ref, v_ref, z_ref, ab_ref, cw_ref, alog_ref, dtb_ref, nw_ref,
                    o_ref, s_ref, tail_ref):
    bb, tt = qk_ref.shape[0], qk_ref.shape[1]
    pw = 2 * GDN_HEAD

    @pl.when(pl.program_id(1) == 0)
    def _():
        s_ref[...] = jnp.zeros(s_ref.shape, F32)
        tail_ref[...] = jnp.zeros(tail_ref.shape, F32)

    ids = [(bi, h) for bi in range(bb) for h in range(GDN_QK_HEADS)]
    ti, tj = _iota2((CHUNK, CHUNK))
    tri = (ti >= tj).astype(BF16)

    def body(ci, carry):
        rows = pl.ds(pl.multiple_of(ci * CHUNK, CHUNK), CHUNK)
        rowp = []
        for bi in range(bb):
            x = jnp.concatenate([qk_ref[bi, rows, :], v_ref[bi, rows, :]], axis=1)
            rowp.append(_gdn_chunk_prep(x, tail_ref[bi], ab_ref[bi, rows, :],
                                        cw_ref[...], alog_ref[...], dtb_ref[...], tri))
            tail_ref[bi] = x[CHUNK - SUBLANES:, :]
        units = []
        for n, (bi, h) in enumerate(ids):
            q, k, v, g_b, beta_b = rowp[bi]
            cv = slice(h * pw, (h + 1) * pw)
            units.append((q[h], k[h], v[:, cv], g_b[:, cv], beta_b[:, cv], s_ref[n]))
        outs, s_new = _gdn_chunk_units(units)
        for n, (bi, h) in enumerate(ids):
            s_ref[n] = s_new[n]
            for e in range(2):
                cols = slice(h * pw + e * GDN_HEAD, h * pw + (e + 1) * GDN_HEAD)
                oh = outs[n][:, e * GDN_HEAD:(e + 1) * GDN_HEAD]
                oh = oh * lax.rsqrt(jnp.mean(oh * oh, axis=-1, keepdims=True) + GDN_NORM_EPS) * nw_ref[...]
                o_ref[bi, rows, cols] = (oh * _silu(z_ref[bi, rows, cols])).astype(BF16)
        return carry

    lax.fori_loop(0, tt // CHUNK, body, 0, unroll=True)


def _gdn_mix(proj, lp, batch, seq, tt, bb):
    nt = seq // tt
    full = lambda b, t: (0, 0)
    qk_w = 2 * GDN_QK_DIM
    wide = lambda col: pl.BlockSpec((bb, tt, GDN_V_DIM), lambda b, t: (b, t, col // GDN_V_DIM))
    return pl.pallas_call(
        _gdn_mix_kernel,
        grid=(batch // bb, nt),
        in_specs=[wide(COL_GQK), wide(COL_GV), wide(COL_GZ),
                  pl.BlockSpec((bb, tt, LANES), lambda b, t: (b, t, COL_AB // LANES)),
                  pl.BlockSpec((GDN_CONV, qk_w + GDN_V_DIM), full),
                  pl.BlockSpec((1, LANES), full),
                  pl.BlockSpec((1, LANES), full),
                  pl.BlockSpec((1, GDN_HEAD), full)],
        out_specs=wide(0),
        out_shape=jax.ShapeDtypeStruct((batch, seq, GDN_V_DIM), BF16),
        scratch_shapes=[pltpu.VMEM((bb * GDN_QK_HEADS, GDN_HEAD, 2 * GDN_HEAD), F32),
                        pltpu.VMEM((bb, SUBLANES, qk_w + GDN_V_DIM), F32)],
        compiler_params=pltpu.CompilerParams(
            dimension_semantics=("parallel", "arbitrary"), vmem_limit_bytes=VMEM_LIMIT),
        name="gdn_mix",
    )(proj, proj, proj, proj, lp["conv_w"], lp["a_log"], lp["dt_bias"], lp["norm_w"])


def _pad_cols(a, n):
    return jnp.pad(a, [(0, 0)] * (a.ndim - 1) + [(0, n - a.shape[-1])])


def _proj_layout(main, vd):
    n_small = RWKV_DECAY_LORA + RWKV_AAA_LORA + RWKV_GATE_LORA
    rkv = main[..., :3 * RWKV_DIM]
    small = main[..., 3 * RWKV_DIM:3 * RWKV_DIM + n_small]
    gdn = main[..., 3 * RWKV_DIM + n_small:]
    zeros = lambda n: jnp.zeros(main.shape[:-1] + (n,), main.dtype)
    ab_pad = COL_WA - (COL_GQK + gdn.shape[-1] + vd.shape[-1])
    return jnp.concatenate([rkv, gdn, vd, zeros(ab_pad), small, zeros(N_PROJ - COL_WA - n_small)], axis=-1)


def kernel(x, attn_norm_w, w_in, rwkv_mu, rwkv_w0, rwkv_w_up, rwkv_a0, rwkv_a_up, rwkv_g_up,
           rwkv_k_k, rwkv_k_a, rwkv_r_k, rwkv_ln_w, rwkv_ln_b, vres_down, vres_mu, vres_up, vres_v0,
           gdn_conv_w, gdn_A_log, gdn_dt_bias, gdn_norm_w, w_out, ffn_norm_w, ffn_w_gate, ffn_w_up,
           ffn_w_down, final_norm_w):
    batch, seq, d = x.shape
    depth = w_in.shape[0]
    m = batch * seq

    gdn_pad = jnp.zeros((depth, w_in.shape[2] - rwkv_mu.shape[1]), F32)
    vmu = jnp.concatenate([jnp.zeros((1, RWKV_MV_LORA), F32), vres_mu], axis=0)
    mu_all = _proj_layout(jnp.concatenate([rwkv_mu, gdn_pad], axis=1), vmu)
    row_place = lambda a, before, total: jnp.pad(
        a, ((0, 0), (before, total - before - a.shape[1]), (0, 0))).astype(BF16)
    w_up = row_place(rwkv_w_up, 0, 128)
    a_up = row_place(rwkv_a_up, RWKV_DECAY_LORA, 128)
    g_up = row_place(rwkv_g_up, 0, 256)
    v_up = row_place(vres_up, AB_VD, 128)
    a_log = _pad_cols(gdn_A_log, LANES)
    dt_bias = _pad_cols(gdn_dt_bias, LANES)
    r_k = rwkv_r_k.reshape(depth, RWKV_DIM)
    vd_w = jnp.concatenate([jnp.zeros((1, d, RWKV_MV_LORA), BF16), vres_down.astype(BF16)], axis=0)
    w_proj = _proj_layout(w_in.astype(BF16), vd_w)
    w_out_b = w_out.astype(BF16)
    wg_b = ffn_w_gate.astype(BF16)
    wu_b = ffn_w_up.astype(BF16)
    wd_b = ffn_w_down.astype(BF16)

    tm = min(1024, m)
    row = lambda a, l: a[l][None, :]

    xf = x.reshape(m, d)
    v_first = None
    for l in range(depth):
        proj = _norm_matmul(xf, row(attn_norm_w, l), w_proj, l, tm, 1664)
        mu_l = mu_all[l]
        lp = dict(mu_rkv=mu_l[None, COL_RKV:COL_RKV + 3 * RWKV_DIM],
                  mu_sm=jnp.concatenate([mu_l[COL_WA:COL_WA + 128], mu_l[COL_GD:COL_GD + 256],
                                         mu_l[COL_AB:COL_AB + 128]])[None, :],
                  w0=row(rwkv_w0, l), w_up=w_up[l], a0=row(rwkv_a0, l), a_up=a_up[l], g_up=g_up[l],
                  k_k=row(rwkv_k_k, l), k_a=row(rwkv_k_a, l), r_k=row(r_k, l),
                  ln_w=row(rwkv_ln_w, l), ln_b=row(rwkv_ln_b, l),
                  conv_w=gdn_conv_w[l], a_log=row(a_log, l), dt_bias=row(dt_bias, l),
                  norm_w=row(gdn_norm_w, l))
        if l > 0:
            lp["v_up"] = v_up[l - 1]
            lp["v0"] = row(vres_v0, l - 1)
        as3 = lambda a: a.reshape(batch, seq, a.shape[-1])
        bb_a = 2 if batch % 2 == 0 else 1
        if l == 0:
            y_a, v_first = _rwkv_mix(as3(proj), None, lp, batch, seq, min(128, seq), bb_a)
        else:
            y_a = _rwkv_mix(as3(proj), v_first, lp, batch, seq, min(128, seq), bb_a)
        y_a = y_a.reshape(m, RWKV_DIM)
        bb_b = 4 if batch % 4 == 0 else bb_a
        y_b = _gdn_mix(as3(proj), lp, batch, seq, min(128, seq), bb_b).reshape(m, GDN_V_DIM)
        xf = _matmul2_res(y_a, y_b, w_out_b, l, xf, min(512, m), d)
        hmid = _norm_swiglu(xf, row(ffn_norm_w, l), wg_b, wu_b, l, tm, 512)
        if l + 1 < depth:
            xf = _matmul_res(hmid, wd_b, l, xf, min(512, m), 1024)
        else:
            xf = _matmul_res_norm(hmid, wd_b, l, xf, final_norm_w[None, :], min(256, m))
    return xf.reshape(batch, seq, d)
```

```python
import functools

import jax
import jax.numpy as jnp
from jax import lax
from jax.experimental import pallas as pl
from jax.experimental.pallas import tpu as pltpu

F32 = jnp.float32
BF16 = jnp.bfloat16

D_MODEL = 2048
RWKV_HEAD = 64
RWKV_DIM = 1024
RWKV_DECAY_LORA = 64
RWKV_AAA_LORA = 64
RWKV_MV_LORA = 32
RWKV_GATE_LORA = 160
RWKV_GN_EPS = RWKV_HEAD * 1e-5
RWKV_L2_EPS = 1e-12
GDN_HEAD = 128
GDN_V_DIM = 1024
GDN_V_HEADS = 8
GDN_QK_HEADS = 4
GDN_QK_DIM = 512
GDN_CONV = 4
GDN_L2_EPS = 1e-6
GDN_NORM_EPS = 1e-6
D_FF = 5632
NORM_EPS = 1e-5

LANES = 128
SUBLANES = 8
CHUNK = 64
PAIR = 2 * CHUNK
VMEM_LIMIT = 56 * 1024 * 1024

COL_RKV = 0
COL_GQK = 3 * RWKV_DIM
COL_GV = COL_GQK + 2 * GDN_QK_DIM
COL_GZ = COL_GV + GDN_V_DIM
COL_AB = COL_GZ + GDN_V_DIM
COL_WA = COL_AB + 128
COL_GD = COL_WA + 128
N_PROJ = COL_GD + 256
AB_VD = 16
SM_WA, SM_GD, SM_AB = 0, 128, 384
SM_W = 512


def _mm(a, b):
    return jnp.dot(a.astype(BF16), b.astype(BF16), preferred_element_type=F32)


def _mm_nt(a, b):
    return lax.dot_general(a.astype(BF16), b.astype(BF16), (((1,), (1,)), ((), ())),
                           preferred_element_type=F32)


def _mm_tn(a, b):
    return lax.dot_general(a.astype(BF16), b.astype(BF16), (((0,), (0,)), ((), ())),
                           preferred_element_type=F32)


def _split3(x):
    hi = x.astype(BF16)
    r1 = x - hi.astype(F32)
    mid = r1.astype(BF16)
    lo = (r1 - mid.astype(F32)).astype(BF16)
    return hi, mid, lo


def _mm_exact_lhs(a01, x):
    n = x.shape[1]
    hi, mid, lo = _split3(x)
    y = jnp.dot(a01, jnp.concatenate([hi, mid, lo], axis=1), preferred_element_type=F32)
    return y[:, :n] + y[:, n:2 * n] + y[:, 2 * n:]


def _sigmoid(x):
    return 1.0 / (1.0 + jnp.exp(-x))


def _softplus(x):
    return jnp.maximum(x, 0.0) + jnp.log(1.0 + jnp.exp(-jnp.abs(x)))


def _silu(x):
    return x * _sigmoid(x)


def _iota2(shape):
    return (lax.broadcasted_iota(jnp.int32, shape, 0), lax.broadcasted_iota(jnp.int32, shape, 1))


def _unit_lower_inverse(n_mats):
    sz = n_mats[0].shape[0]
    i, j = _iota2((sz, sz))
    eye = (i == j).astype(F32)
    blk8 = (i // 8) == (j // 8)
    d1 = [jnp.where(blk8, n, 0.0) for n in n_mats]
    d2 = [_mm(x, x) for x in d1]
    d4 = [_mm(x, x) for x in d2]
    p = [eye + x for x in d1]
    p = [x + _mm(x, y) for x, y in zip(p, d2)]
    p = [x + _mm(x, y) for x, y in zip(p, d4)]
    for s in (8, 16, 32):
        off = ((i // (2 * s)) == (j // (2 * s))) & ((i // s) == (j // s) + 1)
        q = [_mm(jnp.where(off, n, 0.0), x) for n, x in zip(n_mats, p)]
        p = [x + _mm(x, y) for x, y in zip(p, q)]
    return p


def _norm_matmul_kernel(x_ref, nw_ref, w_ref, o_ref, h_ref):
    @pl.when(pl.program_id(1) == 0)
    def _():
        x = x_ref[...]
        ms = jnp.mean(x * x, axis=-1, keepdims=True)
        h_ref[...] = (x * lax.rsqrt(ms + NORM_EPS) * nw_ref[...]).astype(BF16)

    o_ref[...] = jnp.dot(h_ref[...], w_ref[...], preferred_element_type=F32)


def _norm_matmul(x, nw, w, layer, tm, tn):
    m, d = x.shape
    n = w.shape[2]
    return pl.pallas_call(
        _norm_matmul_kernel,
        grid=(m // tm, n // tn),
        in_specs=[pl.BlockSpec((tm, d), lambda i, j: (i, 0)),
                  pl.BlockSpec((1, d), lambda i, j: (0, 0)),
                  pl.BlockSpec((None, d, tn), lambda i, j: (layer, 0, j))],
        out_specs=pl.BlockSpec((tm, tn), lambda i, j: (i, j)),
        out_shape=jax.ShapeDtypeStruct((m, n), F32),
        scratch_shapes=[pltpu.VMEM((tm, d), BF16)],
        compiler_params=pltpu.CompilerParams(
            dimension_semantics=("parallel", "arbitrary"), vmem_limit_bytes=VMEM_LIMIT),
        name="norm_in_proj",
    )(x, nw, w)


def _norm_swiglu_kernel(x_ref, nw_ref, wg_ref, wu_ref, o_ref, h_ref):
    @pl.when(pl.program_id(1) == 0)
    def _():
        x = x_ref[...]
        ms = jnp.mean(x * x, axis=-1, keepdims=True)
        h_ref[...] = (x * lax.rsqrt(ms + NORM_EPS) * nw_ref[...]).astype(BF16)

    h = h_ref[...]
    g = jnp.dot(h, wg_ref[...], preferred_element_type=F32)
    u = jnp.dot(h, wu_ref[...], preferred_element_type=F32)
    o_ref[...] = (_silu(g) * u).astype(BF16)


def _norm_swiglu(x, nw, wg, wu, layer, tm, tn):
    m, d = x.shape
    n = wg.shape[2]
    return pl.pallas_call(
        _norm_swiglu_kernel,
        grid=(m // tm, n // tn),
        in_specs=[pl.BlockSpec((tm, d), lambda i, j: (i, 0)),
                  pl.BlockSpec((1, d), lambda i, j: (0, 0)),
                  pl.BlockSpec((None, d, tn), lambda i, j: (layer, 0, j)),
                  pl.BlockSpec((None, d, tn), lambda i, j: (layer, 0, j))],
        out_specs=pl.BlockSpec((tm, tn), lambda i, j: (i, j)),
        out_shape=jax.ShapeDtypeStruct((m, n), BF16),
        scratch_shapes=[pltpu.VMEM((tm, d), BF16)],
        compiler_params=pltpu.CompilerParams(
            dimension_semantics=("parallel", "arbitrary"), vmem_limit_bytes=VMEM_LIMIT),
        name="norm_swiglu",
    )(x, nw, wg, wu)


def _matmul_res_kernel(a_ref, w_ref, r_ref, o_ref):
    o_ref[...] = r_ref[...] + jnp.dot(a_ref[...], w_ref[...], preferred_element_type=F32)


def _matmul_res(a, w, layer, res, tm, tn):
    m, k = a.shape
    n = w.shape[2]
    return pl.pallas_call(
        _matmul_res_kernel,
        grid=(n // tn, m // tm),
        in_specs=[pl.BlockSpec((tm, k), lambda j, i: (i, 0)),
                  pl.BlockSpec((None, k, tn), lambda j, i: (layer, 0, j)),
                  pl.BlockSpec((tm, tn), lambda j, i: (i, j))],
        out_specs=pl.BlockSpec((tm, tn), lambda j, i: (i, j)),
        out_shape=jax.ShapeDtypeStruct((m, n), F32),
        compiler_params=pltpu.CompilerParams(
            dimension_semantics=("parallel", "arbitrary"), vmem_limit_bytes=VMEM_LIMIT),
        name="matmul_residual",
    )(a, w, res)


def _matmul_res_norm_kernel(a_ref, w_ref, r_ref, nw_ref, o_ref):
    x = r_ref[...] + jnp.dot(a_ref[...], w_ref[...], preferred_element_type=F32)
    ms = jnp.mean(x * x, axis=-1, keepdims=True)
    o_ref[...] = x * lax.rsqrt(ms + NORM_EPS) * nw_ref[...]


def _matmul_res_norm(a, w, layer, res, nw, tm):
    m, k = a.shape
    n = w.shape[2]
    return pl.pallas_call(
        _matmul_res_norm_kernel,
        grid=(m // tm,),
        in_specs=[pl.BlockSpec((tm, k), lambda i: (i, 0)),
                  pl.BlockSpec((None, k, n), lambda i: (layer, 0, 0), pipeline_mode=pl.Buffered(1)),
                  pl.BlockSpec((tm, n), lambda i: (i, 0)),
                  pl.BlockSpec((1, n), lambda i: (0, 0))],
        out_specs=pl.BlockSpec((tm, n), lambda i: (i, 0)),
        out_shape=jax.ShapeDtypeStruct((m, n), F32),
        compiler_params=pltpu.CompilerParams(
            dimension_semantics=("arbitrary",), vmem_limit_bytes=VMEM_LIMIT),
        name="matmul_residual_final_norm",
    )(a, w, res, nw)


def _matmul2_res_kernel(a_ref, b_ref, wa_ref, wb_ref, r_ref, o_ref):
    acc = jnp.dot(a_ref[...], wa_ref[...], preferred_element_type=F32)
    acc = acc + jnp.dot(b_ref[...], wb_ref[...], preferred_element_type=F32)
    o_ref[...] = r_ref[...] + acc


def _matmul2_res(a, b, w, layer, res, tm, tn):
    m, k = a.shape
    n = w.shape[2]
    return pl.pallas_call(
        _matmul2_res_kernel,
        grid=(n // tn, m // tm),
        in_specs=[pl.BlockSpec((tm, k), lambda j, i: (i, 0)),
                  pl.BlockSpec((tm, k), lambda j, i: (i, 0)),
                  pl.BlockSpec((None, k, tn), lambda j, i: (layer, 0, j)),
                  pl.BlockSpec((None, k, tn), lambda j, i: (layer, 1, j)),
                  pl.BlockSpec((tm, tn), lambda j, i: (i, j))],
        out_specs=pl.BlockSpec((tm, tn), lambda j, i: (i, j)),
        out_shape=jax.ShapeDtypeStruct((m, n), F32),
        compiler_params=pltpu.CompilerParams(
            dimension_semantics=("parallel", "arbitrary"), vmem_limit_bytes=VMEM_LIMIT),
        name="out_proj_residual",
    )(a, b, w, w, res)


def _rows_back(x, tail, n_back):
    c = x.shape[0]
    ext = jnp.concatenate([tail, x], axis=0)
    return [pltpu.roll(ext, s, 0)[SUBLANES:SUBLANES + c, :] for s in range(1, n_back + 1)]


def _rwkv_chunk_units(units, ones_bd, tri, m0):
    c = CHUNK
    n2 = 2 * c
    ss = [_mm(u[4] * u[4], ones_bd) for u in units]
    log_w = [_mm_exact_lhs(tri, u[3]) for u in units]
    bon = [_mm(u[0] * u[1] * u[6], ones_bd) for u in units]

    def bd(x):
        return jnp.concatenate([jnp.where(m0, x, 0.0), jnp.where(m0, 0.0, x)], axis=0).astype(BF16)

    lhs, rhs, bk, v_bd, w_last = [], [], [], [], []
    for (r, k, v, lw, kraw, eta, _, _), s2, lg in zip(units, ss, log_w):
        kk = kraw * lax.rsqrt(s2 + RWKV_L2_EPS)
        beta = kk * eta
        lw_last = lg[c - 1:c, :]
        w_inv = jnp.exp(-lg)
        w_end = jnp.exp(lw_last - lg)
        lhs.append(jnp.concatenate([bd(-kk * jnp.exp(lg - lw)), bd(r * jnp.exp(lg))], axis=0))
        rhs.append(jnp.concatenate([bd(beta * w_inv), bd(k * w_inv)], axis=0))
        bk.append(jnp.concatenate([bd(beta * w_end), bd(k * w_end)], axis=0))
        v_bd.append(bd(v))
        w_last.append(jnp.exp(lw_last))

    sc = [_mm_nt(a, b) for a, b in zip(lhs, rhs)]
    xr = [_mm_nt(a, u[7]) for a, u in zip(lhs, units)]
    i, j = _iota2((n2, n2))
    strict = i > j
    incl = i >= j
    a_ab = [jnp.where(strict, x[:n2, :n2], 0.0) for x in sc]
    av = [_mm(jnp.where(strict, x[:n2, n2:], 0.0), vb) for x, vb in zip(sc, v_bd)]
    a_r = [jnp.where(jnp.concatenate([incl, incl], axis=1), x[n2:, :], 0.0).astype(BF16) for x in sc]
    t_inv = _unit_lower_inverse(a_ab)
    u_mat = [_mm(t, x[:n2] + a) for t, x, a in zip(t_inv, xr, av)]
    uv = [jnp.concatenate([x.astype(BF16), vb], axis=0) for x, vb in zip(u_mat, v_bd)]
    y_bd = [x[n2:] + _mm(a, z) for x, a, z in zip(xr, a_r, uv)]
    s_new = [u[7] * wl + _mm_tn(z, b) for u, wl, z, b in zip(units, w_last, uv, bk)]
    y = [x[:c] + x[c:] for x in y_bd]
    return y, bon, s_new


def _rwkv_mix_kernel(has_vres, *refs):
    if has_vres:
        (rkv_ref, wa_ref, gd_ref, ab_ref, vf_ref, mu_rkv_ref, mu_sm_ref, w0_ref, wup_ref, a0_ref,
         aup_ref, gup_ref, kk_ref, ka_ref, rk_ref, lnw_ref, lnb_ref, vup_ref, v0_ref,
         o_ref, s_ref, tail_rkv, tail_sm) = refs
        vout_ref = None
    else:
        (rkv_ref, wa_ref, gd_ref, ab_ref, mu_rkv_ref, mu_sm_ref, w0_ref, wup_ref, a0_ref,
         aup_ref, gup_ref, kk_ref, ka_ref, rk_ref, lnw_ref, lnb_ref,
         o_ref, vout_ref, s_ref, tail_rkv, tail_sm) = refs
    bb, tt = rkv_ref.shape[0], rkv_ref.shape[1]
    n_pairs = RWKV_DIM // LANES

    @pl.when(pl.program_id(1) == 0)
    def _():
        s_ref[...] = jnp.zeros(s_ref.shape, F32)
        tail_rkv[...] = jnp.zeros(tail_rkv.shape, F32)
        tail_sm[...] = jnp.zeros(tail_sm.shape, F32)

    li, lj = _iota2((LANES, LANES))
    ones_bd = ((li // RWKV_HEAD) == (lj // RWKV_HEAD)).astype(BF16)
    ti, tj = _iota2((CHUNK, CHUNK))
    tri = (ti >= tj).astype(BF16)
    m0 = lax.broadcasted_iota(jnp.int32, (CHUNK, LANES), 1) < RWKV_HEAD
    inv_n = 1.0 / RWKV_HEAD
    ids = [(bi, slice(p * LANES, (p + 1) * LANES)) for bi in range(bb) for p in range(n_pairs)]

    def body(ci, carry):
        rows = pl.ds(pl.multiple_of(ci * CHUNK, CHUNK), CHUNK)
        sh, shs = [], []
        for bi in range(bb):
            p = rkv_ref[bi, rows, :]
            (prev,) = _rows_back(p, tail_rkv[bi], 1)
            tail_rkv[bi] = p[CHUNK - SUBLANES:, :]
            sh.append(p + (prev - p) * mu_rkv_ref[...])
            ps = jnp.concatenate([wa_ref[bi, rows, :], gd_ref[bi, rows, :], ab_ref[bi, rows, :]], axis=1)
            (prev_s,) = _rows_back(ps, tail_sm[bi], 1)
            tail_sm[bi] = ps[CHUNK - SUBLANES:, :]
            shs.append(ps + (prev_s - ps) * mu_sm_ref[...])
        w_lo = [_mm(jnp.tanh(x[:, SM_WA:SM_WA + 128]), wup_ref[...]) for x in shs]
        a_lo = [_mm(x[:, SM_WA:SM_WA + 128], aup_ref[...]) for x in shs]
        g_all = [_mm(_sigmoid(x[:, SM_GD:SM_GD + 256]), gup_ref[...]) for x in shs]
        if has_vres:
            v_lo = [_mm(x[:, SM_AB:SM_AB + 128], vup_ref[...]) for x in shs]
        r_all, k_all, v_all, lw_all, kr_all, eta_all = [], [], [], [], [], []
        for bi in range(bb):
            w = -_softplus(-(w0_ref[...] + w_lo[bi])) - 0.5
            lw_all.append(-jnp.exp(w))
            eta = _sigmoid(a0_ref[...] + a_lo[bi])
            k = sh[bi][:, RWKV_DIM:2 * RWKV_DIM]
            v = sh[bi][:, 2 * RWKV_DIM:3 * RWKV_DIM]
            if has_vres:
                v = v + (vf_ref[bi, rows, :] - v) * _sigmoid(v0_ref[...] + v_lo[bi])
            else:
                vout_ref[bi, rows, :] = v
            r_all.append(sh[bi][:, 0:RWKV_DIM])
            v_all.append(v)
            kr_all.append(k * kk_ref[...])
            k_all.append(k * (1.0 + (eta - 1.0) * ka_ref[...]))
            eta_all.append(eta)
        units = [tuple(a[bi][:, cs] for a in (r_all, k_all, v_all, lw_all, kr_all, eta_all))
                 + (rk_ref[:, cs], s_ref[n]) for n, (bi, cs) in enumerate(ids)]
        y, bon, s_new = _rwkv_chunk_units(units, ones_bd, tri, m0)
        mean = [_mm(x, ones_bd) * inv_n for x in y]
        d = [x - mu for x, mu in zip(y, mean)]
        var = [_mm(x * x, ones_bd) * inv_n for x in d]
        for n, (bi, cs) in enumerate(ids):
            yn = d[n] * lax.rsqrt(var[n] + RWKV_GN_EPS) * lnw_ref[:, cs] + lnb_ref[:, cs]
            o_ref[bi, rows, cs] = ((yn + bon[n] * units[n][2]) * g_all[bi][:, cs]).astype(BF16)
            s_ref[n] = s_new[n]
        return carry

    lax.fori_loop(0, tt // CHUNK, body, 0, unroll=2)


def _rwkv_mix(proj, v_first, lp, batch, seq, tt, bb):
    nt = seq // tt
    has_vres = v_first is not None
    full = lambda b, t: (0, 0)
    wide = pl.BlockSpec((bb, tt, RWKV_DIM), lambda b, t: (b, t, 0))
    vec = pl.BlockSpec((1, RWKV_DIM), full)
    lora = pl.BlockSpec((128, RWKV_DIM), full)
    col_blk = lambda col, w: pl.BlockSpec((bb, tt, w), lambda b, t: (b, t, col // w))
    in_specs = [col_blk(COL_RKV, 3 * RWKV_DIM), col_blk(COL_WA, 128), col_blk(COL_GD, 256),
                col_blk(COL_AB, 128)]
    args = [proj, proj, proj, proj]
    if has_vres:
        in_specs.append(wide)
        args.append(v_first)
    in_specs += [pl.BlockSpec((1, 3 * RWKV_DIM), full), pl.BlockSpec((1, SM_W), full),
                 vec, lora, vec, lora, pl.BlockSpec((256, RWKV_DIM), full), vec, vec, vec, vec, vec]
    args += [lp["mu_rkv"], lp["mu_sm"], lp["w0"], lp["w_up"], lp["a0"], lp["a_up"], lp["g_up"],
             lp["k_k"], lp["k_a"], lp["r_k"], lp["ln_w"], lp["ln_b"]]
    y_shape = jax.ShapeDtypeStruct((batch, seq, RWKV_DIM), BF16)
    if has_vres:
        in_specs += [lora, vec]
        args += [lp["v_up"], lp["v0"]]
        out_specs, out_shape = wide, y_shape
    else:
        out_specs = [wide, wide]
        out_shape = [y_shape, jax.ShapeDtypeStruct((batch, seq, RWKV_DIM), F32)]
    return pl.pallas_call(
        functools.partial(_rwkv_mix_kernel, has_vres),
        grid=(batch // bb, nt),
        in_specs=in_specs,
        out_specs=out_specs,
        out_shape=out_shape,
        scratch_shapes=[pltpu.VMEM((bb * RWKV_DIM // LANES, LANES, LANES), F32),
                        pltpu.VMEM((bb, SUBLANES, 3 * RWKV_DIM), F32),
                        pltpu.VMEM((bb, SUBLANES, SM_W), F32)],
        compiler_params=pltpu.CompilerParams(
            dimension_semantics=("parallel", "arbitrary"), vmem_limit_bytes=VMEM_LIMIT),
        name="rwkv_mix",
    )(*args)


def _gdn_chunk_prep(x, tail, ab, cw, a_log, dt_bias, tri):
    x1, x2, x3 = _rows_back(x, tail, GDN_CONV - 1)
    y = _silu(x * cw[3:4, :] + x1 * cw[2:3, :] + x2 * cw[1:2, :] + x3 * cw[0:1, :])
    q, k = [], []
    for h in range(GDN_QK_HEADS):
        qh = y[:, h * GDN_HEAD:(h + 1) * GDN_HEAD]
        kh = y[:, GDN_QK_DIM + h * GDN_HEAD:GDN_QK_DIM + (h + 1) * GDN_HEAD]
        q.append(qh * (lax.rsqrt(jnp.sum(qh * qh, axis=-1, keepdims=True) + GDN_L2_EPS)
                       * (GDN_HEAD ** -0.5)))
        k.append(kh * lax.rsqrt(jnp.sum(kh * kh, axis=-1, keepdims=True) + GDN_L2_EPS))
    g = -jnp.exp(a_log) * _softplus(ab + dt_bias)
    beta = _sigmoid(ab)
    g_cum = _mm_exact_lhs(tri, g)
    bc = lambda a, j: jnp.broadcast_to(a[:, j:j + 1], (a.shape[0], GDN_HEAD))
    g_b = jnp.concatenate([bc(g_cum, j) for j in range(GDN_V_HEADS)], axis=1)
    beta_b = jnp.concatenate([bc(beta, GDN_V_HEADS + j) for j in range(GDN_V_HEADS)], axis=1)
    return q, k, y[:, 2 * GDN_QK_DIM:], g_b, beta_b


def _gdn_chunk_units(units):
    c = CHUNK
    n2 = 2 * c
    st = lambda x: jnp.concatenate([x[:, :GDN_HEAD], x[:, GDN_HEAD:]], axis=0)
    i, j = _iota2((n2, n2))
    incl = ((i // c) == (j // c)) & (i >= j)
    strict = i > j

    g_c = [st(u[3]) for u in units]
    b_c = [st(u[4]) for u in units]
    k2 = [jnp.concatenate([u[1], u[1]], axis=0) for u in units]
    q2 = [jnp.concatenate([u[0], u[0]], axis=0) for u in units]
    kb = [a * b for a, b in zip(k2, b_c)]
    kq = [_mm_nt(jnp.concatenate([a, b], axis=0), kk) for a, b, kk in zip(kb, q2, k2)]
    gamma = [jnp.exp(jnp.where(incl, g - g.T, -jnp.inf)) for g in g_c]
    l_neg = [jnp.where(strict, -(x[:n2] * gm), 0.0) for x, gm in zip(kq, gamma)]
    a_qk = [(x[n2:] * gm).astype(BF16) for x, gm in zip(kq, gamma)]
    t_inv = _unit_lower_inverse(l_neg)
    e_g = [jnp.exp(g) for g in g_c]
    uw = [_mm(t, jnp.concatenate([st(u[2]) * b, kbi * eg], axis=1))
          for t, u, b, kbi, eg in zip(t_inv, units, b_c, kb, e_g)]
    qg = [qq * eg for qq, eg in zip(q2, e_g)]
    hd = lambda e: slice(e * GDN_HEAD, (e + 1) * GDN_HEAD)
    rw = lambda e: slice(e * c, (e + 1) * c)
    wq = [[_mm(jnp.concatenate([x[rw(e), GDN_HEAD:], qe[rw(e)]], axis=0), u[5][:, hd(e)])
           for x, qe, u in zip(uw, qg, units)] for e in range(2)]
    v_new = [jnp.concatenate([x[rw(0), :GDN_HEAD] - w0[:c], x[rw(1), :GDN_HEAD] - w1[:c]], axis=0)
             for x, w0, w1 in zip(uw, wq[0], wq[1])]
    o_s = [jnp.concatenate([w0[c:], w1[c:]], axis=0) + _mm(a, vn)
           for w0, w1, a, vn in zip(wq[0], wq[1], a_qk, v_new)]
    k_g, decay = [], []
    for g, u in zip(g_c, units):
        g_last = (g[c - 1:c, :], g[n2 - 1:n2, :])
        k_g.append([u[1] * jnp.exp(g_last[e] - g[rw(e)]) for e in range(2)])
        decay.append((jnp.exp(g_last[0]), jnp.exp(g_last[1])))
    ds = [[_mm_tn(kg[e], vn[rw(e)]) for kg, vn in zip(k_g, v_new)] for e in range(2)]
    s_new = [jnp.concatenate([u[5][:, hd(0)] * dc[0] + d0, u[5][:, hd(1)] * dc[1] + d1], axis=1)
             for u, dc, d0, d1 in zip(units, decay, ds[0], ds[1])]
    outs = [jnp.concatenate([o[:c], o[c:]], axis=1) for o in o_s]
    return outs, s_new


def _gdn_mix_kernel(qk_ref, v_ref, z_ref, ab_ref, cw_ref, alog_ref, dtb_ref, nw_ref,
                    o_ref, s_ref, tail_ref):
    bb, tt = qk_ref.shape[0], qk_ref.shape[1]
    pw = 2 * GDN_HEAD

    @pl.when(pl.program_id(1) == 0)
    def _():
        s_ref[...] = jnp.zeros(s_ref.shape, F32)
        tail_ref[...] = jnp.zeros(tail_ref.shape, F32)

    ids = [(bi, h) for bi in range(bb) for h in range(GDN_QK_HEADS)]
    ti, tj = _iota2((CHUNK, CHUNK))
    tri = (ti >= tj).astype(BF16)

    def body(ci, carry):
        rows = pl.ds(pl.multiple_of(ci * CHUNK, CHUNK), CHUNK)
        rowp = []
        for bi in range(bb):
            x = jnp.concatenate([qk_ref[bi, rows, :], v_ref[bi, rows, :]], axis=1)
            rowp.append(_gdn_chunk_prep(x, tail_ref[bi], ab_ref[bi, rows, :],
                                        cw_ref[...], alog_ref[...], dtb_ref[...], tri))
            tail_ref[bi] = x[CHUNK - SUBLANES:, :]
        units = []
        for n, (bi, h) in enumerate(ids):
            q, k, v, g_b, beta_b = rowp[bi]
            cv = slice(h * pw, (h + 1) * pw)
            units.append((q[h], k[h], v[:, cv], g_b[:, cv], beta_b[:, cv], s_ref[n]))
        outs, s_new = _gdn_chunk_units(units)
        for n, (bi, h) in enumerate(ids):
            s_ref[n] = s_new[n]
            for e in range(2):
                cols = slice(h * pw + e * GDN_HEAD, h * pw + (e + 1) * GDN_HEAD)
                oh = outs[n][:, e * GDN_HEAD:(e + 1) * GDN_HEAD]
                oh = oh * lax.rsqrt(jnp.mean(oh * oh, axis=-1, keepdims=True) + GDN_NORM_EPS) * nw_ref[...]
                o_ref[bi, rows, cols] = (oh * _silu(z_ref[bi, rows, cols])).astype(BF16)
        return carry

    lax.fori_loop(0, tt // CHUNK, body, 0, unroll=2)


def _gdn_mix(proj, lp, batch, seq, tt, bb):
    nt = seq // tt
    full = lambda b, t: (0, 0)
    qk_w = 2 * GDN_QK_DIM
    wide = lambda col: pl.BlockSpec((bb, tt, GDN_V_DIM), lambda b, t: (b, t, col // GDN_V_DIM))
    return pl.pallas_call(
        _gdn_mix_kernel,
        grid=(batch // bb, nt),
        in_specs=[wide(COL_GQK), wide(COL_GV), wide(COL_GZ),
                  pl.BlockSpec((bb, tt, LANES), lambda b, t: (b, t, COL_AB // LANES)),
                  pl.BlockSpec((GDN_CONV, qk_w + GDN_V_DIM), full),
                  pl.BlockSpec((1, LANES), full),
                  pl.BlockSpec((1, LANES), full),
                  pl.BlockSpec((1, GDN_HEAD), full)],
        out_specs=wide(0),
        out_shape=jax.ShapeDtypeStruct((batch, seq, GDN_V_DIM), BF16),
        scratch_shapes=[pltpu.VMEM((bb * GDN_QK_HEADS, GDN_HEAD, 2 * GDN_HEAD), F32),
                        pltpu.VMEM((bb, SUBLANES, qk_w + GDN_V_DIM), F32)],
        compiler_params=pltpu.CompilerParams(
            dimension_semantics=("parallel", "arbitrary"), vmem_limit_bytes=VMEM_LIMIT),
        name="gdn_mix",
    )(proj, proj, proj, proj, lp["conv_w"], lp["a_log"], lp["dt_bias"], lp["norm_w"])


def _pad_cols(a, n):
    return jnp.pad(a, [(0, 0)] * (a.ndim - 1) + [(0, n - a.shape[-1])])


def _proj_layout(main, vd):
    n_small = RWKV_DECAY_LORA + RWKV_AAA_LORA + RWKV_GATE_LORA
    rkv = main[..., :3 * RWKV_DIM]
    small = main[..., 3 * RWKV_DIM:3 * RWKV_DIM + n_small]
    gdn = main[..., 3 * RWKV_DIM + n_small:]
    zeros = lambda n: jnp.zeros(main.shape[:-1] + (n,), main.dtype)
    ab_pad = COL_WA - (COL_GQK + gdn.shape[-1] + vd.shape[-1])
    return jnp.concatenate([rkv, gdn, vd, zeros(ab_pad), small, zeros(N_PROJ - COL_WA - n_small)], axis=-1)


def kernel(x, attn_norm_w, w_in, rwkv_mu, rwkv_w0, rwkv_w_up, rwkv_a0, rwkv_a_up, rwkv_g_up,
           rwkv_k_k, rwkv_k_a, rwkv_r_k, rwkv_ln_w, rwkv_ln_b, vres_down, vres_mu, vres_up, vres_v0,
           gdn_conv_w, gdn_A_log, gdn_dt_bias, gdn_norm_w, w_out, ffn_norm_w, ffn_w_gate, ffn_w_up,
           ffn_w_down, final_norm_w):
    batch, seq, d = x.shape
    depth = w_in.shape[0]
    m = batch * seq

    gdn_pad = jnp.zeros((depth, w_in.shape[2] - rwkv_mu.shape[1]), F32)
    vmu = jnp.concatenate([jnp.zeros((1, RWKV_MV_LORA), F32), vres_mu], axis=0)
    mu_all = _proj_layout(jnp.concatenate([rwkv_mu, gdn_pad], axis=1), vmu)
    row_place = lambda a, before, total: jnp.pad(
        a, ((0, 0), (before, total - before - a.shape[1]), (0, 0))).astype(BF16)
    w_up = row_place(rwkv_w_up, 0, 128)
    a_up = row_place(rwkv_a_up, RWKV_DECAY_LORA, 128)
    g_up = row_place(rwkv_g_up, 0, 256)
    v_up = row_place(vres_up, AB_VD, 128)
    a_log = _pad_cols(gdn_A_log, LANES)
    dt_bias = _pad_cols(gdn_dt_bias, LANES)
    r_k = rwkv_r_k.reshape(depth, RWKV_DIM)
    vd_w = jnp.concatenate([jnp.zeros((1, d, RWKV_MV_LORA), BF16), vres_down.astype(BF16)], axis=0)
    w_proj = _proj_layout(w_in.astype(BF16), vd_w)
    w_out_b = w_out.astype(BF16)
    wg_b = ffn_w_gate.astype(BF16)
    wu_b = ffn_w_up.astype(BF16)
    wd_b = ffn_w_down.astype(BF16)

    tm = min(1024, m)
    row = lambda a, l: a[l][None, :]

    xf = x.reshape(m, d)
    v_first = None
    for l in range(depth):
        proj = _norm_matmul(xf, row(attn_norm_w, l), w_proj, l, tm, 1664)
        mu_l = mu_all[l]
        lp = dict(mu_rkv=mu_l[None, COL_RKV:COL_RKV + 3 * RWKV_DIM],
                  mu_sm=jnp.concatenate([mu_l[COL_WA:COL_WA + 128], mu_l[COL_GD:COL_GD + 256],
                                         mu_l[COL_AB:COL_AB + 128]])[None, :],
                  w0=row(rwkv_w0, l), w_up=w_up[l], a0=row(rwkv_a0, l), a_up=a_up[l], g_up=g_up[l],
                  k_k=row(rwkv_k_k, l), k_a=row(rwkv_k_a, l), r_k=row(r_k, l),
                  ln_w=row(rwkv_ln_w, l), ln_b=row(rwkv_ln_b, l),
                  conv_w=gdn_conv_w[l], a_log=row(a_log, l), dt_bias=row(dt_bias, l),
                  norm_w=row(gdn_norm_w, l))
        if l > 0:
            lp["v_up"] = v_up[l - 1]
            lp["v0"] = row(vres_v0, l - 1)
        as3 = lambda a: a.reshape(batch, seq, a.shape[-1])
        bb_a = 2 if batch % 2 == 0 else 1
        if l == 0:
            y_a, v_first = _rwkv_mix(as3(proj), None, lp, batch, seq, min(256, seq), bb_a)
        else:
            y_a = _rwkv_mix(as3(proj), v_first, lp, batch, seq, min(256, seq), bb_a)
        y_a = y_a.reshape(m, RWKV_DIM)
        bb_b = 4 if batch % 4 == 0 else bb_a
        y_b = _gdn_mix(as3(proj), lp, batch, seq, min(256, seq), bb_b).reshape(m, GDN_V_DIM)
        xf = _matmul2_res(y_a, y_b, w_out_b, l, xf, min(512, m), d)
        hmid = _norm_swiglu(xf, row(ffn_norm_w, l), wg_b, wu_b, l, tm, 512)
        if l + 1 < depth:
            xf = _matmul_res(hmid, wd_b, l, xf, min(512, m), 1024)
        else:
            xf = _matmul_res_norm(hmid, wd_b, l, xf, final_norm_w[None, :], min(256, m))
    return xf.reshape(batch, seq, d)
```
